```python
import jax
import jax.numpy as jnp
from jax import lax
import numpy as np


D_MODEL = 2048
BATCH = 1
SEQ = 8192
DEPTH = 4

ML_HEADS = 4
ML_DIM = D_MODEL // 16
ML_WIDTH = ML_HEADS * ML_DIM
ML_CONV = 3
FORGET_BIAS_LO = 3.0
FORGET_BIAS_HI = 6.0
NEG_INIT = -1e30
RET_HEADS = 4
RET_QK = D_MODEL // 32
RET_V = D_MODEL // 16
RET_WIDTH = RET_HEADS * RET_V
RET_DECAY_BASE = 5.0
MLA_HEADS = 8
MLA_NOPE = D_MODEL // 16
ROPE_DIM = D_MODEL // 32
MLA_V = D_MODEL // 16
MLA_Q_RANK = D_MODEL // 4
MLA_KV_RANK = D_MODEL // 8
MLA_WIDTH = MLA_HEADS * MLA_V
MIX_WIDTH = ML_WIDTH + RET_WIDTH + MLA_WIDTH
IN_SIZES = (ML_WIDTH, ML_WIDTH, ML_WIDTH, ML_WIDTH, 4 * ML_HEADS,
            RET_HEADS * RET_QK, RET_HEADS * RET_QK, RET_WIDTH, RET_WIDTH,
            MLA_Q_RANK, MLA_KV_RANK, ROPE_DIM)
IN_WIDTH = sum(IN_SIZES)
D_FF = 4 * D_MODEL
CHUNK = 128
Q_BLOCK = 128
ROPE_THETA = 10000.0
EPS = 1e-6

kernel_name = 'hybrid_mlstm_retention_mla_encoder'


def _split_points():
    pts, acc = [], 0
    for s in IN_SIZES[:-1]:
        acc += s
        pts.append(acc)
    return pts


def _rms(x, g):
    xf = x.astype(jnp.float32)
    y = xf * lax.rsqrt(jnp.mean(xf * xf, axis=-1, keepdims=True) + EPS)
    return (y * g.astype(jnp.float32)).astype(x.dtype)


def _head_rms(h, g):
    B, S, H, d = h.shape
    hf = h.astype(jnp.float32)
    y = hf * lax.rsqrt(jnp.mean(hf * hf, axis=-1, keepdims=True) + EPS)
    return y.reshape(B, S, H * d) * g.astype(jnp.float32)


def _rope_tables(positions):
    half = ROPE_DIM // 2
    inv = ROPE_THETA ** (-jnp.arange(half, dtype=jnp.float32) / half)
    ang = positions.astype(jnp.float32)[..., None] * inv
    return jnp.cos(ang)[:, :, None, :], jnp.sin(ang)[:, :, None, :]


def _apply_rope(x, cos, sin):
    x1, x2 = jnp.split(x.astype(jnp.float32), 2, axis=-1)
    return jnp.concatenate([x1 * cos - x2 * sin, x2 * cos + x1 * sin], axis=-1).astype(x.dtype)


def _centred_conv(x, w):
    K = w.shape[0]
    pad = K // 2
    S = x.shape[1]
    xp = jnp.pad(x, ((0, 0), (pad, pad), (0, 0)))
    return sum(xp[:, j:j + S] * w[j] for j in range(K))


def _mlstm_chunkwise(q, k, v, log_i, log_f):
    B, H, S, d = q.shape
    nc = S // CHUNK
    qc = (q * d ** -0.5).reshape(B, H, nc, CHUNK, d)
    kc = k.reshape(B, H, nc, CHUNK, d)
    vc = v.reshape(B, H, nc, CHUNK, d)
    lic = log_i.reshape(B, H, nc, CHUNK)
    b = jnp.cumsum(log_f.reshape(B, H, nc, CHUNK), axis=-1)
    b_last = b[..., -1]
    w_end = b_last[..., None] - b + lic
    m_loc = jnp.max(w_end, axis=-1)
    e_end = jnp.exp(w_end - m_loc[..., None])
    c_loc = jnp.einsum('bhnld,bhnle,bhnl->bhnde', kc, vc, e_end)
    n_loc = jnp.einsum('bhnld,bhnl->bhnd', kc, e_end)

    def step(carry, inp):
        c, n, m = carry
        c_l, n_l, m_l, b_l = inp
        m_new = jnp.maximum(b_l + m, m_l)
        a = jnp.exp(b_l + m - m_new)
        g = jnp.exp(m_l - m_new)
        c_new = a[..., None, None] * c + g[..., None, None] * c_l
        n_new = a[..., None] * n + g[..., None] * n_l
        return (c_new, n_new, m_new), (c, n, m)

    init = (jnp.zeros((B, H, d, d), jnp.float32), jnp.zeros((B, H, d), jnp.float32),
            jnp.full((B, H), NEG_INIT, jnp.float32))
    xs = (jnp.moveaxis(c_loc, 2, 0), jnp.moveaxis(n_loc, 2, 0),
          jnp.moveaxis(m_loc, 2, 0), jnp.moveaxis(b_last, 2, 0))
    _, (c_st, n_st, m_st) = lax.scan(step, init, xs)
    c_st = jnp.moveaxis(c_st, 0, 2)
    n_st = jnp.moveaxis(n_st, 0, 2)
    m_st = jnp.moveaxis(m_st, 0, 2)

    mask = jnp.tril(jnp.ones((CHUNK, CHUNK), dtype=bool))
    d_log = jnp.where(mask, b[..., :, None] - b[..., None, :] + lic[..., None, :], -jnp.inf)
    inter_log = b + m_st[..., None]
    m_t = jnp.maximum(jnp.max(d_log, axis=-1), inter_log)
    w_intra = jnp.exp(d_log - m_t[..., None])
    w_inter = jnp.exp(inter_log - m_t)
    s = jnp.einsum('bhnld,bhnsd->bhnls', qc, kc) * w_intra
    num = (jnp.einsum('bhnls,bhnse->bhnle', s, vc)
           + w_inter[..., None] * jnp.einsum('bhnld,bhnde->bhnle', qc, c_st))
    den = jnp.sum(s, axis=-1) + w_inter * jnp.einsum('bhnld,bhnd->bhnl', qc, n_st)
    h = num / jnp.maximum(jnp.abs(den), jnp.exp(-m_t))[..., None]
    return h.reshape(B, H, S, d)


def _retention_chunkwise(q, k, v, log_gamma):
    B, H, S, dk = q.shape
    dv = v.shape[-1]
    nc = S // CHUNK
    qc = q.reshape(B, H, nc, CHUNK, dk)
    kc = k.reshape(B, H, nc, CHUNK, dk)
    vc = v.reshape(B, H, nc, CHUNK, dv)
    idx = jnp.arange(CHUNK, dtype=jnp.float32)
    diff = idx[:, None] - idx[None, :]
    lower = diff >= 0
    decay = jnp.where(lower[None], jnp.exp(jnp.where(lower, diff, 0.0)[None] * log_gamma[:, None, None]), 0.0)
    scores = jnp.einsum('bhnld,bhnsd->bhnls', qc, kc) * decay[None, :, None]
    y_intra = jnp.einsum('bhnls,bhnse->bhnle', scores, vc)
    zeta = jnp.exp((CHUNK - 1.0 - idx)[None, :] * log_gamma[:, None])
    r_loc = jnp.einsum('bhnld,bhnle,hl->bhnde', kc, vc, zeta)
    chunk_decay = jnp.exp(CHUNK * log_gamma)[None, :, None, None]

    def step(r, r_l):
        return chunk_decay * r + r_l, r

    _, r_start = lax.scan(step, jnp.zeros((B, H, dk, dv), jnp.float32), jnp.moveaxis(r_loc, 2, 0))
    r_start = jnp.moveaxis(r_start, 0, 2)
    inner = jnp.exp((idx + 1.0)[None, :] * log_gamma[:, None])
    y_inter = jnp.einsum('bhnld,bhnde->bhnle', qc, r_start) * inner[None, :, None, :, None]
    return (y_intra + y_inter).reshape(B, H, S, dv)


def _mlstm_group(q, k, v, o, gates, b_gates, w_conv, g_out):
    B, S, _ = q.shape
    qk = jax.nn.silu(_centred_conv(jnp.concatenate([q, k], axis=-1), w_conv))
    q, k = jnp.split(qk, 2, axis=-1)

    def to_heads(t):
        return t.astype(jnp.float32).reshape(B, S, ML_HEADS, ML_DIM).transpose(0, 2, 1, 3)

    qh, kh, vh = to_heads(q), to_heads(k), to_heads(v)
    gp = (gates + b_gates).astype(jnp.float32).reshape(B, S, 4, ML_HEADS).transpose(2, 0, 3, 1)
    i_fwd, f_fwd, i_bwd, f_bwd = gp[0], gp[1], gp[2], gp[3]

    def flip(t):
        return jnp.flip(t, axis=2)

    h_fwd = _mlstm_chunkwise(qh, kh, vh, i_fwd, jax.nn.log_sigmoid(f_fwd))
    h_bwd = flip(_mlstm_chunkwise(flip(qh), flip(kh), flip(vh), flip(i_bwd), jax.nn.log_sigmoid(flip(f_bwd))))
    h = (h_fwd + h_bwd).transpose(0, 2, 1, 3)
    return (jax.nn.sigmoid(o.astype(jnp.float32)) * _head_rms(h, g_out)).astype(v.dtype)


def _retention_group(q, k, v, g, cos, sin, g_out):
    B, S, _ = q.shape
    qh = _apply_rope(q.reshape(B, S, RET_HEADS, RET_QK), cos, sin).astype(jnp.float32).transpose(0, 2, 1, 3)
    kh = (_apply_rope(k.reshape(B, S, RET_HEADS, RET_QK), cos, sin).astype(jnp.float32)
          * RET_QK ** -0.5).transpose(0, 2, 1, 3)
    vh = v.astype(jnp.float32).reshape(B, S, RET_HEADS, RET_V).transpose(0, 2, 1, 3)
    log_gamma = jnp.log1p(-jnp.exp2(-RET_DECAY_BASE - jnp.arange(RET_HEADS, dtype=jnp.float32)))

    def flip(t):
        return jnp.flip(t, axis=2)

    y = (_retention_chunkwise(qh, kh, vh, log_gamma)
         + flip(_retention_chunkwise(flip(qh), flip(kh), flip(vh), log_gamma[::-1])))
    y = y.transpose(0, 2, 1, 3)
    return (jax.nn.silu(g.astype(jnp.float32)) * _head_rms(y, g_out)).astype(v.dtype)


def _mla_group(c_q, c_kv, k_rope, cos, sin, g_q_norm, w_q_up, g_kv_norm, w_kv_up):
    B, S, _ = c_q.shape
    q = (_rms(c_q, g_q_norm) @ w_q_up).reshape(B, S, MLA_HEADS, MLA_NOPE + ROPE_DIM)
    q = jnp.concatenate([q[..., :MLA_NOPE], _apply_rope(q[..., MLA_NOPE:], cos, sin)], axis=-1)
    kv = (_rms(c_kv, g_kv_norm) @ w_kv_up).reshape(B, S, MLA_HEADS, MLA_NOPE + MLA_V)
    k_nope, v = kv[..., :MLA_NOPE], kv[..., MLA_NOPE:]
    k_r = _apply_rope(k_rope[:, :, None, :], cos, sin)
    k = jnp.concatenate([k_nope, jnp.broadcast_to(k_r, (B, S, MLA_HEADS, ROPE_DIM))], axis=-1)
    scale = (MLA_NOPE + ROPE_DIM) ** -0.5
    nb = S // Q_BLOCK
    q_blocks = q.reshape(B, nb, Q_BLOCK, MLA_HEADS, MLA_NOPE + ROPE_DIM).transpose(1, 0, 2, 3, 4)

    def attend(qb):
        s = jnp.einsum('bqhd,bkhd->bhqk', qb, k).astype(jnp.float32) * scale
        p = jax.nn.softmax(s, axis=-1)
        return jnp.einsum('bhqk,bkhd->bqhd', p.astype(v.dtype), v)

    o = lax.map(attend, q_blocks)
    return o.transpose(1, 0, 2, 3, 4).reshape(B, S, MLA_WIDTH)


def _mixer(h, cos, sin, w_in, b_gates, w_conv, g_ml_out, g_ret_out, g_q_norm, w_q_up,
           g_kv_norm, w_kv_up, w_out):
    proj = h @ w_in
    (ml_q, ml_k, ml_v, ml_o, ml_gates, r_q, r_k, r_v, r_g,
     c_q, c_kv, k_rope) = jnp.split(proj, _split_points(), axis=-1)
    y_ml = _mlstm_group(ml_q, ml_k, ml_v, ml_o, ml_gates, b_gates, w_conv, g_ml_out)
    y_ret = _retention_group(r_q, r_k, r_v, r_g, cos, sin, g_ret_out)
    y_mla = _mla_group(c_q, c_kv, k_rope, cos, sin, g_q_norm, w_q_up, g_kv_norm, w_kv_up)
    y = jnp.concatenate([y_ml.astype(h.dtype), y_ret.astype(h.dtype), y_mla.astype(h.dtype)], axis=-1)
    return y @ w_out


def setup_inputs(seed: int = 0) -> dict:
    key = jax.random.key(seed)
    ks = jax.random.split(key, 20)

    def nrm(k, shape, scale):
        return jax.random.normal(k, shape, jnp.float32) * scale

    def gain(k, shape):
        return 1.0 + 0.02 * jax.random.normal(k, shape, jnp.float32)

    x = nrm(ks[0], (BATCH, SEQ, D_MODEL), 1.0)
    positions = (jax.random.randint(ks[1], (BATCH, 1), 0, 1024, dtype=jnp.int32)
                 + jnp.arange(SEQ, dtype=jnp.int32)[None, :]).astype(jnp.int32)
    forget = jnp.linspace(FORGET_BIAS_LO, FORGET_BIAS_HI, ML_HEADS, dtype=jnp.float32)
    zeros = jnp.zeros((ML_HEADS,), jnp.float32)
    b_gates = jnp.concatenate([zeros, forget, zeros, forget])[None, :] + nrm(ks[2], (DEPTH, 4 * ML_HEADS), 0.1)
    return {
        'x': x,
        'positions': positions,
        'g_mix': gain(ks[3], (DEPTH, D_MODEL)),
        'w_in': nrm(ks[4], (DEPTH, D_MODEL, IN_WIDTH), D_MODEL ** -0.5),
        'b_gates': b_gates,
        'w_conv': nrm(ks[5], (DEPTH, ML_CONV, 2 * ML_WIDTH), ML_CONV ** -0.5),
        'g_ml_out': gain(ks[6], (DEPTH, ML_WIDTH)),
        'g_ret_out': gain(ks[7], (DEPTH, RET_WIDTH)),
        'g_q_norm': gain(ks[8], (DEPTH, MLA_Q_RANK)),
        'w_q_up': nrm(ks[9], (DEPTH, MLA_Q_RANK, MLA_HEADS * (MLA_NOPE + ROPE_DIM)), MLA_Q_RANK ** -0.5),
        'g_kv_norm': gain(ks[10], (DEPTH, MLA_KV_RANK)),
        'w_kv_up': nrm(ks[11], (DEPTH, MLA_KV_RANK, MLA_HEADS * (MLA_NOPE + MLA_V)), MLA_KV_RANK ** -0.5),
        'w_out': nrm(ks[12], (DEPTH, MIX_WIDTH, D_MODEL), MIX_WIDTH ** -0.5),
        'g_ffn': gain(ks[13], (DEPTH, D_MODEL)),
        'w_ff1': nrm(ks[14], (DEPTH, D_MODEL, D_FF), D_MODEL ** -0.5),
        'w_ff2': nrm(ks[15], (DEPTH, D_FF, D_MODEL), D_FF ** -0.5),
        'g_final': gain(ks[16], (D_MODEL,)),
    }


def reference(x, positions, g_mix, w_in, b_gates, w_conv, g_ml_out, g_ret_out, g_q_norm, w_q_up,
              g_kv_norm, w_kv_up, w_out, g_ffn, w_ff1, w_ff2, g_final):
    cos, sin = _rope_tables(positions)
    for l in range(DEPTH):
        h = _rms(x, g_mix[l])
        x = x + _mixer(h, cos, sin, w_in[l], b_gates[l], w_conv[l], g_ml_out[l], g_ret_out[l],
                       g_q_norm[l], w_q_up[l], g_kv_norm[l], w_kv_up[l], w_out[l])
        u = _rms(x, g_ffn[l])
        x = x + jnp.square(jax.nn.relu(u @ w_ff1[l])) @ w_ff2[l]
    return _rms(x, g_final)
```

```python
import functools

import numpy as np
import jax
import jax.numpy as jnp
from jax import lax
from jax.experimental import pallas as pl
from jax.experimental.pallas import tpu as pltpu

F32 = jnp.float32
BF16 = jnp.bfloat16

EPS = 1e-6
NEG_INIT = -1e30
ROPE_THETA = 10000.0
CHUNK = 128
LANE = 128
ML_HEADS = 4
ML_DIM = 128
RET_HEADS = 4
RET_QK = 64
RET_V = 128
RET_DECAY_BASE = 5.0
MLA_HEADS = 8
MLA_NOPE = 128
ROPE_DIM = 64
MLA_V = 128
MLA_QK = MLA_NOPE + ROPE_DIM
VMEM_LIMIT = 56 * 1024 * 1024

COL_ML_Q, COL_ML_K, COL_ML_V, COL_ML_O = 0, 4, 8, 12
COL_RQK, COL_RV, COL_RG = 16, 20, 24
COL_CQ, COL_CKV, COL_KROPE, COL_GATES = 28, 32, 34, 35
IN_WIDTH_PACKED = 36 * LANE

_NT = (((1,), (1,)), ((), ()))
_TN = (((0,), (0,)), ((), ()))


def _params(sem):
    return pltpu.CompilerParams(dimension_semantics=sem, vmem_limit_bytes=VMEM_LIMIT)


def _rms_matmul_kernel(x_ref, g_ref, w_ref, o_ref, hn_ref, *, act):
    @pl.when(pl.program_id(1) == 0)
    def _():
        x = x_ref[...]
        ms = jnp.mean(x * x, axis=-1, keepdims=True)
        hn_ref[...] = (x * lax.rsqrt(ms + EPS) * g_ref[...]).astype(hn_ref.dtype)

    y = jnp.dot(hn_ref[...], w_ref[...], preferred_element_type=F32)
    if act:
        y = jnp.square(jnp.maximum(y, 0.0))
    o_ref[...] = y.astype(o_ref.dtype)


def _rms_matmul(x, g, w, layer, *, act, out_dtype, tm, tn):
    S, K = x.shape
    N = w.shape[-1]
    return pl.pallas_call(
        functools.partial(_rms_matmul_kernel, act=act),
        name="rms_matmul_act" if act else "rms_matmul",
        grid=(S // tm, N // tn),
        in_specs=[
            pl.BlockSpec((tm, K), lambda i, j: (i, 0)),
            pl.BlockSpec((None, 1, K), lambda i, j: (layer, 0, 0)),
            pl.BlockSpec((None, K, tn), lambda i, j: (layer, 0, j)),
        ],
        out_specs=pl.BlockSpec((tm, tn), lambda i, j: (i, j)),
        out_shape=jax.ShapeDtypeStruct((S, N), out_dtype),
        scratch_shapes=[pltpu.VMEM((tm, K), BF16)],
        compiler_params=_params(("parallel", "arbitrary")),
    )(x, g, w)


def _mm_res_kernel(*refs, n_pairs):
    a_refs = refs[:n_pairs]
    w_refs = refs[n_pairs:2 * n_pairs]
    x_ref, o_ref = refs[2 * n_pairs], refs[2 * n_pairs + 1]
    acc = x_ref[...]
    for a_ref, w_ref in zip(a_refs, w_refs):
        acc = acc + jnp.dot(a_ref[...], w_ref[...], preferred_element_type=F32)
    o_ref[...] = acc


def _mm_res(acts, w, layer, x, *, tm, tn):
    S, N = x.shape
    n_pairs = len(acts)
    a_specs, w_specs = [], []
    row = 0
    for a in acts:
        k = a.shape[1]
        assert row % k == 0
        a_specs.append(pl.BlockSpec((tm, k), lambda i, j: (i, 0)))
        w_specs.append(pl.BlockSpec((None, k, tn), lambda i, j, rb=row // k: (layer, rb, j)))
        row += k
    assert row == w.shape[1]
    return pl.pallas_call(
        functools.partial(_mm_res_kernel, n_pairs=n_pairs),
        name="mm_res%d" % n_pairs,
        grid=(S // tm, N // tn),
        in_specs=a_specs + w_specs + [pl.BlockSpec((tm, tn), lambda i, j: (i, j))],
        out_specs=pl.BlockSpec((tm, tn), lambda i, j: (i, j)),
        out_shape=jax.ShapeDtypeStruct((S, N), F32),
        compiler_params=_params(("parallel", "arbitrary")),
    )(*acts, *([w] * n_pairs), x)


def _rms_kernel(x_ref, g_ref, o_ref):
    x = x_ref[...]
    ms = jnp.mean(x * x, axis=-1, keepdims=True)
    o_ref[...] = x * lax.rsqrt(ms + EPS) * g_ref[...]


def _rms(x, g, *, tm):
    S, K = x.shape
    return pl.pallas_call(
        _rms_kernel,
        name="final_rms",
        grid=(S // tm,),
        in_specs=[pl.BlockSpec((tm, K), lambda i: (i, 0)), pl.BlockSpec((1, K), lambda i: (0, 0))],
        out_specs=pl.BlockSpec((tm, K), lambda i: (i, 0)),
        out_shape=jax.ShapeDtypeStruct((S, K), F32),
        compiler_params=_params(("parallel",)),
    )(x, g)


def _prep_kernel(qk_ref, prev_ref, next_ref, wc_ref, rqk_ref, cos_ref, sin_ref, qk_out, rqk_out, *, tm, nblk):
    i = pl.program_id(0)
    x = qk_ref[...]
    row = lax.broadcasted_iota(jnp.int32, x.shape, 0)
    prev_row = jnp.where(i > 0, prev_ref[7:8, :], 0.0)
    next_row = jnp.where(i < nblk - 1, next_ref[0:1, :], 0.0)
    x_m = jnp.where(row == 0, prev_row, pltpu.roll(x, 1, 0))
    x_p = jnp.where(row == tm - 1, next_row, pltpu.roll(x, tm - 1, 0))
    w = wc_ref[...]
    y = x_m * w[0:1, :] + x * w[1:2, :] + x_p * w[2:3, :]
    y = y / (1.0 + jnp.exp(-y))
    half = ML_HEADS * ML_DIM
    qk_out[:, :half] = (y[:, :half] * (ML_DIM ** -0.5)).astype(qk_out.dtype)
    qk_out[:, half:] = y[:, half:].astype(qk_out.dtype)

    cosf = cos_ref[...]
    sinf = sin_ref[...]
    width = RET_HEADS * RET_QK
    lane = lax.broadcasted_iota(jnp.int32, (tm, width), 1)
    first_half = (lane % RET_QK) < (RET_QK // 2)

    def rope(t):
        partner = jnp.where(first_half, pltpu.roll(t, width - RET_QK // 2, 1), pltpu.roll(t, RET_QK // 2, 1))
        return t * cosf + partner * sinf

    r = rqk_ref[...]
    rqk_out[:, :width] = rope(r[:, :width]).astype(rqk_out.dtype)
    rqk_out[:, width:] = (rope(r[:, width:]) * (RET_QK ** -0.5)).astype(rqk_out.dtype)


def _prep(proj, w_conv, layer, cos4, sin4, *, tm):
    S = proj.shape[0]
    nblk = S // tm
    qk_w = 2 * ML_HEADS * ML_DIM
    rqk_w = 2 * RET_HEADS * RET_QK
    return pl.pallas_call(
        functools.partial(_prep_kernel, tm=tm, nblk=nblk),
        name="prep",
        grid=(nblk,),
        in_specs=[
            pl.BlockSpec((tm, qk_w), lambda i: (i, 0)),
            pl.BlockSpec((8, qk_w), lambda i: (jnp.maximum(i * (tm // 8) - 1, 0), 0)),
            pl.BlockSpec((8, qk_w), lambda i: (jnp.minimum((i + 1) * (tm // 8), S // 8 - 1), 0)),
            pl.BlockSpec((None, 3, qk_w), lambda i: (layer, 0, 0)),
            pl.BlockSpec((tm, rqk_w), lambda i: (i, COL_RQK * LANE // rqk_w)),
            pl.BlockSpec((tm, rqk_w // 2), lambda i: (i, 0)),
            pl.BlockSpec((tm, rqk_w // 2), lambda i: (i, 0)),
        ],
        out_specs=[pl.BlockSpec((tm, qk_w), lambda i: (i, 0)), pl.BlockSpec((tm, rqk_w), lambda i: (i, 0))],
        out_shape=[jax.ShapeDtypeStruct((S, qk_w), BF16), jax.ShapeDtypeStruct((S, rqk_w), BF16)],
        compiler_params=_params(("parallel",)),
    )(proj, proj, proj, w_conv, proj, cos4, sin4)


def _log_sigmoid(t):
    return jnp.minimum(t, 0.0) - jnp.log1p(jnp.exp(-jnp.abs(t)))


def _mlstm_kernel(qf, kf, vf, gf, qb, kb, vb, gb, bias_ref, tri_ref, hf_ref, hb_ref, c_ref, m_ref):
    @pl.when(pl.program_id(0) == 0)
    def _():
        c_ref[...] = jnp.zeros(c_ref.shape, F32)
        m_ref[...] = jnp.full(m_ref.shape, NEG_INIT, F32)

    L = CHUNK
    row = lax.broadcasted_iota(jnp.int32, (L, L), 0)
    col = lax.broadcasted_iota(jnp.int32, (L, L), 1)
    ones = jnp.ones((L, ML_DIM), BF16)
    dirs = ((qf, kf, vf, gf, hf_ref), (qb, kb, vb, gb, hb_ref))
    for d, (q_ref, k_ref, v_ref, g_ref, h_ref) in enumerate(dirs):
        g = g_ref[...] + bias_ref[...]
        logf = _log_sigmoid(g)
        g_t = g.T
        logf_t = logf.T
        tri_col = tri_ref[d]
        tri_row = tri_ref[1 - d]
        b_cols = jnp.dot(tri_col, logf, precision=lax.Precision.HIGHEST, preferred_element_type=F32)
        b_rows = jnp.dot(logf_t[0:16, :], tri_row, precision=lax.Precision.HIGHEST, preferred_element_type=F32)
        mask = (row >= col) if d == 0 else (row <= col)
        for h in range(ML_HEADS):
            ci, cf, st = 8 * d + h, 8 * d + 4 + h, ML_HEADS * d + h
            sl = slice(h * ML_DIM, (h + 1) * ML_DIM)
            bcol = b_cols[:, cf:cf + 1]
            brow = b_rows[cf:cf + 1, :]
            irow = g_t[ci:ci + 1, :]
            icol = g[:, ci:ci + 1]
            blast = brow[:, L - 1:L] if d == 0 else brow[:, 0:1]
            m_st = m_ref[st:st + 1, 0:1]
            c_st = c_ref[st]
            q = q_ref[:, sl]
            k = k_ref[:, sl]
            v1 = jnp.concatenate([v_ref[:, sl].astype(BF16), ones], axis=1)

            dlog = jnp.where(mask, bcol - brow + irow, -jnp.inf)
            inter_log = bcol + m_st
            m_t = jnp.maximum(jnp.max(dlog, axis=1, keepdims=True), inter_log)
            w_intra = jnp.exp(dlog - m_t)
            w_inter = jnp.exp(inter_log - m_t)
            s = lax.dot_general(q, k, _NT, preferred_element_type=F32) * w_intra
            tot = (jnp.dot(s.astype(BF16), v1, preferred_element_type=F32)
                   + w_inter * jnp.dot(q, c_st.astype(BF16), preferred_element_type=F32))
            num = tot[:, :ML_DIM]
            den = tot[:, ML_DIM:]
            h_ref[:, sl] = num / jnp.maximum(jnp.abs(den), jnp.exp(-m_t))

            wend = blast - bcol + icol
            m_loc = jnp.max(wend, axis=0, keepdims=True)
            kw = (k.astype(F32) * jnp.exp(wend - m_loc)).astype(BF16)
            c_loc = lax.dot_general(kw, v1, _TN, preferred_element_type=F32)
            m_new = jnp.maximum(blast + m_st, m_loc)
            c_ref[st] = jnp.exp(blast + m_st - m_new) * c_st + jnp.exp(m_loc - m_new) * c_loc
            m_ref[st:st + 1, :] = jnp.broadcast_to(m_new, (1, LANE))


def _mlstm(qk, proj, bias, tri):
    S = proj.shape[0]
    nc = S // CHUNK
    W = ML_HEADS * ML_DIM
    fwd = lambda n: n
    bwd = lambda n: nc - 1 - n

    def specs(idx):
        return [
            pl.BlockSpec((CHUNK, W), lambda n: (idx(n), 0)),
            pl.BlockSpec((CHUNK, W), lambda n: (idx(n), 1)),
            pl.BlockSpec((CHUNK, W), lambda n: (idx(n), COL_ML_V * LANE // W)),
            pl.BlockSpec((CHUNK, LANE), lambda n: (idx(n), COL_GATES)),
        ]

    return pl.pallas_call(
        _mlstm_kernel,
        name="mlstm",
        grid=(nc,),
        in_specs=specs(fwd) + specs(bwd) + [
            pl.BlockSpec((1, LANE), lambda n: (0, 0)),
            pl.BlockSpec((2, CHUNK, CHUNK), lambda n: (0, 0, 0)),
        ],
        out_specs=[pl.BlockSpec((CHUNK, W), lambda n: (fwd(n), 0)), pl.BlockSpec((CHUNK, W), lambda n: (bwd(n), 0))],
        out_shape=[jax.ShapeDtypeStruct((S, W), F32), jax.ShapeDtypeStruct((S, W), F32)],
        scratch_shapes=[pltpu.VMEM((2 * ML_HEADS, ML_DIM, 2 * ML_DIM), F32), pltpu.VMEM((2 * ML_HEADS, LANE), F32)],
        compiler_params=_params(("arbitrary",)),
    )(qk, qk, proj, proj, qk, qk, proj, proj, bias, tri)


def _retention_kernel(qkf, vf, qkb, vb, decay_ref, inner_ref, zeta_ref, yf_ref, yb_ref, r_ref, *, chunk_decay):
    @pl.when(pl.program_id(0) == 0)
    def _():
        r_ref[...] = jnp.zeros(r_ref.shape, F32)

    kofs = RET_HEADS * RET_QK
    for d, (qk_ref, v_ref, y_ref) in enumerate(((qkf, vf, yf_ref), (qkb, vb, yb_ref))):
        for h in range(RET_HEADS):
            st = RET_HEADS * d + h
            q = qk_ref[:, h * RET_QK:(h + 1) * RET_QK]
            k = qk_ref[:, kofs + h * RET_QK:kofs + (h + 1) * RET_QK]
            v = v_ref[:, h * RET_V:(h + 1) * RET_V].astype(BF16)
            r_st = r_ref[st]
            s = lax.dot_general(q, k, _NT, preferred_element_type=F32) * decay_ref[st]
            y = (jnp.dot(s.astype(BF16), v, preferred_element_type=F32)
                 + jnp.dot(q, r_st.astype(BF16), preferred_element_type=F32) * inner_ref[st])
            y_ref[:, h * RET_V:(h + 1) * RET_V] = y
            kw = (k.astype(F32) * zeta_ref[st][:, :RET_QK]).astype(BF16)
            r_ref[st] = chunk_decay[st] * r_st + lax.dot_general(kw, v, _TN, preferred_element_type=F32)


def _retention_consts():
    log_gamma = jnp.log1p(-jnp.exp2(-RET_DECAY_BASE - jnp.arange(RET_HEADS, dtype=F32)))
    idx = jnp.arange(CHUNK, dtype=F32)
    diff = idx[:, None] - idx[None, :]
    decay, inner, zeta = [], [], []
    for d in range(2):
        lg_d = log_gamma if d == 0 else log_gamma[::-1]
        dd = diff if d == 0 else -diff
        pos = idx if d == 0 else (CHUNK - 1.0 - idx)
        keep = dd >= 0
        for h in range(RET_HEADS):
            lg = lg_d[h]
            decay.append(jnp.where(keep, jnp.exp(jnp.where(keep, dd, 0.0) * lg), 0.0))
            inner.append(jnp.broadcast_to(jnp.exp((pos + 1.0) * lg)[:, None], (CHUNK, LANE)))
            zeta.append(jnp.broadcast_to(jnp.exp((CHUNK - 1.0 - pos) * lg)[:, None], (CHUNK, LANE)))
    lg_np = np.log1p(-np.exp2(-RET_DECAY_BASE - np.arange(RET_HEADS, dtype=np.float32))).astype(np.float32)
    chunk_decay = tuple(float(np.exp(np.float32(CHUNK) * lg)) for lg in list(lg_np) + list(lg_np[::-1]))
    return jnp.stack(decay), jnp.stack(inner), jnp.stack(zeta), chunk_decay


def _retention(rqk, proj, consts):
    S = proj.shape[0]
    nc = S // CHUNK
    decay, inner, zeta, chunk_decay = consts
    QW = 2 * RET_HEADS * RET_QK
    VW = RET_HEADS * RET_V
    fwd = lambda n: n
    bwd = lambda n: nc - 1 - n

    def specs(idx):
        return [
            pl.BlockSpec((CHUNK, QW), lambda n: (idx(n), 0)),
            pl.BlockSpec((CHUNK, VW), lambda n: (idx(n), COL_RV * LANE // VW)),
        ]

    const_spec = pl.BlockSpec((2 * RET_HEADS, CHUNK, LANE), lambda n: (0, 0, 0))
    return pl.pallas_call(
        functools.partial(_retention_kernel, chunk_decay=chunk_decay),
        name="retention",
        grid=(nc,),
        in_specs=specs(fwd) + specs(bwd) + [const_spec, const_spec, const_spec],
        out_specs=[pl.BlockSpec((CHUNK, VW), lambda n: (fwd(n), 0)), pl.BlockSpec((CHUNK, VW), lambda n: (bwd(n), 0))],
        out_shape=[jax.ShapeDtypeStruct((S, VW), F32), jax.ShapeDtypeStruct((S, VW), F32)],
        scratch_shapes=[pltpu.VMEM((2 * RET_HEADS, RET_QK, RET_V), F32)],
        compiler_params=_params(("arbitrary",)),
    )(rqk, proj, rqk, proj, decay, inner, zeta)


def _head_rms(t, g):
    return t * lax.rsqrt(jnp.mean(t * t, axis=-1, keepdims=True) + EPS) * g


def _combine_kernel(hf, hb, o_ref, gml_ref, yf, yb, rg_ref, gret_ref, yml_out, yret_out):
    for h in range(ML_HEADS):
        sl = slice(h * ML_DIM, (h + 1) * ML_DIM)
        y = _head_rms(hf[:, sl] + hb[:, sl], gml_ref[:, sl])
        yml_out[:, sl] = (y / (1.0 + jnp.exp(-o_ref[:, sl]))).astype(yml_out.dtype)
    for h in range(RET_HEADS):
        sl = slice(h * RET_V, (h + 1) * RET_V)
        y = _head_rms(yf[:, sl] + yb[:, sl], gret_ref[:, sl])
        gate = rg_ref[:, sl]
        yret_out[:, sl] = (gate / (1.0 + jnp.exp(-gate)) * y).astype(yret_out.dtype)


def _combine(h_f, h_b, y_f, y_b, proj, g_ml, g_ret, layer, *, tm):
    S = proj.shape[0]
    W = ML_HEADS * ML_DIM
    blk = lambda c: pl.BlockSpec((tm, W), lambda i: (i, c))
    gain = pl.BlockSpec((None, 1, W), lambda i: (layer, 0, 0))
    return pl.pallas_call(
        _combine_kernel,
        name="combine",
        grid=(S // tm,),
        in_specs=[blk(0), blk(0), blk(COL_ML_O * LANE // W), gain, blk(0), blk(0), blk(COL_RG * LANE // W), gain],
        out_specs=[blk(0), blk(0)],
        out_shape=[jax.ShapeDtypeStruct((S, W), BF16), jax.ShapeDtypeStruct((S, W), BF16)],
        compiler_params=_params(("parallel",)),
    )(h_f, h_b, proj, g_ml, y_f, y_b, proj, g_ret)


def _swap_halves(t):
    half = t.shape[1] // 2
    return jnp.concatenate([t[:, half:], t[:, :half]], axis=1)


def _mla_prep_kernel(cq_ref, ckv_ref, kr_ref, gq_ref, gkv_ref, wq_ref, wkv_ref, cos_ref, sin_ref,
                     q_out, k_out, v_out, qn_ref, kvn_ref, krr_ref):
    cosf = cos_ref[...]
    sinf = sin_ref[...]

    @pl.when(pl.program_id(1) == 0)
    def _():
        cq = cq_ref[...]
        qn_ref[...] = (cq * lax.rsqrt(jnp.mean(cq * cq, axis=-1, keepdims=True) + EPS) * gq_ref[...]).astype(BF16)
        ckv = ckv_ref[...]
        kvn_ref[...] = (ckv * lax.rsqrt(jnp.mean(ckv * ckv, axis=-1, keepdims=True) + EPS) * gkv_ref[...]).astype(BF16)
        kr = kr_ref[:, :ROPE_DIM]
        krr_ref[...] = (kr * cosf + _swap_halves(kr) * sinf).astype(BF16)

    scale = MLA_QK ** -0.5
    q = jnp.dot(qn_ref[...], wq_ref[...], preferred_element_type=F32)
    q_rope = q[:, MLA_NOPE:]
    q_out[:, :MLA_NOPE] = (q[:, :MLA_NOPE] * scale).astype(BF16)
    q_out[:, MLA_NOPE:] = ((q_rope * cosf + _swap_halves(q_rope) * sinf) * scale).astype(BF16)
    kv = jnp.dot(kvn_ref[...], wkv_ref[...], preferred_element_type=F32)
    k_out[:, :MLA_NOPE] = kv[:, :MLA_NOPE].astype(BF16)
    k_out[:, MLA_NOPE:] = krr_ref[...]
    v_out[...] = kv[:, MLA_NOPE:].astype(BF16)


def _mla_prep(proj, g_q, g_kv, w_q, w_kv, layer, cos1, sin1, *, tm):
    S = proj.shape[0]
    q_rank, kv_rank = w_q.shape[2], w_kv.shape[2]
    return pl.pallas_call(
        _mla_prep_kernel,
        name="mla_prep",
        grid=(S // tm, MLA_HEADS),
        in_specs=[
            pl.BlockSpec((tm, q_rank), lambda i, h: (i, COL_CQ * LANE // q_rank)),
            pl.BlockSpec((tm, kv_rank), lambda i, h: (i, COL_CKV * LANE // kv_rank)),
            pl.BlockSpec((tm, LANE), lambda i, h: (i, COL_KROPE)),
            pl.BlockSpec((None, 1, q_rank), lambda i, h: (layer, 0, 0)),
            pl.BlockSpec((None, 1, kv_rank), lambda i, h: (layer, 0, 0)),
            pl.BlockSpec((None, None, q_rank, MLA_QK), lambda i, h: (layer, h, 0, 0)),
            pl.BlockSpec((None, None, kv_rank, MLA_NOPE + MLA_V), lambda i, h: (layer, h, 0, 0)),
            pl.BlockSpec((tm, ROPE_DIM), lambda i, h: (i, 0)),
            pl.BlockSpec((tm, ROPE_DIM), lambda i, h: (i, 0)),
        ],
        out_specs=[
            pl.BlockSpec((None, tm, MLA_QK), lambda i, h: (h, i, 0)),
            pl.BlockSpec((None, tm, MLA_QK), lambda i, h: (h, i, 0)),
            pl.BlockSpec((None, tm, MLA_V), lambda i, h: (h, i, 0)),
        ],
        out_shape=[
            jax.ShapeDtypeStruct((MLA_HEADS, S, MLA_QK), BF16),
            jax.ShapeDtypeStruct((MLA_HEADS, S, MLA_QK), BF16),
            jax.ShapeDtypeStruct((MLA_HEADS, S, MLA_V), BF16),
        ],
        scratch_shapes=[pltpu.VMEM((tm, q_rank), BF16), pltpu.VMEM((tm, kv_rank), BF16), pltpu.VMEM((tm, ROPE_DIM), BF16)],
        compiler_params=_params(("parallel", "arbitrary")),
    )(proj, proj, proj, g_q, g_kv, w_q, w_kv, cos1, sin1)


def _attn_kernel(q_ref, k_ref, v_ref, o_ref, *, tk, nk):
    q = q_ref[...]
    tq = q.shape[0]
    m = jnp.full((tq, 1), -jnp.inf, F32)
    l = jnp.zeros((tq, 1), F32)
    acc = jnp.zeros((tq, MLA_V), F32)
    for c in range(nk):
        k = k_ref[c * tk:(c + 1) * tk, :]
        v = v_ref[c * tk:(c + 1) * tk, :]
        s = lax.dot_general(q, k, _NT, preferred_element_type=F32)
        m_new = jnp.maximum(m, jnp.max(s, axis=-1, keepdims=True))
        alpha = jnp.exp(m - m_new)
        p = jnp.exp(s - m_new)
        l = alpha * l + jnp.sum(p, axis=-1, keepdims=True)
        acc = alpha * acc + jnp.dot(p.astype(BF16), v, preferred_element_type=F32)
        m = m_new
    o_ref[...] = (acc / l).astype(o_ref.dtype)


def _attention(q, k, v, *, tq, tk):
    S = q.shape[1]
    return pl.pallas_call(
        functools.partial(_attn_kernel, tk=tk, nk=S // tk),
        name="attention",
        grid=(MLA_HEADS, S // tq),
        in_specs=[
            pl.BlockSpec((None, tq, MLA_QK), lambda h, i: (h, i, 0)),
            pl.BlockSpec((None, S, MLA_QK), lambda h, i: (h, 0, 0)),
            pl.BlockSpec((None, S, MLA_V), lambda h, i: (h, 0, 0)),
        ],
        out_specs=pl.BlockSpec((tq, MLA_V), lambda h, i: (i, h)),
        out_shape=jax.ShapeDtypeStruct((S, MLA_HEADS * MLA_V), BF16),
        compiler_params=_params(("parallel", "arbitrary")),
    )(q, k, v)


def _pack_w_in(w_in):
    L, K, _ = w_in.shape
    ml_end = 4 * ML_HEADS * ML_DIM
    gates = 4 * ML_HEADS
    z = lambda n: jnp.zeros((L, K, n), w_in.dtype)
    packed = jnp.concatenate(
        [w_in[:, :, :ml_end], w_in[:, :, ml_end + gates:], z(LANE - ROPE_DIM),
         w_in[:, :, ml_end:ml_end + gates], z(LANE - gates)], axis=-1)
    assert packed.shape[-1] == IN_WIDTH_PACKED
    return packed.astype(BF16)


def kernel(x, positions, g_mix, w_in, b_gates, w_conv, g_ml_out, g_ret_out, g_q_norm, w_q_up, g_kv_norm, w_kv_up,
           w_out, g_ffn, w_ff1, w_ff2, g_final):
    B, S, D = x.shape
    assert B == 1
    depth = w_in.shape[0]
    tm = min(512, S)

    half = ROPE_DIM // 2
    inv = ROPE_THETA ** (-jnp.arange(half, dtype=F32) / half)
    ang = positions[0].astype(F32)[:, None] * inv
    cos1 = jnp.concatenate([jnp.cos(ang), jnp.cos(ang)], axis=-1)
    sin1 = jnp.concatenate([-jnp.sin(ang), jnp.sin(ang)], axis=-1)
    cos4 = jnp.tile(cos1, (1, RET_HEADS))
    sin4 = jnp.tile(sin1, (1, RET_HEADS))

    w_in_p = _pack_w_in(w_in)
    q_rank, kv_rank = w_q_up.shape[1], w_kv_up.shape[1]
    w_q = w_q_up.reshape(depth, q_rank, MLA_HEADS, MLA_QK).transpose(0, 2, 1, 3).astype(BF16)
    w_kv = w_kv_up.reshape(depth, kv_rank, MLA_HEADS, MLA_NOPE + MLA_V).transpose(0, 2, 1, 3).astype(BF16)
    w_out_b = w_out.astype(BF16)
    w_ff1_b = w_ff1.astype(BF16)
    w_ff2_b = w_ff2.astype(BF16)
    bias = jnp.pad(b_gates, ((0, 0), (0, LANE - b_gates.shape[1])))[:, None, :]
    idx = jnp.arange(CHUNK)
    tri = jnp.stack([idx[:, None] >= idx[None, :], idx[:, None] <= idx[None, :]]).astype(F32)
    ret_consts = _retention_consts()
    r3 = lambda g: g[:, None, :]

    xs = x[0]
    for l in range(depth):
        proj = _rms_matmul(xs, r3(g_mix), w_in_p, l, act=False, out_dtype=F32, tm=tm, tn=1536)
        qk, rqk = _prep(proj, w_conv, l, cos4, sin4, tm=tm)
        h_f, h_b = _mlstm(qk, proj, bias[l], tri)
        y_f, y_b = _retention(rqk, proj, ret_consts)
        y_ml, y_ret = _combine(h_f, h_b, y_f, y_b, proj, r3(g_ml_out), r3(g_ret_out), l, tm=tm)
        q, k, v = _mla_prep(proj, r3(g_q_norm), r3(g_kv_norm), w_q, w_kv, l, cos1, sin1, tm=tm)
        y_mla = _attention(q, k, v, tq=min(256, S), tk=min(1024, S))
        xs = _mm_res([y_ml, y_ret, y_mla], w_out_b, l, xs, tm=tm, tn=512)
        act = _rms_matmul(xs, r3(g_ffn), w_ff1_b, l, act=True, out_dtype=BF16, tm=tm, tn=1024)
        xs = _mm_res([act], w_ff2_b, l, xs, tm=tm, tn=512)
    return _rms(xs, g_final[None, :], tm=tm)[None]
```

```python
import functools
import math

import numpy as np
import jax
import jax.numpy as jnp
from jax import lax
from jax.experimental import pallas as pl
from jax.experimental.pallas import tpu as pltpu

F32 = jnp.float32
BF16 = jnp.bfloat16

EPS = 1e-6
NEG_INIT = -1e30
ROPE_THETA = 10000.0
CHUNK = 128
LANE = 128
BF16_ROWS = 16
ML_HEADS = 4
ML_DIM = 128
RET_HEADS = 4
RET_QK = 64
RET_V = 128
RET_DECAY_BASE = 5.0
MLA_HEADS = 8
MLA_NOPE = 128
ROPE_DIM = 64
MLA_V = 128
MLA_QK = MLA_NOPE + ROPE_DIM
VMEM_LIMIT = 56 * 1024 * 1024

COL_ML_Q, COL_ML_K, COL_ML_V, COL_ML_O = 0, 4, 8, 12
COL_RQK, COL_RV, COL_RG = 16, 20, 24
COL_CQ, COL_CKV, COL_KROPE, COL_GATES = 28, 32, 34, 35
IN_WIDTH_PACKED = 36 * LANE

_NT = (((1,), (1,)), ((), ()))
_TN = (((0,), (0,)), ((), ()))


def _params(sem):
    return pltpu.CompilerParams(dimension_semantics=sem, vmem_limit_bytes=VMEM_LIMIT)


def _rmsnorm(x, g):
    return x * lax.rsqrt(jnp.mean(x * x, axis=-1, keepdims=True) + EPS) * g


def _in_proj_kernel(x_ref, g_ref, w_ref, o_ref, gates_ref, hn_ref):
    j = pl.program_id(1)

    @pl.when(j == 0)
    def _():
        hn_ref[...] = _rmsnorm(x_ref[...], g_ref[...]).astype(hn_ref.dtype)

    y = jnp.dot(hn_ref[...], w_ref[...], preferred_element_type=F32)
    o_ref[...] = y.astype(o_ref.dtype)

    @pl.when(j == pl.num_programs(1) - 1)
    def _():
        gates_ref[...] = y[:, y.shape[1] - LANE:]


def _in_proj(x, g, w, layer, *, tm, tn):
    S, K = x.shape
    N = w.shape[-1]
    return pl.pallas_call(
        _in_proj_kernel,
        name="in_proj",
        grid=(S // tm, N // tn),
        in_specs=[
            pl.BlockSpec((tm, K), lambda i, j: (i, 0)),
            pl.BlockSpec((None, 1, K), lambda i, j: (layer, 0, 0)),
            pl.BlockSpec((None, K, tn), lambda i, j: (layer, 0, j)),
        ],
        out_specs=[pl.BlockSpec((tm, tn), lambda i, j: (i, j)), pl.BlockSpec((tm, LANE), lambda i, j: (i, 0))],
        out_shape=[jax.ShapeDtypeStruct((S, N), BF16), jax.ShapeDtypeStruct((S, LANE), F32)],
        scratch_shapes=[pltpu.VMEM((tm, K), BF16)],
        compiler_params=_params(("parallel", "arbitrary")),
    )(x, g, w)


def _mm_res_kernel(*refs, n_pairs):
    a_refs = refs[:n_pairs]
    w_refs = refs[n_pairs:2 * n_pairs]
    x_ref, o_ref = refs[2 * n_pairs], refs[2 * n_pairs + 1]
    acc = x_ref[...]
    for a_ref, w_ref in zip(a_refs, w_refs):
        acc = acc + jnp.dot(a_ref[...], w_ref[...], preferred_element_type=F32)
    o_ref[...] = acc


def _mm_res(acts, w, layer, x, *, tm, tn):
    S, N = x.shape
    n_pairs = len(acts)
    a_specs, w_specs = [], []
    row = 0
    for a in acts:
        k = a.shape[1]
        assert row % k == 0
        a_specs.append(pl.BlockSpec((tm, k), lambda i, j: (i, 0)))
        w_specs.append(pl.BlockSpec((None, k, tn), lambda i, j, rb=row // k: (layer, rb, j)))
        row += k
    assert row == w.shape[1]
    return pl.pallas_call(
        functools.partial(_mm_res_kernel, n_pairs=n_pairs),
        name="out_proj",
        grid=(S // tm, N // tn),
        in_specs=a_specs + w_specs + [pl.BlockSpec((tm, tn), lambda i, j: (i, j))],
        out_specs=pl.BlockSpec((tm, tn), lambda i, j: (i, j)),
        out_shape=jax.ShapeDtypeStruct((S, N), F32),
        compiler_params=_params(("parallel", "arbitrary")),
    )(*acts, *([w] * n_pairs), x)


def _ffn1_kernel(x_ref, g_ref, w_ref, o_ref, hn_ref):
    @pl.when(pl.program_id(1) == 0)
    def _():
        hn_ref[...] = _rmsnorm(x_ref[...], g_ref[...]).astype(hn_ref.dtype)

    y = jnp.dot(hn_ref[...], w_ref[...], preferred_element_type=F32)
    o_ref[...] = jnp.square(jnp.maximum(y, 0.0)).astype(o_ref.dtype)


def _ffn1(x, g, w, layer, *, tm, tn):
    S, K = x.shape
    N = w.shape[-1]
    return pl.pallas_call(
        _ffn1_kernel,
        name="ffn1",
        grid=(S // tm, N // tn),
        in_specs=[
            pl.BlockSpec((tm, K), lambda i, j: (i, 0)),
            pl.BlockSpec((None, 1, K), lambda i, j: (layer, 0, 0)),
            pl.BlockSpec((None, K, tn), lambda i, j: (layer, 0, j)),
        ],
        out_specs=pl.BlockSpec((tm, tn), lambda i, j: (i, j)),
        out_shape=jax.ShapeDtypeStruct((S, N), BF16),
        scratch_shapes=[pltpu.VMEM((tm, K), BF16)],
        compiler_params=_params(("parallel", "arbitrary")),
    )(x, g, w)


def _ffn2_kernel(a_ref, w_ref, x_ref, o_ref):
    @pl.when(pl.program_id(2) == 0)
    def _():
        o_ref[...] = x_ref[...]

    o_ref[...] += jnp.dot(a_ref[...], w_ref[...], preferred_element_type=F32)


def _ffn2(a, w, layer, x, *, tm, tn, tk):
    S, N = x.shape
    K = a.shape[1]
    return pl.pallas_call(
        _ffn2_kernel,
        name="ffn2",
        grid=(S // tm, N // tn, K // tk),
        in_specs=[
            pl.BlockSpec((tm, tk), lambda i, j, k: (i, k)),
            pl.BlockSpec((None, tk, tn), lambda i, j, k: (layer, k, j)),
            pl.BlockSpec((tm, tn), lambda i, j, k: (i, j)),
        ],
        out_specs=pl.BlockSpec((tm, tn), lambda i, j, k: (i, j)),
        out_shape=jax.ShapeDtypeStruct((S, N), F32),
        compiler_params=_params(("parallel", "arbitrary", "arbitrary")),
    )(a, w, x)


def _final_rms_kernel(x_ref, g_ref, o_ref):
    o_ref[...] = _rmsnorm(x_ref[...], g_ref[...])


def _final_rms(x, g, *, tm):
    S, K = x.shape
    return pl.pallas_call(
        _final_rms_kernel,
        name="final_rms",
        grid=(S // tm,),
        in_specs=[pl.BlockSpec((tm, K), lambda i: (i, 0)), pl.BlockSpec((1, K), lambda i: (0, 0))],
        out_specs=pl.BlockSpec((tm, K), lambda i: (i, 0)),
        out_shape=jax.ShapeDtypeStruct((S, K), F32),
        compiler_params=_params(("parallel",)),
    )(x, g)


def _rope_grouped(t, cosf, sinf):
    width = t.shape[1]
    lane = lax.broadcasted_iota(jnp.int32, t.shape, 1)
    first_half = (lane % ROPE_DIM) < (ROPE_DIM // 2)
    partner = jnp.where(first_half, pltpu.roll(t, width - ROPE_DIM // 2, 1), pltpu.roll(t, ROPE_DIM // 2, 1))
    return t * cosf + partner * sinf


def _prep_kernel(qk_ref, prev_ref, next_ref, wc_ref, rqk_ref, cos_ref, sin_ref, qk_out, rqk_out, *, tm, nblk):
    i = pl.program_id(0)
    x = qk_ref[...].astype(F32)
    row = lax.broadcasted_iota(jnp.int32, x.shape, 0)
    prev_row = jnp.where(i > 0, prev_ref[BF16_ROWS - 1:BF16_ROWS, :].astype(F32), 0.0)
    next_row = jnp.where(i < nblk - 1, next_ref[0:1, :].astype(F32), 0.0)
    x_m = jnp.where(row == 0, prev_row, pltpu.roll(x, 1, 0))
    x_p = jnp.where(row == tm - 1, next_row, pltpu.roll(x, tm - 1, 0))
    w = wc_ref[...]
    y = x_m * w[0:1, :] + x * w[1:2, :] + x_p * w[2:3, :]
    y = y / (1.0 + jnp.exp(-y))
    half = ML_HEADS * ML_DIM
    qk_out[:, :half] = (y[:, :half] * (ML_DIM ** -0.5)).astype(qk_out.dtype)
    qk_out[:, half:] = y[:, half:].astype(qk_out.dtype)

    cosf = cos_ref[...]
    sinf = sin_ref[...]
    width = RET_HEADS * RET_QK
    r = rqk_ref[...].astype(F32)
    rqk_out[:, :width] = _rope_grouped(r[:, :width], cosf, sinf).astype(rqk_out.dtype)
    rqk_out[:, width:] = (_rope_grouped(r[:, width:], cosf, sinf) * (RET_QK ** -0.5)).astype(rqk_out.dtype)


def _prep(proj, w_conv, layer, cos4, sin4, *, tm):
    S = proj.shape[0]
    nblk = S // tm
    qk_w = 2 * ML_HEADS * ML_DIM
    rqk_w = 2 * RET_HEADS * RET_QK
    hb = tm // BF16_ROWS
    return pl.pallas_call(
        functools.partial(_prep_kernel, tm=tm, nblk=nblk),
        name="prep",
        grid=(nblk,),
        in_specs=[
            pl.BlockSpec((tm, qk_w), lambda i: (i, 0)),
            pl.BlockSpec((BF16_ROWS, qk_w), lambda i: (jnp.maximum(i * hb - 1, 0), 0)),
            pl.BlockSpec((BF16_ROWS, qk_w), lambda i: (jnp.minimum((i + 1) * hb, S // BF16_ROWS - 1), 0)),
            pl.BlockSpec((None, 3, qk_w), lambda i: (layer, 0, 0)),
            pl.BlockSpec((tm, rqk_w), lambda i: (i, COL_RQK * LANE // rqk_w)),
            pl.BlockSpec((tm, rqk_w // 2), lambda i: (i, 0)),
            pl.BlockSpec((tm, rqk_w // 2), lambda i: (i, 0)),
        ],
        out_specs=[pl.BlockSpec((tm, qk_w), lambda i: (i, 0)), pl.BlockSpec((tm, rqk_w), lambda i: (i, 0))],
        out_shape=[jax.ShapeDtypeStruct((S, qk_w), BF16), jax.ShapeDtypeStruct((S, rqk_w), BF16)],
        compiler_params=_params(("parallel",)),
    )(proj, proj, proj, w_conv, proj, cos4, sin4)


def _log_sigmoid(t):
    return jnp.minimum(t, 0.0) - jnp.log1p(jnp.exp(-jnp.abs(t)))


def _mlstm_kernel(qf, kf, vf, gf, qb, kb, vb, gb, bias_ref, tri_ref, hf_ref, hb_ref, c_ref, m_ref):
    @pl.when(pl.program_id(0) == 0)
    def _():
        c_ref[...] = jnp.zeros(c_ref.shape, F32)
        m_ref[...] = jnp.full(m_ref.shape, NEG_INIT, F32)

    L = CHUNK
    row = lax.broadcasted_iota(jnp.int32, (L, L), 0)
    col = lax.broadcasted_iota(jnp.int32, (L, L), 1)
    ones = jnp.ones((L, ML_DIM), BF16)
    dirs = ((qf, kf, vf, gf, hf_ref), (qb, kb, vb, gb, hb_ref))
    for d, (q_ref, k_ref, v_ref, g_ref, h_ref) in enumerate(dirs):
        g = g_ref[...] + bias_ref[...]
        logf = _log_sigmoid(g)
        g_t = g.T
        logf_t = logf.T
        tri_col = tri_ref[d]
        tri_row = tri_ref[1 - d]
        b_cols = jnp.dot(tri_col, logf, precision=lax.Precision.HIGHEST, preferred_element_type=F32)
        b_rows = jnp.dot(logf_t[0:16, :], tri_row, precision=lax.Precision.HIGHEST, preferred_element_type=F32)
        mask = (row >= col) if d == 0 else (row <= col)
        for h in range(ML_HEADS):
            ci, cf, st = 8 * d + h, 8 * d + 4 + h, ML_HEADS * d + h
            sl = slice(h * ML_DIM, (h + 1) * ML_DIM)
            bcol = b_cols[:, cf:cf + 1]
            brow = b_rows[cf:cf + 1, :]
            irow = g_t[ci:ci + 1, :]
            icol = g[:, ci:ci + 1]
            blast = brow[:, L - 1:L] if d == 0 else brow[:, 0:1]
            m_st = m_ref[st:st + 1, 0:1]
            c_st = c_ref[st]
            q = q_ref[:, sl]
            k = k_ref[:, sl]
            v1 = jnp.concatenate([v_ref[:, sl], ones], axis=1)

            dlog = jnp.where(mask, bcol - brow + irow, -jnp.inf)
            inter_log = bcol + m_st
            m_t = jnp.maximum(jnp.max(dlog, axis=1, keepdims=True), inter_log)
            w_intra = jnp.exp(dlog - m_t)
            w_inter = jnp.exp(inter_log - m_t)
            s = lax.dot_general(q, k, _NT, preferred_element_type=F32) * w_intra
            tot = (jnp.dot(s.astype(BF16), v1, preferred_element_type=F32)
                   + w_inter * jnp.dot(q, c_st.astype(BF16), preferred_element_type=F32))
            num = tot[:, :ML_DIM]
            den = tot[:, ML_DIM:]
            h_ref[:, sl] = num / jnp.maximum(jnp.abs(den), jnp.exp(-m_t))

            wend = blast - bcol + icol
            m_loc = jnp.max(wend, axis=0, keepdims=True)
            kw = (k.astype(F32) * jnp.exp(wend - m_loc)).astype(BF16)
            c_loc = lax.dot_general(kw, v1, _TN, preferred_element_type=F32)
            m_new = jnp.maximum(blast + m_st, m_loc)
            c_ref[st] = jnp.exp(blast + m_st - m_new) * c_st + jnp.exp(m_loc - m_new) * c_loc
            m_ref[st:st + 1, :] = jnp.broadcast_to(m_new, (1, LANE))


def _mlstm(qk, proj, gates, bias, tri):
    S = proj.shape[0]
    nc = S // CHUNK
    W = ML_HEADS * ML_DIM
    fwd = lambda n: n
    bwd = lambda n: nc - 1 - n

    def specs(idx):
        return [
            pl.BlockSpec((CHUNK, W), lambda n: (idx(n), 0)),
            pl.BlockSpec((CHUNK, W), lambda n: (idx(n), 1)),
            pl.BlockSpec((CHUNK, W), lambda n: (idx(n), COL_ML_V * LANE // W)),
            pl.BlockSpec((CHUNK, LANE), lambda n: (idx(n), 0)),
        ]

    return pl.pallas_call(
        _mlstm_kernel,
        name="mlstm",
        grid=(nc,),
        in_specs=specs(fwd) + specs(bwd) + [
            pl.BlockSpec((1, LANE), lambda n: (0, 0)),
            pl.BlockSpec((2, CHUNK, CHUNK), lambda n: (0, 0, 0)),
        ],
        out_specs=[pl.BlockSpec((CHUNK, W), lambda n: (fwd(n), 0)), pl.BlockSpec((CHUNK, W), lambda n: (bwd(n), 0))],
        out_shape=[jax.ShapeDtypeStruct((S, W), F32), jax.ShapeDtypeStruct((S, W), F32)],
        scratch_shapes=[pltpu.VMEM((2 * ML_HEADS, ML_DIM, 2 * ML_DIM), F32), pltpu.VMEM((2 * ML_HEADS, LANE), F32)],
        compiler_params=_params(("arbitrary",)),
    )(qk, qk, proj, gates, qk, qk, proj, gates, bias, tri)


def _retention_kernel(qkf, vf, qkb, vb, decay_ref, inner_ref, zeta_ref, yf_ref, yb_ref, r_ref, *, chunk_decay):
    @pl.when(pl.program_id(0) == 0)
    def _():
        r_ref[...] = jnp.zeros(r_ref.shape, F32)

    kofs = RET_HEADS * RET_QK
    for d, (qk_ref, v_ref, y_ref) in enumerate(((qkf, vf, yf_ref), (qkb, vb, yb_ref))):
        for h in range(RET_HEADS):
            st = RET_HEADS * d + h
            q = qk_ref[:, h * RET_QK:(h + 1) * RET_QK]
            k = qk_ref[:, kofs + h * RET_QK:kofs + (h + 1) * RET_QK]
            v = v_ref[:, h * RET_V:(h + 1) * RET_V]
            r_st = r_ref[st]
            s = lax.dot_general(q, k, _NT, preferred_element_type=F32) * decay_ref[st]
            y = (jnp.dot(s.astype(BF16), v, preferred_element_type=F32)
                 + jnp.dot(q, r_st.astype(BF16), preferred_element_type=F32) * inner_ref[st])
            y_ref[:, h * RET_V:(h + 1) * RET_V] = y
            kw = (k.astype(F32) * zeta_ref[st][:, :RET_QK]).astype(BF16)
            r_ref[st] = chunk_decay[st] * r_st + lax.dot_general(kw, v, _TN, preferred_element_type=F32)


def _retention_consts():
    log_gamma = jnp.log1p(-jnp.exp2(-RET_DECAY_BASE - jnp.arange(RET_HEADS, dtype=F32)))
    idx = jnp.arange(CHUNK, dtype=F32)
    diff = idx[:, None] - idx[None, :]
    decay, inner, zeta = [], [], []
    for d in range(2):
        lg_d = log_gamma if d == 0 else log_gamma[::-1]
        dd = diff if d == 0 else -diff
        pos = idx if d == 0 else (CHUNK - 1.0 - idx)
        keep = dd >= 0
        for h in range(RET_HEADS):
            lg = lg_d[h]
            decay.append(jnp.where(keep, jnp.exp(jnp.where(keep, dd, 0.0) * lg), 0.0))
            inner.append(jnp.broadcast_to(jnp.exp((pos + 1.0) * lg)[:, None], (CHUNK, LANE)))
            zeta.append(jnp.broadcast_to(jnp.exp((CHUNK - 1.0 - pos) * lg)[:, None], (CHUNK, LANE)))
    lg_np = np.log1p(-np.exp2(-RET_DECAY_BASE - np.arange(RET_HEADS, dtype=np.float32))).astype(np.float32)
    chunk_decay = tuple(float(np.exp(np.float32(CHUNK) * lg)) for lg in list(lg_np) + list(lg_np[::-1]))
    return jnp.stack(decay), jnp.stack(inner), jnp.stack(zeta), chunk_decay


def _retention(rqk, proj, consts):
    S = proj.shape[0]
    nc = S // CHUNK
    decay, inner, zeta, chunk_decay = consts
    QW = 2 * RET_HEADS * RET_QK
    VW = RET_HEADS * RET_V
    fwd = lambda n: n
    bwd = lambda n: nc - 1 - n

    def specs(idx):
        return [
            pl.BlockSpec((CHUNK, QW), lambda n: (idx(n), 0)),
            pl.BlockSpec((CHUNK, VW), lambda n: (idx(n), COL_RV * LANE // VW)),
        ]

    const_spec = pl.BlockSpec((2 * RET_HEADS, CHUNK, LANE), lambda n: (0, 0, 0))
    return pl.pallas_call(
        functools.partial(_retention_kernel, chunk_decay=chunk_decay),
        name="retention",
        grid=(nc,),
        in_specs=specs(fwd) + specs(bwd) + [const_spec, const_spec, const_spec],
        out_specs=[pl.BlockSpec((CHUNK, VW), lambda n: (fwd(n), 0)), pl.BlockSpec((CHUNK, VW), lambda n: (bwd(n), 0))],
        out_shape=[jax.ShapeDtypeStruct((S, VW), F32), jax.ShapeDtypeStruct((S, VW), F32)],
        scratch_shapes=[pltpu.VMEM((2 * RET_HEADS, RET_QK, RET_V), F32)],
        compiler_params=_params(("arbitrary",)),
    )(rqk, proj, rqk, proj, decay, inner, zeta)


def _combine_kernel(hf, hb, o_ref, gml_ref, yf, yb, rg_ref, gret_ref, yml_out, yret_out):
    for h in range(ML_HEADS):
        sl = slice(h * ML_DIM, (h + 1) * ML_DIM)
        y = _rmsnorm(hf[:, sl] + hb[:, sl], gml_ref[:, sl])
        yml_out[:, sl] = (y / (1.0 + jnp.exp(-o_ref[:, sl].astype(F32)))).astype(yml_out.dtype)
    for h in range(RET_HEADS):
        sl = slice(h * RET_V, (h + 1) * RET_V)
        y = _rmsnorm(yf[:, sl] + yb[:, sl], gret_ref[:, sl])
        gate = rg_ref[:, sl].astype(F32)
        yret_out[:, sl] = (gate / (1.0 + jnp.exp(-gate)) * y).astype(yret_out.dtype)


def _combine(h_f, h_b, y_f, y_b, proj, g_ml, g_ret, layer, *, tm):
    S = proj.shape[0]
    W = ML_HEADS * ML_DIM
    blk = lambda c: pl.BlockSpec((tm, W), lambda i: (i, c))
    gain = pl.BlockSpec((None, 1, W), lambda i: (layer, 0, 0))
    return pl.pallas_call(
        _combine_kernel,
        name="combine",
        grid=(S // tm,),
        in_specs=[blk(0), blk(0), blk(COL_ML_O * LANE // W), gain, blk(0), blk(0), blk(COL_RG * LANE // W), gain],
        out_specs=[blk(0), blk(0)],
        out_shape=[jax.ShapeDtypeStruct((S, W), BF16), jax.ShapeDtypeStruct((S, W), BF16)],
        compiler_params=_params(("parallel",)),
    )(h_f, h_b, proj, g_ml, y_f, y_b, proj, g_ret)


def _mla_prep_kernel(cq_ref, ckv_ref, kr_ref, gq_ref, gkv_ref, wq_ref, wkv_ref, cos_ref, sin_ref,
                     q_out, k_out, v_out):
    tm = cq_ref.shape[0]
    cos4 = cos_ref[...]
    sin4 = sin_ref[...]
    qn = _rmsnorm(cq_ref[...].astype(F32), gq_ref[...]).astype(BF16)
    kvn = _rmsnorm(ckv_ref[...].astype(F32), gkv_ref[...]).astype(BF16)
    q_all = jnp.dot(qn, wq_ref[...], preferred_element_type=F32)
    kv_all = jnp.dot(kvn, wkv_ref[...], preferred_element_type=F32)
    q_scale = (MLA_QK ** -0.5) * math.log2(math.e)
    nope_w = MLA_HEADS * MLA_NOPE
    q_rope = _rope_grouped(q_all[:, nope_w:], jnp.concatenate([cos4, cos4], axis=1),
                           jnp.concatenate([sin4, sin4], axis=1)) * q_scale
    k_rope = _rope_grouped(kr_ref[...].astype(F32), cos4[:, :LANE], sin4[:, :LANE])[:, :ROPE_DIM].astype(BF16)
    ones = jnp.ones((tm, MLA_V), BF16)
    for h in range(MLA_HEADS):
        q_out[h, :, :MLA_NOPE] = (q_all[:, h * MLA_NOPE:(h + 1) * MLA_NOPE] * q_scale).astype(BF16)
        q_out[h, :, MLA_NOPE:] = q_rope[:, h * ROPE_DIM:(h + 1) * ROPE_DIM].astype(BF16)
        kv0 = h * (MLA_NOPE + MLA_V)
        k_out[h, :, :MLA_NOPE] = kv_all[:, kv0:kv0 + MLA_NOPE].astype(BF16)
        k_out[h, :, MLA_NOPE:] = k_rope
        v_out[h, :, :MLA_V] = kv_all[:, kv0 + MLA_NOPE:kv0 + MLA_NOPE + MLA_V].astype(BF16)
        v_out[h, :, MLA_V:] = ones


def _mla_prep(proj, g_q, g_kv, w_q, w_kv, layer, cos4, sin4, *, tm):
    S = proj.shape[0]
    q_rank, kv_rank = w_q.shape[1], w_kv.shape[1]
    return pl.pallas_call(
        _mla_prep_kernel,
        name="mla_prep",
        grid=(S // tm,),
        in_specs=[
            pl.BlockSpec((tm, q_rank), lambda i: (i, COL_CQ * LANE // q_rank)),
            pl.BlockSpec((tm, kv_rank), lambda i: (i, COL_CKV * LANE // kv_rank)),
            pl.BlockSpec((tm, LANE), lambda i: (i, COL_KROPE)),
            pl.BlockSpec((None, 1, q_rank), lambda i: (layer, 0, 0)),
            pl.BlockSpec((None, 1, kv_rank), lambda i: (layer, 0, 0)),
            pl.BlockSpec((None, q_rank, w_q.shape[2]), lambda i: (layer, 0, 0)),
            pl.BlockSpec((None, kv_rank, w_kv.shape[2]), lambda i: (layer, 0, 0)),
            pl.BlockSpec((tm, 2 * LANE), lambda i: (i, 0)),
            pl.BlockSpec((tm, 2 * LANE), lambda i: (i, 0)),
        ],
        out_specs=[
            pl.BlockSpec((MLA_HEADS, tm, MLA_QK), lambda i: (0, i, 0)),
            pl.BlockSpec((MLA_HEADS, tm, MLA_QK), lambda i: (0, i, 0)),
            pl.BlockSpec((MLA_HEADS, tm, 2 * MLA_V), lambda i: (0, i, 0)),
        ],
        out_shape=[
            jax.ShapeDtypeStruct((MLA_HEADS, S, MLA_QK), BF16),
            jax.ShapeDtypeStruct((MLA_HEADS, S, MLA_QK), BF16),
            jax.ShapeDtypeStruct((MLA_HEADS, S, 2 * MLA_V), BF16),
        ],
        compiler_params=_params(("parallel",)),
    )(proj, proj, proj, g_q, g_kv, w_q, w_kv, cos4, sin4)


def _attn_kernel(q_ref, k_ref, v_ref, o_ref, *, tk, nk):
    q = q_ref[...]
    m = acc = None
    for c in range(nk):
        k = k_ref[c * tk:(c + 1) * tk, :]
        v = v_ref[c * tk:(c + 1) * tk, :]
        s = lax.dot_general(q, k, _NT, preferred_element_type=F32)
        m_c = jnp.max(s, axis=-1, keepdims=True)
        if c == 0:
            m = m_c
            acc = jnp.dot(jnp.exp2(s - m).astype(BF16), v, preferred_element_type=F32)
        else:
            m_new = jnp.maximum(m, m_c)
            acc = jnp.exp2(m - m_new) * acc + jnp.dot(jnp.exp2(s - m_new).astype(BF16), v, preferred_element_type=F32)
            m = m_new
    o_ref[...] = (acc[:, :MLA_V] / acc[:, MLA_V:]).astype(o_ref.dtype)


def _attention(q, k, v, *, tq, tk):
    S = q.shape[1]
    return pl.pallas_call(
        functools.partial(_attn_kernel, tk=tk, nk=S // tk),
        name="attention",
        grid=(MLA_HEADS, S // tq),
        in_specs=[
            pl.BlockSpec((None, tq, MLA_QK), lambda h, i: (h, i, 0)),
            pl.BlockSpec((None, S, MLA_QK), lambda h, i: (h, 0, 0)),
            pl.BlockSpec((None, S, 2 * MLA_V), lambda h, i: (h, 0, 0)),
        ],
        out_specs=pl.BlockSpec((tq, MLA_V), lambda h, i: (i, h)),
        out_shape=jax.ShapeDtypeStruct((S, MLA_HEADS * MLA_V), BF16),
        compiler_params=_params(("parallel", "arbitrary")),
    )(q, k, v)


def _pack_w_in(w_in):
    L, K, _ = w_in.shape
    ml_end = 4 * ML_HEADS * ML_DIM
    gates = 4 * ML_HEADS
    w = w_in.astype(BF16)
    z = lambda n: jnp.zeros((L, K, n), BF16)
    packed = jnp.concatenate(
        [w[:, :, :ml_end], w[:, :, ml_end + gates:], z(LANE - ROPE_DIM),
         w[:, :, ml_end:ml_end + gates], z(LANE - gates)], axis=-1)
    assert packed.shape[-1] == IN_WIDTH_PACKED
    return packed


def kernel(x, positions, g_mix, w_in, b_gates, w_conv, g_ml_out, g_ret_out, g_q_norm, w_q_up, g_kv_norm, w_kv_up,
           w_out, g_ffn, w_ff1, w_ff2, g_final):
    B, S, D = x.shape
    assert B == 1
    depth = w_in.shape[0]
    tm = min(512, S)
    tm_big = min(1024, S)

    half = ROPE_DIM // 2
    inv = ROPE_THETA ** (-jnp.arange(half, dtype=F32) / half)
    ang = positions[0].astype(F32)[:, None] * inv
    cos4 = jnp.tile(jnp.concatenate([jnp.cos(ang), jnp.cos(ang)], axis=-1), (1, RET_HEADS))
    sin4 = jnp.tile(jnp.concatenate([-jnp.sin(ang), jnp.sin(ang)], axis=-1), (1, RET_HEADS))

    w_in_p = _pack_w_in(w_in)
    q_rank = w_q_up.shape[1]
    w_q4 = w_q_up.astype(BF16).reshape(depth, q_rank, MLA_HEADS, MLA_QK)
    w_q = jnp.concatenate([w_q4[..., :MLA_NOPE].reshape(depth, q_rank, MLA_HEADS * MLA_NOPE),
                           w_q4[..., MLA_NOPE:].reshape(depth, q_rank, MLA_HEADS * ROPE_DIM)], axis=-1)
    w_kv = w_kv_up.astype(BF16)
    w_out_b = w_out.astype(BF16)
    w_ff1_b = w_ff1.astype(BF16)
    w_ff2_b = w_ff2.astype(BF16)
    bias = jnp.pad(b_gates, ((0, 0), (0, LANE - b_gates.shape[1])))[:, None, :]
    idx = jnp.arange(CHUNK)
    tri = jnp.stack([idx[:, None] >= idx[None, :], idx[:, None] <= idx[None, :]]).astype(F32)
    ret_consts = _retention_consts()
    r3 = lambda g: g[:, None, :]

    xs = x[0]
    for l in range(depth):
        proj, gates = _in_proj(xs, r3(g_mix), w_in_p, l, tm=tm_big, tn=1536)
        qk, rqk = _prep(proj, w_conv, l, cos4, sin4, tm=tm)
        h_f, h_b = _mlstm(qk, proj, gates, bias[l], tri)
        y_f, y_b = _retention(rqk, proj, ret_consts)
        y_ml, y_ret = _combine(h_f, h_b, y_f, y_b, proj, r3(g_ml_out), r3(g_ret_out), l, tm=tm)
        q, k, v = _mla_prep(proj, r3(g_q_norm), r3(g_kv_norm), w_q, w_kv, l, cos4, sin4, tm=tm)
        y_mla = _attention(q, k, v, tq=min(1024, S), tk=min(512, S))
        xs = _mm_res([y_ml, y_ret, y_mla], w_out_b, l, xs, tm=tm_big, tn=1024)
        act = _ffn1(xs, r3(g_ffn), w_ff1_b, l, tm=tm_big, tn=1024)
        xs = _ffn2(act, w_ff2_b, l, xs, tm=tm_big, tn=1024, tk=2048)
    return _final_rms(xs, g_final[None, :], tm=tm)[None]
```

```python
import functools
import math

import numpy as np
import jax
import jax.numpy as jnp
from jax import lax
from jax.experimental import pallas as pl
from jax.experimental.pallas import tpu as pltpu

F32 = jnp.float32
BF16 = jnp.bfloat16

EPS = 1e-6
NEG_INIT = -1e30
ROPE_THETA = 10000.0
CHUNK = 128
LANE = 128
BF16_ROWS = 16
ML_HEADS = 4
ML_DIM = 128
RET_HEADS = 4
RET_QK = 64
RET_V = 128
RET_DECAY_BASE = 5.0
MLA_HEADS = 8
MLA_NOPE = 128
ROPE_DIM = 64
MLA_V = 128
MLA_QK = MLA_NOPE + ROPE_DIM
VMEM_LIMIT = 56 * 1024 * 1024

COL_ML_Q, COL_ML_K, COL_ML_V, COL_ML_O = 0, 4, 8, 12
COL_RQK, COL_RV, COL_RG = 16, 20, 24
COL_CQ, COL_CKV, COL_KROPE, COL_GATES = 28, 32, 34, 35
IN_WIDTH_PACKED = 36 * LANE

_NT = (((1,), (1,)), ((), ()))
_TN = (((0,), (0,)), ((), ()))


def _params(sem):
    return pltpu.CompilerParams(dimension_semantics=sem, vmem_limit_bytes=VMEM_LIMIT)


def _rmsnorm(x, g):
    return x * lax.rsqrt(jnp.mean(x * x, axis=-1, keepdims=True) + EPS) * g


def _in_proj_kernel(x_ref, g_ref, w_ref, o_ref, gates_ref, hn_ref):
    j = pl.program_id(1)

    @pl.when(j == 0)
    def _():
        hn_ref[...] = _rmsnorm(x_ref[...], g_ref[...]).astype(hn_ref.dtype)

    y = jnp.dot(hn_ref[...], w_ref[...], preferred_element_type=F32)
    o_ref[...] = y.astype(o_ref.dtype)

    @pl.when(j == pl.num_programs(1) - 1)
    def _():
        gates_ref[...] = y[:, y.shape[1] - LANE:].T


def _in_proj(x, g, w, layer, *, tm, tn):
    S, K = x.shape
    N = w.shape[-1]
    return pl.pallas_call(
        _in_proj_kernel,
        name="in_proj",
        grid=(S // tm, N // tn),
        in_specs=[
            pl.BlockSpec((tm, K), lambda i, j: (i, 0)),
            pl.BlockSpec((None, 1, K), lambda i, j: (layer, 0, 0)),
            pl.BlockSpec((None, K, tn), lambda i, j: (layer, 0, j)),
        ],
        out_specs=[pl.BlockSpec((tm, tn), lambda i, j: (i, j)), pl.BlockSpec((LANE, tm), lambda i, j: (0, i))],
        out_shape=[jax.ShapeDtypeStruct((S, N), BF16), jax.ShapeDtypeStruct((LANE, S), F32)],
        scratch_shapes=[pltpu.VMEM((tm, K), BF16)],
        compiler_params=_params(("parallel", "arbitrary")),
    )(x, g, w)


def _mm_res_kernel(*refs, n_pairs):
    a_refs = refs[:n_pairs]
    w_refs = refs[n_pairs:2 * n_pairs]
    x_ref, o_ref = refs[2 * n_pairs], refs[2 * n_pairs + 1]
    acc = x_ref[...]
    for a_ref, w_ref in zip(a_refs, w_refs):
        acc = acc + jnp.dot(a_ref[...], w_ref[...], preferred_element_type=F32)
    o_ref[...] = acc


def _mm_res(acts, w, layer, x, *, tm, tn):
    S, N = x.shape
    n_pairs = len(acts)
    a_specs, w_specs = [], []
    row = 0
    for a in acts:
        k = a.shape[1]
        assert row % k == 0
        a_specs.append(pl.BlockSpec((tm, k), lambda i, j: (i, 0)))
        w_specs.append(pl.BlockSpec((None, k, tn), lambda i, j, rb=row // k: (layer, rb, j)))
        row += k
    assert row == w.shape[1]
    return pl.pallas_call(
        functools.partial(_mm_res_kernel, n_pairs=n_pairs),
        name="out_proj",
        grid=(S // tm, N // tn),
        in_specs=a_specs + w_specs + [pl.BlockSpec((tm, tn), lambda i, j: (i, j))],
        out_specs=pl.BlockSpec((tm, tn), lambda i, j: (i, j)),
        out_shape=jax.ShapeDtypeStruct((S, N), F32),
        compiler_params=_params(("parallel", "arbitrary")),
    )(*acts, *([w] * n_pairs), x)


def _ffn1_kernel(x_ref, g_ref, w_ref, o_ref, hn_ref):
    @pl.when(pl.program_id(1) == 0)
    def _():
        hn_ref[...] = _rmsnorm(x_ref[...], g_ref[...]).astype(hn_ref.dtype)

    y = jnp.dot(hn_ref[...], w_ref[...], preferred_element_type=F32)
    o_ref[...] = jnp.square(jnp.maximum(y, 0.0)).astype(o_ref.dtype)


def _ffn1(x, g, w, layer, *, tm, tn):
    S, K = x.shape
    N = w.shape[-1]
    return pl.pallas_call(
        _ffn1_kernel,
        name="ffn1",
        grid=(S // tm, N // tn),
        in_specs=[
            pl.BlockSpec((tm, K), lambda i, j: (i, 0)),
            pl.BlockSpec((None, 1, K), lambda i, j: (layer, 0, 0)),
            pl.BlockSpec((None, K, tn), lambda i, j: (layer, 0, j)),
        ],
        out_specs=pl.BlockSpec((tm, tn), lambda i, j: (i, j)),
        out_shape=jax.ShapeDtypeStruct((S, N), BF16),
        scratch_shapes=[pltpu.VMEM((tm, K), BF16)],
        compiler_params=_params(("parallel", "arbitrary")),
    )(x, g, w)


def _ffn2_kernel(a_ref, w_ref, x_ref, o_ref):
    @pl.when(pl.program_id(2) == 0)
    def _():
        o_ref[...] = x_ref[...]

    o_ref[...] += jnp.dot(a_ref[...], w_ref[...], preferred_element_type=F32)


def _ffn2(a, w, layer, x, *, tm, tn, tk):
    S, N = x.shape
    K = a.shape[1]
    return pl.pallas_call(
        _ffn2_kernel,
        name="ffn2",
        grid=(S // tm, N // tn, K // tk),
        in_specs=[
            pl.BlockSpec((tm, tk), lambda i, j, k: (i, k)),
            pl.BlockSpec((None, tk, tn), lambda i, j, k: (layer, k, j)),
            pl.BlockSpec((tm, tn), lambda i, j, k: (i, j)),
        ],
        out_specs=pl.BlockSpec((tm, tn), lambda i, j, k: (i, j)),
        out_shape=jax.ShapeDtypeStruct((S, N), F32),
        compiler_params=_params(("parallel", "arbitrary", "arbitrary")),
    )(a, w, x)


def _final_rms_kernel(x_ref, g_ref, o_ref):
    o_ref[...] = _rmsnorm(x_ref[...], g_ref[...])


def _final_rms(x, g, *, tm):
    S, K = x.shape
    return pl.pallas_call(
        _final_rms_kernel,
        name="final_rms",
        grid=(S // tm,),
        in_specs=[pl.BlockSpec((tm, K), lambda i: (i, 0)), pl.BlockSpec((1, K), lambda i: (0, 0))],
        out_specs=pl.BlockSpec((tm, K), lambda i: (i, 0)),
        out_shape=jax.ShapeDtypeStruct((S, K), F32),
        compiler_params=_params(("parallel",)),
    )(x, g)


def _rope_grouped(t, cosf, sinf):
    width = t.shape[1]
    lane = lax.broadcasted_iota(jnp.int32, t.shape, 1)
    first_half = (lane % ROPE_DIM) < (ROPE_DIM // 2)
    partner = jnp.where(first_half, pltpu.roll(t, width - ROPE_DIM // 2, 1), pltpu.roll(t, ROPE_DIM // 2, 1))
    return t * cosf + partner * sinf


def _prep_kernel(qk_ref, prev_ref, next_ref, wc_ref, rqk_ref, cos_ref, sin_ref, mq_out, mkt_out, rq_out, rkt_out,
                 *, tm, nblk):
    i = pl.program_id(0)
    x = qk_ref[...].astype(F32)
    row = lax.broadcasted_iota(jnp.int32, x.shape, 0)
    prev_row = jnp.where(i > 0, prev_ref[BF16_ROWS - 1:BF16_ROWS, :].astype(F32), 0.0)
    next_row = jnp.where(i < nblk - 1, next_ref[0:1, :].astype(F32), 0.0)
    x_m = jnp.where(row == 0, prev_row, pltpu.roll(x, 1, 0))
    x_p = jnp.where(row == tm - 1, next_row, pltpu.roll(x, tm - 1, 0))
    w = wc_ref[...]
    y = x_m * w[0:1, :] + x * w[1:2, :] + x_p * w[2:3, :]
    y = y / (1.0 + jnp.exp(-y))
    half = ML_HEADS * ML_DIM
    mq_out[...] = (y[:, :half] * (ML_DIM ** -0.5)).astype(mq_out.dtype)
    mkt_out[...] = y[:, half:].T.astype(mkt_out.dtype)

    cosf = cos_ref[...]
    sinf = sin_ref[...]
    width = RET_HEADS * RET_QK
    r = rqk_ref[...].astype(F32)
    rq_out[...] = _rope_grouped(r[:, :width], cosf, sinf).astype(rq_out.dtype)
    rkt_out[...] = (_rope_grouped(r[:, width:], cosf, sinf) * (RET_QK ** -0.5)).T.astype(rkt_out.dtype)


def _prep(proj, w_conv, layer, cos4, sin4, *, tm):
    S = proj.shape[0]
    nblk = S // tm
    qk_w = 2 * ML_HEADS * ML_DIM
    rqk_w = 2 * RET_HEADS * RET_QK
    hb = tm // BF16_ROWS
    return pl.pallas_call(
        functools.partial(_prep_kernel, tm=tm, nblk=nblk),
        name="prep",
        grid=(nblk,),
        in_specs=[
            pl.BlockSpec((tm, qk_w), lambda i: (i, 0)),
            pl.BlockSpec((BF16_ROWS, qk_w), lambda i: (jnp.maximum(i * hb - 1, 0), 0)),
            pl.BlockSpec((BF16_ROWS, qk_w), lambda i: (jnp.minimum((i + 1) * hb, S // BF16_ROWS - 1), 0)),
            pl.BlockSpec((None, 3, qk_w), lambda i: (layer, 0, 0)),
            pl.BlockSpec((tm, rqk_w), lambda i: (i, COL_RQK * LANE // rqk_w)),
            pl.BlockSpec((tm, rqk_w // 2), lambda i: (i, 0)),
            pl.BlockSpec((tm, rqk_w // 2), lambda i: (i, 0)),
        ],
        out_specs=[
            pl.BlockSpec((tm, qk_w // 2), lambda i: (i, 0)), pl.BlockSpec((qk_w // 2, tm), lambda i: (0, i)),
            pl.BlockSpec((tm, rqk_w // 2), lambda i: (i, 0)), pl.BlockSpec((rqk_w // 2, tm), lambda i: (0, i)),
        ],
        out_shape=[
            jax.ShapeDtypeStruct((S, qk_w // 2), BF16), jax.ShapeDtypeStruct((qk_w // 2, S), BF16),
            jax.ShapeDtypeStruct((S, rqk_w // 2), BF16), jax.ShapeDtypeStruct((rqk_w // 2, S), BF16),
        ],
        compiler_params=_params(("parallel",)),
    )(proj, proj, proj, w_conv, proj, cos4, sin4)


def _log_sigmoid(t):
    return jnp.minimum(t, 0.0) - jnp.log1p(jnp.exp(-jnp.abs(t)))


def _split3(t):
    hi = t.astype(BF16)
    r1 = t - hi.astype(F32)
    mid = r1.astype(BF16)
    lo = (r1 - mid.astype(F32)).astype(BF16)
    return hi, mid, lo


GATE_ROWS = 16


def _mlstm_gates(g_ref, bias_ref, tri_ref, d):
    g_t = g_ref[...] + bias_ref[...]
    logf_t = _log_sigmoid(g_t)
    cs = jnp.dot(jnp.concatenate(_split3(logf_t), axis=0), tri_ref[1 - d], preferred_element_type=F32)
    b_t = cs[0:GATE_ROWS] + cs[GATE_ROWS:2 * GATE_ROWS] + cs[2 * GATE_ROWS:]
    tot_t = jnp.sum(logf_t, axis=1, keepdims=True)
    r_t = pltpu.roll(g_t, 4, 0) - b_t
    wend_t = tot_t + r_t
    mloc_t = jnp.max(wend_t, axis=1, keepdims=True)
    e_t = jnp.exp(wend_t - mloc_t)
    return b_t, r_t, tot_t, mloc_t, e_t


def _mixers_kernel(mqf, mktf, mvf, gf, rqf, rktf, rvf, mqb, mktb, mvb, gb, rqb, rktb, rvb,
                   bias_ref, tri_ref, sel_ref, decay_ref, inner_ref, zeta_ref,
                   hf_ref, hb_ref, yf_ref, yb_ref, c_ref, m_ref, r_ref, *, chunk_decay):
    @pl.when(pl.program_id(0) == 0)
    def _():
        c_ref[...] = jnp.zeros(c_ref.shape, F32)
        m_ref[...] = jnp.full(m_ref.shape, NEG_INIT, F32)
        r_ref[...] = jnp.zeros(r_ref.shape, F32)

    L = CHUNK
    row = lax.broadcasted_iota(jnp.int32, (L, L), 0)
    col = lax.broadcasted_iota(jnp.int32, (L, L), 1)
    ones = jnp.ones((L, ML_DIM), BF16)

    gates = [_mlstm_gates(g_ref, bias_ref, tri_ref, d) for d, g_ref in enumerate((gf, gb))]
    x_rows = jnp.concatenate([part for g in gates for part in _split3(g[0])], axis=0)
    bcast = lax.dot_general(x_rows, sel_ref[...], _TN, preferred_element_type=F32)
    for d, (q_ref, kt_ref, v_ref, y_ref) in enumerate(((rqf, rktf, rvf, yf_ref), (rqb, rktb, rvb, yb_ref))):
        for h in range(RET_HEADS):
            st = RET_HEADS * d + h
            q = q_ref[:, h * RET_QK:(h + 1) * RET_QK]
            kt = kt_ref[h * RET_QK:(h + 1) * RET_QK, :]
            v = v_ref[:, h * RET_V:(h + 1) * RET_V]
            r_st = r_ref[st]
            s = jnp.dot(q, kt, preferred_element_type=F32) * decay_ref[st]
            y = (jnp.dot(s.astype(BF16), v, preferred_element_type=F32)
                 + jnp.dot(q, r_st.astype(BF16), preferred_element_type=F32) * inner_ref[st])
            y_ref[:, h * RET_V:(h + 1) * RET_V] = y
            kw = (kt.astype(F32) * zeta_ref[st:st + 1, :]).astype(BF16)
            r_ref[st] = chunk_decay[st] * r_st + jnp.dot(kw, v, preferred_element_type=F32)

    dirs = ((mqf, mktf, mvf, hf_ref), (mqb, mktb, mvb, hb_ref))
    for d, (q_ref, kt_ref, v_ref, h_ref) in enumerate(dirs):
        b_t, r_t, tot_t, mloc_t, e_t = gates[d]
        mask = (row >= col) if d == 0 else (row <= col)
        for h in range(ML_HEADS):
            cf, st = 8 * d + 4 + h, ML_HEADS * d + h
            sl = slice(h * ML_DIM, (h + 1) * ML_DIM)
            bc = bcast[:, st * LANE:(st + 1) * LANE]
            m_st = m_ref[st:st + 1, :]
            c_st = c_ref[st]
            q = q_ref[:, sl]
            kt = kt_ref[sl, :]
            v1 = jnp.concatenate([v_ref[:, sl], ones], axis=1)

            dlog = jnp.where(mask, bc + r_t[cf:cf + 1, :], -jnp.inf)
            inter_log = bc + m_st
            m_t = jnp.maximum(jnp.max(dlog, axis=1, keepdims=True), inter_log)
            w_intra = jnp.exp(dlog - m_t)
            w_inter = jnp.exp(inter_log - m_t)
            s = jnp.dot(q, kt, preferred_element_type=F32) * w_intra
            intra = jnp.dot(s.astype(BF16), v1, preferred_element_type=F32)
            inter = jnp.dot(q, c_st.astype(BF16), preferred_element_type=F32)
            num = intra[:, :ML_DIM] + w_inter * inter[:, :ML_DIM]
            den = intra[:, ML_DIM:] + w_inter * inter[:, ML_DIM:]
            h_ref[:, sl] = num / jnp.maximum(jnp.abs(den), jnp.exp(-m_t))

            kw = (kt.astype(F32) * e_t[cf:cf + 1, :]).astype(BF16)
            c_loc = jnp.dot(kw, v1, preferred_element_type=F32)
            blast = jnp.broadcast_to(tot_t[cf:cf + 1, :], (1, LANE))
            m_loc = jnp.broadcast_to(mloc_t[cf:cf + 1, :], (1, LANE))
            m_new = jnp.maximum(blast + m_st, m_loc)
            keep = jnp.exp(blast + m_st - m_new)
            gain = jnp.exp(m_loc - m_new)
            c_ref[st] = (jnp.concatenate([keep, keep], axis=1) * c_st
                         + jnp.concatenate([gain, gain], axis=1) * c_loc)
            m_ref[st:st + 1, :] = m_new


def _gate_select_matrix():
    sel = np.zeros((2 * 3 * GATE_ROWS, 2 * ML_HEADS * LANE), np.float32)
    for d in range(2):
        for part in range(3):
            for h in range(ML_HEADS):
                st = ML_HEADS * d + h
                sel[(3 * d + part) * GATE_ROWS + 8 * d + 4 + h, st * LANE:(st + 1) * LANE] = 1.0
    return jnp.asarray(sel, BF16)


def _mixers(mq, mkt, rq, rkt, proj, gates_t, bias_t, layer, tri, sel, ret_consts):
    S = proj.shape[0]
    nc = S // CHUNK
    MW = ML_HEADS * ML_DIM
    RQ = RET_HEADS * RET_QK
    RV = RET_HEADS * RET_V
    decay, inner, zeta, chunk_decay = ret_consts
    fwd = lambda n: n
    bwd = lambda n: nc - 1 - n

    def specs(idx):
        return [
            pl.BlockSpec((CHUNK, MW), lambda n: (idx(n), 0)),
            pl.BlockSpec((MW, CHUNK), lambda n: (0, idx(n))),
            pl.BlockSpec((CHUNK, MW), lambda n: (idx(n), COL_ML_V * LANE // MW)),
            pl.BlockSpec((GATE_ROWS, CHUNK), lambda n: (0, idx(n))),
            pl.BlockSpec((CHUNK, RQ), lambda n: (idx(n), 0)),
            pl.BlockSpec((RQ, CHUNK), lambda n: (0, idx(n))),
            pl.BlockSpec((CHUNK, RV), lambda n: (idx(n), COL_RV * LANE // RV)),
        ]

    whole = lambda a: pl.BlockSpec(a.shape, lambda n: (0,) * a.ndim)
    out = lambda idx, w: pl.BlockSpec((CHUNK, w), lambda n: (idx(n), 0))
    operands = (mq, mkt, proj, gates_t, rq, rkt, proj)
    return pl.pallas_call(
        functools.partial(_mixers_kernel, chunk_decay=chunk_decay),
        name="mixers",
        grid=(nc,),
        in_specs=specs(fwd) + specs(bwd) + [
            pl.BlockSpec((None, GATE_ROWS, CHUNK), lambda n: (layer, 0, 0)),
            whole(tri), whole(sel), whole(decay), whole(inner), whole(zeta),
        ],
        out_specs=[out(fwd, MW), out(bwd, MW), out(fwd, RV), out(bwd, RV)],
        out_shape=[jax.ShapeDtypeStruct((S, MW), F32), jax.ShapeDtypeStruct((S, MW), F32),
                   jax.ShapeDtypeStruct((S, RV), F32), jax.ShapeDtypeStruct((S, RV), F32)],
        scratch_shapes=[pltpu.VMEM((2 * ML_HEADS, ML_DIM, 2 * ML_DIM), F32), pltpu.VMEM((2 * ML_HEADS, LANE), F32),
                        pltpu.VMEM((2 * RET_HEADS, RET_QK, RET_V), F32)],
        compiler_params=_params(("arbitrary",)),
    )(*operands, *operands, bias_t, tri, sel, decay, inner, zeta)


def _retention_consts():
    log_gamma = jnp.log1p(-jnp.exp2(-RET_DECAY_BASE - jnp.arange(RET_HEADS, dtype=F32)))
    idx = jnp.arange(CHUNK, dtype=F32)
    diff = idx[:, None] - idx[None, :]
    decay, inner, zeta = [], [], []
    for d in range(2):
        lg_d = log_gamma if d == 0 else log_gamma[::-1]
        dd = diff if d == 0 else -diff
        pos = idx if d == 0 else (CHUNK - 1.0 - idx)
        keep = dd >= 0
        for h in range(RET_HEADS):
            lg = lg_d[h]
            decay.append(jnp.where(keep, jnp.exp(jnp.where(keep, dd, 0.0) * lg), 0.0))
            inner.append(jnp.broadcast_to(jnp.exp((pos + 1.0) * lg)[:, None], (CHUNK, LANE)))
            zeta.append(jnp.exp((CHUNK - 1.0 - pos) * lg))
    lg_np = np.log1p(-np.exp2(-RET_DECAY_BASE - np.arange(RET_HEADS, dtype=np.float32))).astype(np.float32)
    chunk_decay = tuple(float(np.exp(np.float32(CHUNK) * lg)) for lg in list(lg_np) + list(lg_np[::-1]))
    return jnp.stack(decay), jnp.stack(inner), jnp.stack(zeta), chunk_decay


def _combine_kernel(hf, hb, o_ref, gml_ref, yf, yb, rg_ref, gret_ref, yml_out, yret_out):
    for h in range(ML_HEADS):
        sl = slice(h * ML_DIM, (h + 1) * ML_DIM)
        y = _rmsnorm(hf[:, sl] + hb[:, sl], gml_ref[:, sl])
        yml_out[:, sl] = (y / (1.0 + jnp.exp(-o_ref[:, sl].astype(F32)))).astype(yml_out.dtype)
    for h in range(RET_HEADS):
        sl = slice(h * RET_V, (h + 1) * RET_V)
        y = _rmsnorm(yf[:, sl] + yb[:, sl], gret_ref[:, sl])
        gate = rg_ref[:, sl].astype(F32)
        yret_out[:, sl] = (gate / (1.0 + jnp.exp(-gate)) * y).astype(yret_out.dtype)


def _combine(h_f, h_b, y_f, y_b, proj, g_ml, g_ret, layer, *, tm):
    S = proj.shape[0]
    W = ML_HEADS * ML_DIM
    blk = lambda c: pl.BlockSpec((tm, W), lambda i: (i, c))
    gain = pl.BlockSpec((None, 1, W), lambda i: (layer, 0, 0))
    return pl.pallas_call(
        _combine_kernel,
        name="combine",
        grid=(S // tm,),
        in_specs=[blk(0), blk(0), blk(COL_ML_O * LANE // W), gain, blk(0), blk(0), blk(COL_RG * LANE // W), gain],
        out_specs=[blk(0), blk(0)],
        out_shape=[jax.ShapeDtypeStruct((S, W), BF16), jax.ShapeDtypeStruct((S, W), BF16)],
        compiler_params=_params(("parallel",)),
    )(h_f, h_b, proj, g_ml, y_f, y_b, proj, g_ret)


def _mla_prep_kernel(cq_ref, ckv_ref, kr_ref, gq_ref, gkv_ref, wq_ref, wkv_ref, cos_ref, sin_ref,
                     q_out, k_out, v_out):
    tm = cq_ref.shape[0]
    cos4 = cos_ref[...]
    sin4 = sin_ref[...]
    qn = _rmsnorm(cq_ref[...].astype(F32), gq_ref[...]).astype(BF16)
    kvn = _rmsnorm(ckv_ref[...].astype(F32), gkv_ref[...]).astype(BF16)
    q_all = jnp.dot(qn, wq_ref[...], preferred_element_type=F32)
    kv_all = jnp.dot(kvn, wkv_ref[...], preferred_element_type=F32)
    q_scale = (MLA_QK ** -0.5) * math.log2(math.e)
    nope_w = MLA_HEADS * MLA_NOPE
    q_rope = _rope_grouped(q_all[:, nope_w:], jnp.concatenate([cos4, cos4], axis=1),
                           jnp.concatenate([sin4, sin4], axis=1)) * q_scale
    k_rope = _rope_grouped(kr_ref[...].astype(F32), cos4[:, :LANE], sin4[:, :LANE])[:, :ROPE_DIM].astype(BF16)
    ones = jnp.ones((tm, MLA_V), BF16)
    for h in range(MLA_HEADS):
        q_out[h, :, :MLA_NOPE] = (q_all[:, h * MLA_NOPE:(h + 1) * MLA_NOPE] * q_scale).astype(BF16)
        q_out[h, :, MLA_NOPE:] = q_rope[:, h * ROPE_DIM:(h + 1) * ROPE_DIM].astype(BF16)
        kv0 = h * (MLA_NOPE + MLA_V)
        k_out[h, :, :MLA_NOPE] = kv_all[:, kv0:kv0 + MLA_NOPE].astype(BF16)
        k_out[h, :, MLA_NOPE:] = k_rope
        v_out[h, :, :MLA_V] = kv_all[:, kv0 + MLA_NOPE:kv0 + MLA_NOPE + MLA_V].astype(BF16)
        v_out[h, :, MLA_V:] = ones


def _mla_prep(proj, g_q, g_kv, w_q, w_kv, layer, cos4, sin4, *, tm):
    S = proj.shape[0]
    q_rank, kv_rank = w_q.shape[1], w_kv.shape[1]
    return pl.pallas_call(
        _mla_prep_kernel,
        name="mla_prep",
        grid=(S // tm,),
        in_specs=[
            pl.BlockSpec((tm, q_rank), lambda i: (i, COL_CQ * LANE // q_rank)),
            pl.BlockSpec((tm, kv_rank), lambda i: (i, COL_CKV * LANE // kv_rank)),
            pl.BlockSpec((tm, LANE), lambda i: (i, COL_KROPE)),
            pl.BlockSpec((None, 1, q_rank), lambda i: (layer, 0, 0)),
            pl.BlockSpec((None, 1, kv_rank), lambda i: (layer, 0, 0)),
            pl.BlockSpec((None, q_rank, w_q.shape[2]), lambda i: (layer, 0, 0)),
            pl.BlockSpec((None, kv_rank, w_kv.shape[2]), lambda i: (layer, 0, 0)),
            pl.BlockSpec((tm, 2 * LANE), lambda i: (i, 0)),
            pl.BlockSpec((tm, 2 * LANE), lambda i: (i, 0)),
        ],
        out_specs=[
            pl.BlockSpec((MLA_HEADS, tm, MLA_QK), lambda i: (0, i, 0)),
            pl.BlockSpec((MLA_HEADS, tm, MLA_QK), lambda i: (0, i, 0)),
            pl.BlockSpec((MLA_HEADS, tm, 2 * MLA_V), lambda i: (0, i, 0)),
        ],
        out_shape=[
            jax.ShapeDtypeStruct((MLA_HEADS, S, MLA_QK), BF16),
            jax.ShapeDtypeStruct((MLA_HEADS, S, MLA_QK), BF16),
            jax.ShapeDtypeStruct((MLA_HEADS, S, 2 * MLA_V), BF16),
        ],
        compiler_params=_params(("parallel",)),
    )(proj, proj, proj, g_q, g_kv, w_q, w_kv, cos4, sin4)


def _attn_kernel(q_ref, k_ref, v_ref, o_ref, *, tk, nk):
    q = q_ref[...]
    m = acc = None
    for c in range(nk):
        k = k_ref[c * tk:(c + 1) * tk, :]
        v = v_ref[c * tk:(c + 1) * tk, :]
        s = lax.dot_general(q, k, _NT, preferred_element_type=F32)
        m_c = jnp.max(s, axis=-1, keepdims=True)
        if c == 0:
            m = m_c
            acc = jnp.dot(jnp.exp2(s - m).astype(BF16), v, preferred_element_type=F32)
        else:
            m_new = jnp.maximum(m, m_c)
            acc = jnp.exp2(m - m_new) * acc + jnp.dot(jnp.exp2(s - m_new).astype(BF16), v, preferred_element_type=F32)
            m = m_new
    o_ref[...] = (acc[:, :MLA_V] / acc[:, MLA_V:]).astype(o_ref.dtype)


def _attention(q, k, v, *, tq, tk):
    S = q.shape[1]
    return pl.pallas_call(
        functools.partial(_attn_kernel, tk=tk, nk=S // tk),
        name="attention",
        grid=(MLA_HEADS, S // tq),
        in_specs=[
            pl.BlockSpec((None, tq, MLA_QK), lambda h, i: (h, i, 0)),
            pl.BlockSpec((None, S, MLA_QK), lambda h, i: (h, 0, 0)),
            pl.BlockSpec((None, S, 2 * MLA_V), lambda h, i: (h, 0, 0)),
        ],
        out_specs=pl.BlockSpec((tq, MLA_V), lambda h, i: (i, h)),
        out_shape=jax.ShapeDtypeStruct((S, MLA_HEADS * MLA_V), BF16),
        compiler_params=_params(("parallel", "arbitrary")),
    )(q, k, v)


def _pack_w_in_kernel(w_ref, o_ref):
    tr, n_in = w_ref.shape
    ml_end = 4 * ML_HEADS * ML_DIM
    gates = 4 * ML_HEADS
    rest = n_in - ml_end - gates
    assert ml_end + rest + (LANE - ROPE_DIM) + LANE == IN_WIDTH_PACKED
    w = w_ref[...]
    o_ref[:, :ml_end] = w[:, :ml_end].astype(BF16)
    o_ref[:, ml_end:ml_end + rest] = w[:, ml_end + gates:].astype(BF16)
    o_ref[:, ml_end + rest:IN_WIDTH_PACKED - LANE] = jnp.zeros((tr, LANE - ROPE_DIM), BF16)
    o_ref[:, IN_WIDTH_PACKED - LANE:] = jnp.concatenate(
        [w[:, ml_end:ml_end + gates], jnp.zeros((tr, LANE - gates), F32)], axis=1).astype(BF16)


def _pack_w_in(w_in, *, tr):
    L, K, n_in = w_in.shape
    return pl.pallas_call(
        _pack_w_in_kernel,
        name="pack_w_in",
        grid=(L, K // tr),
        in_specs=[pl.BlockSpec((None, tr, n_in), lambda l, i: (l, i, 0))],
        out_specs=pl.BlockSpec((None, tr, IN_WIDTH_PACKED), lambda l, i: (l, i, 0)),
        out_shape=jax.ShapeDtypeStruct((L, K, IN_WIDTH_PACKED), BF16),
        compiler_params=_params(("parallel", "parallel")),
    )(w_in)


def kernel(x, positions, g_mix, w_in, b_gates, w_conv, g_ml_out, g_ret_out, g_q_norm, w_q_up, g_kv_norm, w_kv_up,
           w_out, g_ffn, w_ff1, w_ff2, g_final):
    B, S, D = x.shape
    assert B == 1
    depth = w_in.shape[0]
    tm = min(512, S)
    tm_big = min(1024, S)

    half = ROPE_DIM // 2
    inv = ROPE_THETA ** (-jnp.arange(half, dtype=F32) / half)
    ang = positions[0].astype(F32)[:, None] * inv
    cos4 = jnp.tile(jnp.concatenate([jnp.cos(ang), jnp.cos(ang)], axis=-1), (1, RET_HEADS))
    sin4 = jnp.tile(jnp.concatenate([-jnp.sin(ang), jnp.sin(ang)], axis=-1), (1, RET_HEADS))

    w_in_p = _pack_w_in(w_in, tr=256)
    q_rank = w_q_up.shape[1]
    w_q4 = w_q_up.astype(BF16).reshape(depth, q_rank, MLA_HEADS, MLA_QK)
    w_q = jnp.concatenate([w_q4[..., :MLA_NOPE].reshape(depth, q_rank, MLA_HEADS * MLA_NOPE),
                           w_q4[..., MLA_NOPE:].reshape(depth, q_rank, MLA_HEADS * ROPE_DIM)], axis=-1)
    w_kv = w_kv_up.astype(BF16)
    w_out_b = w_out.astype(BF16)
    w_ff1_b = w_ff1.astype(BF16)
    w_ff2_b = w_ff2.astype(BF16)
    bias_t = jnp.broadcast_to(b_gates[:, :, None], (depth, GATE_ROWS, CHUNK))
    idx = jnp.arange(CHUNK)
    tri = jnp.stack([idx[:, None] >= idx[None, :], idx[:, None] <= idx[None, :]]).astype(BF16)
    sel = _gate_select_matrix()
    ret_consts = _retention_consts()
    r3 = lambda g: g[:, None, :]

    xs = x[0]
    for l in range(depth):
        proj, gates_t = _in_proj(xs, r3(g_mix), w_in_p, l, tm=tm_big, tn=1536)
        mq, mkt, rq, rkt = _prep(proj, w_conv, l, cos4, sin4, tm=tm)
        h_f, h_b, y_f, y_b = _mixers(mq, mkt, rq, rkt, proj, gates_t, bias_t, l, tri, sel, ret_consts)
        y_ml, y_ret = _combine(h_f, h_b, y_f, y_b, proj, r3(g_ml_out), r3(g_ret_out), l, tm=tm)
        q, k, v = _mla_prep(proj, r3(g_q_norm), r3(g_kv_norm), w_q, w_kv, l, cos4, sin4, tm=tm)
        y_mla = _attention(q, k, v, tq=min(1024, S), tk=min(512, S))
        xs = _mm_res([y_ml, y_ret, y_mla], w_out_b, l, xs, tm=tm_big, tn=1024)
        act = _ffn1(xs, r3(g_ffn), w_ff1_b, l, tm=tm_big, tn=1024)
        xs = _ffn2(act, w_ff2_b, l, xs, tm=tm_big, tn=1024, tk=2048)
    return _final_rms(xs, g_final[None, :], tm=tm)[None]
```

```python
import functools
import math

import numpy as np
import jax
import jax.numpy as jnp
from jax import lax
from jax.experimental import pallas as pl
from jax.experimental.pallas import tpu as pltpu

F32 = jnp.float32
BF16 = jnp.bfloat16

EPS = 1e-6
NEG_INIT = -1e30
ROPE_THETA = 10000.0
CHUNK = 128
LANE = 128
BF16_ROWS = 16
ML_HEADS = 4
ML_DIM = 128
RET_HEADS = 4
RET_QK = 64
RET_V = 128
RET_DECAY_BASE = 5.0
MLA_HEADS = 8
MLA_NOPE = 128
ROPE_DIM = 64
MLA_V = 128
MLA_QK = MLA_NOPE + ROPE_DIM
VMEM_LIMIT = 56 * 1024 * 1024

COL_ML_Q, COL_ML_K, COL_ML_V, COL_ML_O = 0, 4, 8, 12
COL_RQK, COL_RV, COL_RG = 16, 20, 24
COL_CQ, COL_CKV, COL_KROPE, COL_GATES = 28, 32, 34, 35
IN_WIDTH_PACKED = 36 * LANE

_NT = (((1,), (1,)), ((), ()))
_TN = (((0,), (0,)), ((), ()))


def _params(sem):
    return pltpu.CompilerParams(dimension_semantics=sem, vmem_limit_bytes=VMEM_LIMIT)


def _rmsnorm(x, g):
    return x * lax.rsqrt(jnp.mean(x * x, axis=-1, keepdims=True) + EPS) * g


def _in_proj_kernel(x_ref, g_ref, w_ref, o_ref, gates_ref, hn_ref):
    j = pl.program_id(1)

    @pl.when(j == 0)
    def _():
        hn_ref[...] = _rmsnorm(x_ref[...], g_ref[...]).astype(hn_ref.dtype)

    y = jnp.dot(hn_ref[...], w_ref[...], preferred_element_type=F32)
    o_ref[...] = y.astype(o_ref.dtype)

    @pl.when(j == pl.num_programs(1) - 1)
    def _():
        gates_ref[...] = y[:, y.shape[1] - LANE:].T


def _in_proj(x, g, w, layer, *, tm, tn):
    S, K = x.shape
    N = w.shape[-1]
    return pl.pallas_call(
        _in_proj_kernel,
        name="in_proj",
        grid=(S // tm, N // tn),
        in_specs=[
            pl.BlockSpec((tm, K), lambda i, j: (i, 0)),
            pl.BlockSpec((None, 1, K), lambda i, j: (layer, 0, 0)),
            pl.BlockSpec((None, K, tn), lambda i, j: (layer, 0, j)),
        ],
        out_specs=[pl.BlockSpec((tm, tn), lambda i, j: (i, j)), pl.BlockSpec((LANE, tm), lambda i, j: (0, i))],
        out_shape=[jax.ShapeDtypeStruct((S, N), BF16), jax.ShapeDtypeStruct((LANE, S), F32)],
        scratch_shapes=[pltpu.VMEM((tm, K), BF16)],
        compiler_params=_params(("parallel", "arbitrary")),
    )(x, g, w)


def _mm_res_kernel(*refs, n_pairs):
    a_refs = refs[:n_pairs]
    w_refs = refs[n_pairs:2 * n_pairs]
    x_ref, o_ref = refs[2 * n_pairs], refs[2 * n_pairs + 1]
    acc = x_ref[...]
    for a_ref, w_ref in zip(a_refs, w_refs):
        acc = acc + jnp.dot(a_ref[...], w_ref[...], preferred_element_type=F32)
    o_ref[...] = acc


def _mm_res(acts, w, x, *, tm, tn):
    S, N = x.shape
    n_pairs = len(acts)
    a_specs, w_specs = [], []
    row = 0
    for a in acts:
        k = a.shape[1]
        assert row % k == 0
        a_specs.append(pl.BlockSpec((tm, k), lambda i, j: (i, 0)))
        w_specs.append(pl.BlockSpec((k, tn), lambda i, j, rb=row // k: (rb, j)))
        row += k
    assert row == w.shape[0]
    return pl.pallas_call(
        functools.partial(_mm_res_kernel, n_pairs=n_pairs),
        name="out_proj",
        grid=(S // tm, N // tn),
        in_specs=a_specs + w_specs + [pl.BlockSpec((tm, tn), lambda i, j: (i, j))],
        out_specs=pl.BlockSpec((tm, tn), lambda i, j: (i, j)),
        out_shape=jax.ShapeDtypeStruct((S, N), F32),
        compiler_params=_params(("parallel", "arbitrary")),
    )(*acts, *([w] * n_pairs), x)


def _ffn1_kernel(x_ref, g_ref, w_ref, o_ref, hn_ref):
    @pl.when(pl.program_id(1) == 0)
    def _():
        hn_ref[...] = _rmsnorm(x_ref[...], g_ref[...]).astype(hn_ref.dtype)

    y = jnp.dot(hn_ref[...], w_ref[...], preferred_element_type=F32)
    o_ref[...] = jnp.square(jnp.maximum(y, 0.0)).astype(o_ref.dtype)


def _ffn1(x, g, w, layer, *, tm, tn):
    S, K = x.shape
    N = w.shape[-1]
    return pl.pallas_call(
        _ffn1_kernel,
        name="ffn1",
        grid=(S // tm, N // tn),
        in_specs=[
            pl.BlockSpec((tm, K), lambda i, j: (i, 0)),
            pl.BlockSpec((None, 1, K), lambda i, j: (layer, 0, 0)),
            pl.BlockSpec((K, tn), lambda i, j: (0, j)),
        ],
        out_specs=pl.BlockSpec((tm, tn), lambda i, j: (i, j)),
        out_shape=jax.ShapeDtypeStruct((S, N), BF16),
        scratch_shapes=[pltpu.VMEM((tm, K), BF16)],
        compiler_params=_params(("parallel", "arbitrary")),
    )(x, g, w)


def _ffn2_kernel(a_ref, w_ref, x_ref, o_ref):
    @pl.when(pl.program_id(2) == 0)
    def _():
        o_ref[...] = x_ref[...]

    o_ref[...] += jnp.dot(a_ref[...], w_ref[...], preferred_element_type=F32)


def _ffn2(a, w, x, *, tm, tn, tk):
    S, N = x.shape
    K = a.shape[1]
    return pl.pallas_call(
        _ffn2_kernel,
        name="ffn2",
        grid=(S // tm, N // tn, K // tk),
        in_specs=[
            pl.BlockSpec((tm, tk), lambda i, j, k: (i, k)),
            pl.BlockSpec((tk, tn), lambda i, j, k: (k, j)),
            pl.BlockSpec((tm, tn), lambda i, j, k: (i, j)),
        ],
        out_specs=pl.BlockSpec((tm, tn), lambda i, j, k: (i, j)),
        out_shape=jax.ShapeDtypeStruct((S, N), F32),
        compiler_params=_params(("parallel", "arbitrary", "arbitrary")),
    )(a, w, x)


def _final_rms_kernel(x_ref, g_ref, o_ref):
    o_ref[...] = _rmsnorm(x_ref[...], g_ref[...])


def _final_rms(x, g, *, tm):
    S, K = x.shape
    return pl.pallas_call(
        _final_rms_kernel,
        name="final_rms",
        grid=(S // tm,),
        in_specs=[pl.BlockSpec((tm, K), lambda i: (i, 0)), pl.BlockSpec((1, K), lambda i: (0, 0))],
        out_specs=pl.BlockSpec((tm, K), lambda i: (i, 0)),
        out_shape=jax.ShapeDtypeStruct((S, K), F32),
        compiler_params=_params(("parallel",)),
    )(x, g)


def _rope_grouped(t, cosf, sinf):
    width = t.shape[1]
    lane = lax.broadcasted_iota(jnp.int32, t.shape, 1)
    first_half = (lane % ROPE_DIM) < (ROPE_DIM // 2)
    partner = jnp.where(first_half, pltpu.roll(t, width - ROPE_DIM // 2, 1), pltpu.roll(t, ROPE_DIM // 2, 1))
    return t * cosf + partner * sinf


def _prep_kernel(qk_ref, prev_ref, next_ref, wc_ref, rqk_ref, cos_ref, sin_ref, mq_out, mkt_out, rq_out, rkt_out,
                 *, tm, nblk):
    i = pl.program_id(0)
    x = qk_ref[...].astype(F32)
    row = lax.broadcasted_iota(jnp.int32, x.shape, 0)
    prev_row = jnp.where(i > 0, prev_ref[BF16_ROWS - 1:BF16_ROWS, :].astype(F32), 0.0)
    next_row = jnp.where(i < nblk - 1, next_ref[0:1, :].astype(F32), 0.0)
    x_m = jnp.where(row == 0, prev_row, pltpu.roll(x, 1, 0))
    x_p = jnp.where(row == tm - 1, next_row, pltpu.roll(x, tm - 1, 0))
    w = wc_ref[...]
    y = x_m * w[0:1, :] + x * w[1:2, :] + x_p * w[2:3, :]
    y = y / (1.0 + jnp.exp(-y))
    half = ML_HEADS * ML_DIM
    mq_out[...] = (y[:, :half] * (ML_DIM ** -0.5)).astype(mq_out.dtype)
    mkt_out[...] = y[:, half:].T.astype(mkt_out.dtype)

    cosf = cos_ref[...]
    sinf = sin_ref[...]
    width = RET_HEADS * RET_QK
    r = rqk_ref[...].astype(F32)
    rq_out[...] = _rope_grouped(r[:, :width], cosf, sinf).astype(rq_out.dtype)
    rkt_out[...] = (_rope_grouped(r[:, width:], cosf, sinf) * (RET_QK ** -0.5)).T.astype(rkt_out.dtype)


def _prep(proj, w_conv, layer, cos4, sin4, *, tm):
    S = proj.shape[0]
    nblk = S // tm
    qk_w = 2 * ML_HEADS * ML_DIM
    rqk_w = 2 * RET_HEADS * RET_QK
    hb = tm // BF16_ROWS
    return pl.pallas_call(
        functools.partial(_prep_kernel, tm=tm, nblk=nblk),
        name="prep",
        grid=(nblk,),
        in_specs=[
            pl.BlockSpec((tm, qk_w), lambda i: (i, 0)),
            pl.BlockSpec((BF16_ROWS, qk_w), lambda i: (jnp.maximum(i * hb - 1, 0), 0)),
            pl.BlockSpec((BF16_ROWS, qk_w), lambda i: (jnp.minimum((i + 1) * hb, S // BF16_ROWS - 1), 0)),
            pl.BlockSpec((None, 3, qk_w), lambda i: (layer, 0, 0)),
            pl.BlockSpec((tm, rqk_w), lambda i: (i, COL_RQK * LANE // rqk_w)),
            pl.BlockSpec((tm, rqk_w // 2), lambda i: (i, 0)),
            pl.BlockSpec((tm, rqk_w // 2), lambda i: (i, 0)),
        ],
        out_specs=[
            pl.BlockSpec((tm, qk_w // 2), lambda i: (i, 0)), pl.BlockSpec((qk_w // 2, tm), lambda i: (0, i)),
            pl.BlockSpec((tm, rqk_w // 2), lambda i: (i, 0)), pl.BlockSpec((rqk_w // 2, tm), lambda i: (0, i)),
        ],
        out_shape=[
            jax.ShapeDtypeStruct((S, qk_w // 2), BF16), jax.ShapeDtypeStruct((qk_w // 2, S), BF16),
            jax.ShapeDtypeStruct((S, rqk_w // 2), BF16), jax.ShapeDtypeStruct((rqk_w // 2, S), BF16),
        ],
        compiler_params=_params(("parallel",)),
    )(proj, proj, proj, w_conv, proj, cos4, sin4)


def _log_sigmoid(t):
    return jnp.minimum(t, 0.0) - jnp.log1p(jnp.exp(-jnp.abs(t)))


def _split3(t):
    hi = t.astype(BF16)
    r1 = t - hi.astype(F32)
    mid = r1.astype(BF16)
    lo = (r1 - mid.astype(F32)).astype(BF16)
    return hi, mid, lo


GATE_ROWS = 16


def _mlstm_gates(g_ref, bias_ref, tri_ref, d):
    g_t = g_ref[...] + bias_ref[...]
    logf_t = _log_sigmoid(g_t)
    cs = jnp.dot(jnp.concatenate(_split3(logf_t), axis=0), tri_ref[1 - d], preferred_element_type=F32)
    b_t = cs[0:GATE_ROWS] + cs[GATE_ROWS:2 * GATE_ROWS] + cs[2 * GATE_ROWS:]
    tot_t = jnp.sum(logf_t, axis=1, keepdims=True)
    r_t = pltpu.roll(g_t, 4, 0) - b_t
    wend_t = tot_t + r_t
    mloc_t = jnp.max(wend_t, axis=1, keepdims=True)
    e_t = jnp.exp(wend_t - mloc_t)
    return b_t, r_t, tot_t, mloc_t, e_t


def _mixers_kernel(mqf, mktf, mvf, gf, rqf, rktf, rvf, mqb, mktb, mvb, gb, rqb, rktb, rvb,
                   bias_ref, tri_ref, sel_ref, decay_ref, inner_ref, zeta_ref,
                   hf_ref, hb_ref, yf_ref, yb_ref, c_ref, m_ref, r_ref, *, chunk_decay):
    @pl.when(pl.program_id(0) == 0)
    def _():
        c_ref[...] = jnp.zeros(c_ref.shape, F32)
        m_ref[...] = jnp.full(m_ref.shape, NEG_INIT, F32)
        r_ref[...] = jnp.zeros(r_ref.shape, F32)

    L = CHUNK
    row = lax.broadcasted_iota(jnp.int32, (L, L), 0)
    col = lax.broadcasted_iota(jnp.int32, (L, L), 1)
    ones = jnp.ones((L, ML_DIM), BF16)

    gates = [_mlstm_gates(g_ref, bias_ref, tri_ref, d) for d, g_ref in enumerate((gf, gb))]
    x_rows = jnp.concatenate([part for g in gates for part in _split3(g[0])], axis=0)
    bcast = lax.dot_general(x_rows, sel_ref[...], _TN, preferred_element_type=F32)
    for d, (q_ref, kt_ref, v_ref, y_ref) in enumerate(((rqf, rktf, rvf, yf_ref), (rqb, rktb, rvb, yb_ref))):
        for h in range(RET_HEADS):
            st = RET_HEADS * d + h
            q = q_ref[:, h * RET_QK:(h + 1) * RET_QK]
            kt = kt_ref[h * RET_QK:(h + 1) * RET_QK, :]
            v = v_ref[:, h * RET_V:(h + 1) * RET_V]
            r_st = r_ref[st]
            s = jnp.dot(q, kt, preferred_element_type=F32) * decay_ref[st]
            y = (jnp.dot(s.astype(BF16), v, preferred_element_type=F32)
                 + jnp.dot(q, r_st.astype(BF16), preferred_element_type=F32) * inner_ref[st])
            y_ref[:, h * RET_V:(h + 1) * RET_V] = y
            kw = (kt.astype(F32) * zeta_ref[st:st + 1, :]).astype(BF16)
            r_ref[st] = chunk_decay[st] * r_st + jnp.dot(kw, v, preferred_element_type=F32)

    dirs = ((mqf, mktf, mvf, hf_ref), (mqb, mktb, mvb, hb_ref))
    for d, (q_ref, kt_ref, v_ref, h_ref) in enumerate(dirs):
        b_t, r_t, tot_t, mloc_t, e_t = gates[d]
        mask = (row >= col) if d == 0 else (row <= col)
        for h in range(ML_HEADS):
            cf, st = 8 * d + 4 + h, ML_HEADS * d + h
            sl = slice(h * ML_DIM, (h + 1) * ML_DIM)
            bc = bcast[:, st * LANE:(st + 1) * LANE]
            m_st = m_ref[st:st + 1, :]
            c_st = c_ref[st]
            q = q_ref[:, sl]
            kt = kt_ref[sl, :]
            v1 = jnp.concatenate([v_ref[:, sl], ones], axis=1)

            dlog = jnp.where(mask, bc + r_t[cf:cf + 1, :], -jnp.inf)
            inter_log = bc + m_st
            m_t = jnp.maximum(jnp.max(dlog, axis=1, keepdims=True), inter_log)
            w_intra = jnp.exp(dlog - m_t)
            w_inter = jnp.exp(inter_log - m_t)
            s = jnp.dot(q, kt, preferred_element_type=F32) * w_intra
            intra = jnp.dot(s.astype(BF16), v1, preferred_element_type=F32)
            inter = jnp.dot(q, c_st.astype(BF16), preferred_element_type=F32)
            num = intra[:, :ML_DIM] + w_inter * inter[:, :ML_DIM]
            den = intra[:, ML_DIM:] + w_inter * inter[:, ML_DIM:]
            h_ref[:, sl] = num / jnp.maximum(jnp.abs(den), jnp.exp(-m_t))

            kw = (kt.astype(F32) * e_t[cf:cf + 1, :]).astype(BF16)
            c_loc = jnp.dot(kw, v1, preferred_element_type=F32)
            blast = jnp.broadcast_to(tot_t[cf:cf + 1, :], (1, LANE))
            m_loc = jnp.broadcast_to(mloc_t[cf:cf + 1, :], (1, LANE))
            m_new = jnp.maximum(blast + m_st, m_loc)
            keep = jnp.exp(blast + m_st - m_new)
            gain = jnp.exp(m_loc - m_new)
            c_ref[st] = (jnp.concatenate([keep, keep], axis=1) * c_st
                         + jnp.concatenate([gain, gain], axis=1) * c_loc)
            m_ref[st:st + 1, :] = m_new


def _gate_select_matrix():
    sel = np.zeros((2 * 3 * GATE_ROWS, 2 * ML_HEADS * LANE), np.float32)
    for d in range(2):
        for part in range(3):
            for h in range(ML_HEADS):
                st = ML_HEADS * d + h
                sel[(3 * d + part) * GATE_ROWS + 8 * d + 4 + h, st * LANE:(st + 1) * LANE] = 1.0
    return jnp.asarray(sel, BF16)


def _mixers(mq, mkt, rq, rkt, proj, gates_t, bias_t, layer, tri, sel, ret_consts):
    S = proj.shape[0]
    nc = S // CHUNK
    MW = ML_HEADS * ML_DIM
    RQ = RET_HEADS * RET_QK
    RV = RET_HEADS * RET_V
    decay, inner, zeta, chunk_decay = ret_consts
    fwd = lambda n: n
    bwd = lambda n: nc - 1 - n

    def specs(idx):
        return [
            pl.BlockSpec((CHUNK, MW), lambda n: (idx(n), 0)),
            pl.BlockSpec((MW, CHUNK), lambda n: (0, idx(n))),
            pl.BlockSpec((CHUNK, MW), lambda n: (idx(n), COL_ML_V * LANE // MW)),
            pl.BlockSpec((GATE_ROWS, CHUNK), lambda n: (0, idx(n))),
            pl.BlockSpec((CHUNK, RQ), lambda n: (idx(n), 0)),
            pl.BlockSpec((RQ, CHUNK), lambda n: (0, idx(n))),
            pl.BlockSpec((CHUNK, RV), lambda n: (idx(n), COL_RV * LANE // RV)),
        ]

    whole = lambda a: pl.BlockSpec(a.shape, lambda n: (0,) * a.ndim)
    out = lambda idx, w: pl.BlockSpec((CHUNK, w), lambda n: (idx(n), 0))
    operands = (mq, mkt, proj, gates_t, rq, rkt, proj)
    return pl.pallas_call(
        functools.partial(_mixers_kernel, chunk_decay=chunk_decay),
        name="mixers",
        grid=(nc,),
        in_specs=specs(fwd) + specs(bwd) + [
            pl.BlockSpec((None, GATE_ROWS, CHUNK), lambda n: (layer, 0, 0)),
            whole(tri), whole(sel), whole(decay), whole(inner), whole(zeta),
        ],
        out_specs=[out(fwd, MW), out(bwd, MW), out(fwd, RV), out(bwd, RV)],
        out_shape=[jax.ShapeDtypeStruct((S, MW), F32), jax.ShapeDtypeStruct((S, MW), F32),
                   jax.ShapeDtypeStruct((S, RV), F32), jax.ShapeDtypeStruct((S, RV), F32)],
        scratch_shapes=[pltpu.VMEM((2 * ML_HEADS, ML_DIM, 2 * ML_DIM), F32), pltpu.VMEM((2 * ML_HEADS, LANE), F32),
                        pltpu.VMEM((2 * RET_HEADS, RET_QK, RET_V), F32)],
        compiler_params=_params(("arbitrary",)),
    )(*operands, *operands, bias_t, tri, sel, decay, inner, zeta)


def _retention_consts():
    log_gamma = jnp.log1p(-jnp.exp2(-RET_DECAY_BASE - jnp.arange(RET_HEADS, dtype=F32)))
    idx = jnp.arange(CHUNK, dtype=F32)
    diff = idx[:, None] - idx[None, :]
    decay, inner, zeta = [], [], []
    for d in range(2):
        lg_d = log_gamma if d == 0 else log_gamma[::-1]
        dd = diff if d == 0 else -diff
        pos = idx if d == 0 else (CHUNK - 1.0 - idx)
        keep = dd >= 0
        for h in range(RET_HEADS):
            lg = lg_d[h]
            decay.append(jnp.where(keep, jnp.exp(jnp.where(keep, dd, 0.0) * lg), 0.0))
            inner.append(jnp.broadcast_to(jnp.exp((pos + 1.0) * lg)[:, None], (CHUNK, LANE)))
            zeta.append(jnp.exp((CHUNK - 1.0 - pos) * lg))
    lg_np = np.log1p(-np.exp2(-RET_DECAY_BASE - np.arange(RET_HEADS, dtype=np.float32))).astype(np.float32)
    chunk_decay = tuple(float(np.exp(np.float32(CHUNK) * lg)) for lg in list(lg_np) + list(lg_np[::-1]))
    return jnp.stack(decay), jnp.stack(inner), jnp.stack(zeta), chunk_decay


def _combine_kernel(hf, hb, o_ref, gml_ref, yf, yb, rg_ref, gret_ref, yml_out, yret_out):
    for h in range(ML_HEADS):
        sl = slice(h * ML_DIM, (h + 1) * ML_DIM)
        y = _rmsnorm(hf[:, sl] + hb[:, sl], gml_ref[:, sl])
        yml_out[:, sl] = (y / (1.0 + jnp.exp(-o_ref[:, sl].astype(F32)))).astype(yml_out.dtype)
    for h in range(RET_HEADS):
        sl = slice(h * RET_V, (h + 1) * RET_V)
        y = _rmsnorm(yf[:, sl] + yb[:, sl], gret_ref[:, sl])
        gate = rg_ref[:, sl].astype(F32)
        yret_out[:, sl] = (gate / (1.0 + jnp.exp(-gate)) * y).astype(yret_out.dtype)


def _combine(h_f, h_b, y_f, y_b, proj, g_ml, g_ret, layer, *, tm):
    S = proj.shape[0]
    W = ML_HEADS * ML_DIM
    blk = lambda c: pl.BlockSpec((tm, W), lambda i: (i, c))
    gain = pl.BlockSpec((None, 1, W), lambda i: (layer, 0, 0))
    return pl.pallas_call(
        _combine_kernel,
        name="combine",
        grid=(S // tm,),
        in_specs=[blk(0), blk(0), blk(COL_ML_O * LANE // W), gain, blk(0), blk(0), blk(COL_RG * LANE // W), gain],
        out_specs=[blk(0), blk(0)],
        out_shape=[jax.ShapeDtypeStruct((S, W), BF16), jax.ShapeDtypeStruct((S, W), BF16)],
        compiler_params=_params(("parallel",)),
    )(h_f, h_b, proj, g_ml, y_f, y_b, proj, g_ret)


def _mla_prep_kernel(cq_ref, ckv_ref, kr_ref, gq_ref, gkv_ref, wq_ref, wkv_ref, cos_ref, sin_ref,
                     q_out, k_out, v_out):
    tm = cq_ref.shape[0]
    cos4 = cos_ref[...]
    sin4 = sin_ref[...]
    qn = _rmsnorm(cq_ref[...].astype(F32), gq_ref[...]).astype(BF16)
    kvn = _rmsnorm(ckv_ref[...].astype(F32), gkv_ref[...]).astype(BF16)
    q_all = jnp.dot(qn, wq_ref[...], preferred_element_type=F32)
    kv_all = jnp.dot(kvn, wkv_ref[...], preferred_element_type=F32)
    q_scale = (MLA_QK ** -0.5) * math.log2(math.e)
    nope_w = MLA_HEADS * MLA_NOPE
    q_rope = _rope_grouped(q_all[:, nope_w:], jnp.concatenate([cos4, cos4], axis=1),
                           jnp.concatenate([sin4, sin4], axis=1)) * q_scale
    k_rope = _rope_grouped(kr_ref[...].astype(F32), cos4[:, :LANE], sin4[:, :LANE])[:, :ROPE_DIM].astype(BF16)
    ones = jnp.ones((tm, MLA_V), BF16)
    for h in range(MLA_HEADS):
        q_out[h, :, :MLA_NOPE] = (q_all[:, h * MLA_NOPE:(h + 1) * MLA_NOPE] * q_scale).astype(BF16)
        q_out[h, :, MLA_NOPE:] = q_rope[:, h * ROPE_DIM:(h + 1) * ROPE_DIM].astype(BF16)
        kv0 = h * (MLA_NOPE + MLA_V)
        k_out[h, :, :MLA_NOPE] = kv_all[:, kv0:kv0 + MLA_NOPE].astype(BF16)
        k_out[h, :, MLA_NOPE:] = k_rope
        v_out[h, :, :MLA_V] = kv_all[:, kv0 + MLA_NOPE:kv0 + MLA_NOPE + MLA_V].astype(BF16)
        v_out[h, :, MLA_V:] = ones


def _mla_prep(proj, g_q, g_kv, w_q, w_kv, layer, cos4, sin4, *, tm):
    S = proj.shape[0]
    q_rank, kv_rank = w_q.shape[1], w_kv.shape[1]
    return pl.pallas_call(
        _mla_prep_kernel,
        name="mla_prep",
        grid=(S // tm,),
        in_specs=[
            pl.BlockSpec((tm, q_rank), lambda i: (i, COL_CQ * LANE // q_rank)),
            pl.BlockSpec((tm, kv_rank), lambda i: (i, COL_CKV * LANE // kv_rank)),
            pl.BlockSpec((tm, LANE), lambda i: (i, COL_KROPE)),
            pl.BlockSpec((None, 1, q_rank), lambda i: (layer, 0, 0)),
            pl.BlockSpec((None, 1, kv_rank), lambda i: (layer, 0, 0)),
            pl.BlockSpec((None, q_rank, w_q.shape[2]), lambda i: (layer, 0, 0)),
            pl.BlockSpec((None, kv_rank, w_kv.shape[2]), lambda i: (layer, 0, 0)),
            pl.BlockSpec((tm, 2 * LANE), lambda i: (i, 0)),
            pl.BlockSpec((tm, 2 * LANE), lambda i: (i, 0)),
        ],
        out_specs=[
            pl.BlockSpec((MLA_HEADS, tm, MLA_QK), lambda i: (0, i, 0)),
            pl.BlockSpec((MLA_HEADS, tm, MLA_QK), lambda i: (0, i, 0)),
            pl.BlockSpec((MLA_HEADS, tm, 2 * MLA_V), lambda i: (0, i, 0)),
        ],
        out_shape=[
            jax.ShapeDtypeStruct((MLA_HEADS, S, MLA_QK), BF16),
            jax.ShapeDtypeStruct((MLA_HEADS, S, MLA_QK), BF16),
            jax.ShapeDtypeStruct((MLA_HEADS, S, 2 * MLA_V), BF16),
        ],
        compiler_params=_params(("parallel",)),
    )(proj, proj, proj, g_q, g_kv, w_q, w_kv, cos4, sin4)


def _attn_kernel(q_ref, k_ref, v_ref, *rest, tk, nk, n_cast):
    w_refs, o_ref, wb_refs = rest[:n_cast], rest[n_cast], rest[n_cast + 1:]
    for w_ref, wb_ref in zip(w_refs, wb_refs):
        wb_ref[...] = w_ref[...].astype(wb_ref.dtype)

    q = q_ref[...]
    m = acc = None
    for c in range(nk):
        k = k_ref[c * tk:(c + 1) * tk, :]
        v = v_ref[c * tk:(c + 1) * tk, :]
        s = lax.dot_general(q, k, _NT, preferred_element_type=F32)
        m_c = jnp.max(s, axis=-1, keepdims=True)
        if c == 0:
            m = m_c
            acc = jnp.dot(jnp.exp2(s - m).astype(BF16), v, preferred_element_type=F32)
        else:
            m_new = jnp.maximum(m, m_c)
            acc = jnp.exp2(m - m_new) * acc + jnp.dot(jnp.exp2(s - m_new).astype(BF16), v, preferred_element_type=F32)
            m = m_new
    o_ref[...] = (acc[:, :MLA_V] / acc[:, MLA_V:]).astype(o_ref.dtype)


def _attention(q, k, v, cast_weights, layer, *, tq, tk):
    S = q.shape[1]
    nq = S // tq
    n_steps = MLA_HEADS * nq
    w_specs, wb_specs, wb_shapes = [], [], []
    for w in cast_weights:
        _, rows, cols = w.shape
        rb = rows // n_steps
        assert rb * n_steps == rows and rb % BF16_ROWS == 0
        w_specs.append(pl.BlockSpec((None, rb, cols), lambda h, i: (layer, h * nq + i, 0)))
        wb_specs.append(pl.BlockSpec((rb, cols), lambda h, i: (h * nq + i, 0)))
        wb_shapes.append(jax.ShapeDtypeStruct((rows, cols), BF16))
    outs = pl.pallas_call(
        functools.partial(_attn_kernel, tk=tk, nk=S // tk, n_cast=len(cast_weights)),
        name="attention",
        grid=(MLA_HEADS, nq),
        in_specs=[
            pl.BlockSpec((None, tq, MLA_QK), lambda h, i: (h, i, 0)),
            pl.BlockSpec((None, S, MLA_QK), lambda h, i: (h, 0, 0)),
            pl.BlockSpec((None, S, 2 * MLA_V), lambda h, i: (h, 0, 0)),
        ] + w_specs,
        out_specs=[pl.BlockSpec((tq, MLA_V), lambda h, i: (i, h))] + wb_specs,
        out_shape=[jax.ShapeDtypeStruct((S, MLA_HEADS * MLA_V), BF16)] + wb_shapes,
        compiler_params=_params(("parallel", "arbitrary")),
    )(q, k, v, *cast_weights)
    return outs[0], outs[1:]


def _pack_w_in_kernel(w_ref, o_ref):
    tr, n_in = w_ref.shape
    ml_end = 4 * ML_HEADS * ML_DIM
    gates = 4 * ML_HEADS
    rest = n_in - ml_end - gates
    assert ml_end + rest + (LANE - ROPE_DIM) + LANE == IN_WIDTH_PACKED
    w = w_ref[...]
    o_ref[:, :ml_end] = w[:, :ml_end].astype(BF16)
    o_ref[:, ml_end:ml_end + rest] = w[:, ml_end + gates:].astype(BF16)
    o_ref[:, ml_end + rest:IN_WIDTH_PACKED - LANE] = jnp.zeros((tr, LANE - ROPE_DIM), BF16)
    o_ref[:, IN_WIDTH_PACKED - LANE:] = jnp.concatenate(
        [w[:, ml_end:ml_end + gates], jnp.zeros((tr, LANE - gates), F32)], axis=1).astype(BF16)


def _pack_w_in(w_in, *, tr):
    L, K, n_in = w_in.shape
    return pl.pallas_call(
        _pack_w_in_kernel,
        name="pack_w_in",
        grid=(L, K // tr),
        in_specs=[pl.BlockSpec((None, tr, n_in), lambda l, i: (l, i, 0))],
        out_specs=pl.BlockSpec((None, tr, IN_WIDTH_PACKED), lambda l, i: (l, i, 0)),
        out_shape=jax.ShapeDtypeStruct((L, K, IN_WIDTH_PACKED), BF16),
        compiler_params=_params(("parallel", "parallel")),
    )(w_in)


def kernel(x, positions, g_mix, w_in, b_gates, w_conv, g_ml_out, g_ret_out, g_q_norm, w_q_up, g_kv_norm, w_kv_up,
           w_out, g_ffn, w_ff1, w_ff2, g_final):
    B, S, D = x.shape
    assert B == 1
    depth = w_in.shape[0]
    tm = min(512, S)
    tm_big = min(1024, S)

    half = ROPE_DIM // 2
    inv = ROPE_THETA ** (-jnp.arange(half, dtype=F32) / half)
    ang = positions[0].astype(F32)[:, None] * inv
    cos4 = jnp.tile(jnp.concatenate([jnp.cos(ang), jnp.cos(ang)], axis=-1), (1, RET_HEADS))
    sin4 = jnp.tile(jnp.concatenate([-jnp.sin(ang), jnp.sin(ang)], axis=-1), (1, RET_HEADS))

    w_in_p = _pack_w_in(w_in, tr=256)
    q_rank = w_q_up.shape[1]
    w_q4 = w_q_up.astype(BF16).reshape(depth, q_rank, MLA_HEADS, MLA_QK)
    w_q = jnp.concatenate([w_q4[..., :MLA_NOPE].reshape(depth, q_rank, MLA_HEADS * MLA_NOPE),
                           w_q4[..., MLA_NOPE:].reshape(depth, q_rank, MLA_HEADS * ROPE_DIM)], axis=-1)
    w_kv = w_kv_up.astype(BF16)
    bias_t = jnp.broadcast_to(b_gates[:, :, None], (depth, GATE_ROWS, CHUNK))
    idx = jnp.arange(CHUNK)
    tri = jnp.stack([idx[:, None] >= idx[None, :], idx[:, None] <= idx[None, :]]).astype(BF16)
    sel = _gate_select_matrix()
    ret_consts = _retention_consts()
    r3 = lambda g: g[:, None, :]

    xs = x[0]
    for l in range(depth):
        proj, gates_t = _in_proj(xs, r3(g_mix), w_in_p, l, tm=tm_big, tn=1536)
        mq, mkt, rq, rkt = _prep(proj, w_conv, l, cos4, sin4, tm=tm)
        h_f, h_b, y_f, y_b = _mixers(mq, mkt, rq, rkt, proj, gates_t, bias_t, l, tri, sel, ret_consts)
        y_ml, y_ret = _combine(h_f, h_b, y_f, y_b, proj, r3(g_ml_out), r3(g_ret_out), l, tm=tm)
        q, k, v = _mla_prep(proj, r3(g_q_norm), r3(g_kv_norm), w_q, w_kv, l, cos4, sin4, tm=tm)
        y_mla, (w_out_b, w_ff1_b, w_ff2_b) = _attention(q, k, v, (w_out, w_ff1, w_ff2), l,
                                                        tq=min(1024, S), tk=min(256, S))
        xs = _mm_res([y_ml, y_ret, y_mla], w_out_b, xs, tm=tm_big, tn=1024)
        act = _ffn1(xs, r3(g_ffn), w_ff1_b, l, tm=tm_big, tn=2048)
        xs = _ffn2(act, w_ff2_b, xs, tm=tm_big, tn=1024, tk=4096)
    return _final_rms(xs, g_final[None, :], tm=tm)[None]
```

```python
import functools
import math

import numpy as np
import jax
import jax.numpy as jnp
from jax import lax
from jax.experimental import pallas as pl
from jax.experimental.pallas import tpu as pltpu

F32 = jnp.float32
BF16 = jnp.bfloat16

EPS = 1e-6
NEG_INIT = -1e30
ROPE_THETA = 10000.0
CHUNK = 128
LANE = 128
BF16_ROWS = 16
ML_HEADS = 4
ML_DIM = 128
RET_HEADS = 4
RET_QK = 64
RET_V = 128
RET_DECAY_BASE = 5.0
MLA_HEADS = 8
MLA_NOPE = 128
ROPE_DIM = 64
MLA_V = 128
MLA_QK = MLA_NOPE + ROPE_DIM
VMEM_LIMIT = 56 * 1024 * 1024

COL_ML_Q, COL_ML_K, COL_ML_V, COL_ML_O = 0, 4, 8, 12
COL_RQK, COL_RV, COL_RG = 16, 20, 24
COL_CQ, COL_CKV, COL_KROPE, COL_GATES = 28, 32, 34, 35
IN_WIDTH_PACKED = 36 * LANE

_NT = (((1,), (1,)), ((), ()))
_TN = (((0,), (0,)), ((), ()))


def _params(sem):
    return pltpu.CompilerParams(dimension_semantics=sem, vmem_limit_bytes=VMEM_LIMIT)


def _rmsnorm(x, g):
    return x * lax.rsqrt(jnp.mean(x * x, axis=-1, keepdims=True) + EPS) * g


def _in_proj_kernel(x_ref, g_ref, w_ref, o_ref, gates_ref, hn_ref):
    j = pl.program_id(1)

    @pl.when(j == 0)
    def _():
        hn_ref[...] = _rmsnorm(x_ref[...], g_ref[...]).astype(hn_ref.dtype)

    y = lax.dot_general(hn_ref[...], w_ref[...], _NT, preferred_element_type=F32)
    o_ref[...] = y.astype(o_ref.dtype)

    @pl.when(j == pl.num_programs(1) - 1)
    def _():
        gates_ref[...] = y[:, y.shape[1] - LANE:].T


def _in_proj(x, g, w, layer, *, tm, tn):
    S, K = x.shape
    N = w.shape[1]
    return pl.pallas_call(
        _in_proj_kernel,
        name="in_proj",
        grid=(S // tm, N // tn),
        in_specs=[
            pl.BlockSpec((tm, K), lambda i, j: (i, 0)),
            pl.BlockSpec((None, 1, K), lambda i, j: (layer, 0, 0)),
            pl.BlockSpec((None, tn, K), lambda i, j: (layer, j, 0)),
        ],
        out_specs=[pl.BlockSpec((tm, tn), lambda i, j: (i, j)), pl.BlockSpec((LANE, tm), lambda i, j: (0, i))],
        out_shape=[jax.ShapeDtypeStruct((S, N), BF16), jax.ShapeDtypeStruct((LANE, S), F32)],
        scratch_shapes=[pltpu.VMEM((tm, K), BF16)],
        compiler_params=_params(("parallel", "arbitrary")),
    )(x, g, w)


def _mm_res_kernel(*refs, n_pairs):
    a_refs = refs[:n_pairs]
    w_refs = refs[n_pairs:2 * n_pairs]
    x_ref, o_ref = refs[2 * n_pairs], refs[2 * n_pairs + 1]
    acc = x_ref[...]
    for a_ref, w_ref in zip(a_refs, w_refs):
        acc = acc + jnp.dot(a_ref[...], w_ref[...], preferred_element_type=F32)
    o_ref[...] = acc


def _mm_res(acts, w, x, *, tm, tn):
    S, N = x.shape
    n_pairs = len(acts)
    a_specs, w_specs = [], []
    row = 0
    for a in acts:
        k = a.shape[1]
        assert row % k == 0
        a_specs.append(pl.BlockSpec((tm, k), lambda i, j: (i, 0)))
        w_specs.append(pl.BlockSpec((k, tn), lambda i, j, rb=row // k: (rb, j)))
        row += k
    assert row == w.shape[0]
    return pl.pallas_call(
        functools.partial(_mm_res_kernel, n_pairs=n_pairs),
        name="out_proj",
        grid=(S // tm, N // tn),
        in_specs=a_specs + w_specs + [pl.BlockSpec((tm, tn), lambda i, j: (i, j))],
        out_specs=pl.BlockSpec((tm, tn), lambda i, j: (i, j)),
        out_shape=jax.ShapeDtypeStruct((S, N), F32),
        compiler_params=_params(("parallel", "arbitrary")),
    )(*acts, *([w] * n_pairs), x)


def _ffn1_kernel(x_ref, g_ref, w_ref, o_ref, hn_ref):
    @pl.when(pl.program_id(1) == 0)
    def _():
        hn_ref[...] = _rmsnorm(x_ref[...], g_ref[...]).astype(hn_ref.dtype)

    y = jnp.dot(hn_ref[...], w_ref[...], preferred_element_type=F32)
    o_ref[...] = jnp.square(jnp.maximum(y, 0.0)).astype(o_ref.dtype)


def _ffn1(x, g, w, layer, *, tm, tn):
    S, K = x.shape
    N = w.shape[-1]
    return pl.pallas_call(
        _ffn1_kernel,
        name="ffn1",
        grid=(S // tm, N // tn),
        in_specs=[
            pl.BlockSpec((tm, K), lambda i, j: (i, 0)),
            pl.BlockSpec((None, 1, K), lambda i, j: (layer, 0, 0)),
            pl.BlockSpec((K, tn), lambda i, j: (0, j)),
        ],
        out_specs=pl.BlockSpec((tm, tn), lambda i, j: (i, j)),
        out_shape=jax.ShapeDtypeStruct((S, N), BF16),
        scratch_shapes=[pltpu.VMEM((tm, K), BF16)],
        compiler_params=_params(("parallel", "arbitrary")),
    )(x, g, w)


def _ffn2_kernel(a_ref, w_ref, x_ref, o_ref):
    @pl.when(pl.program_id(2) == 0)
    def _():
        o_ref[...] = x_ref[...]

    o_ref[...] += jnp.dot(a_ref[...], w_ref[...], preferred_element_type=F32)


def _ffn2(a, w, x, *, tm, tn, tk):
    S, N = x.shape
    K = a.shape[1]
    return pl.pallas_call(
        _ffn2_kernel,
        name="ffn2",
        grid=(S // tm, N // tn, K // tk),
        in_specs=[
            pl.BlockSpec((tm, tk), lambda i, j, k: (i, k)),
            pl.BlockSpec((tk, tn), lambda i, j, k: (k, j)),
            pl.BlockSpec((tm, tn), lambda i, j, k: (i, j)),
        ],
        out_specs=pl.BlockSpec((tm, tn), lambda i, j, k: (i, j)),
        out_shape=jax.ShapeDtypeStruct((S, N), F32),
        compiler_params=_params(("parallel", "arbitrary", "arbitrary")),
    )(a, w, x)


def _final_rms_kernel(x_ref, g_ref, o_ref):
    o_ref[...] = _rmsnorm(x_ref[...], g_ref[...])


def _final_rms(x, g, *, tm):
    S, K = x.shape
    return pl.pallas_call(
        _final_rms_kernel,
        name="final_rms",
        grid=(S // tm,),
        in_specs=[pl.BlockSpec((tm, K), lambda i: (i, 0)), pl.BlockSpec((1, K), lambda i: (0, 0))],
        out_specs=pl.BlockSpec((tm, K), lambda i: (i, 0)),
        out_shape=jax.ShapeDtypeStruct((S, K), F32),
        compiler_params=_params(("parallel",)),
    )(x, g)


def _rope_grouped(t, cosf, sinf):
    width = t.shape[1]
    lane = lax.broadcasted_iota(jnp.int32, t.shape, 1)
    first_half = (lane % ROPE_DIM) < (ROPE_DIM // 2)
    partner = jnp.where(first_half, pltpu.roll(t, width - ROPE_DIM // 2, 1), pltpu.roll(t, ROPE_DIM // 2, 1))
    return t * cosf + partner * sinf


def _prep_kernel(qk_ref, prev_ref, next_ref, wc_ref, rqk_ref, cos_ref, sin_ref, mq_out, mkt_out, rq_out, rkt_out,
                 *, tm, nblk):
    i = pl.program_id(0)
    x = qk_ref[...].astype(F32)
    row = lax.broadcasted_iota(jnp.int32, x.shape, 0)
    prev_row = jnp.where(i > 0, prev_ref[BF16_ROWS - 1:BF16_ROWS, :].astype(F32), 0.0)
    next_row = jnp.where(i < nblk - 1, next_ref[0:1, :].astype(F32), 0.0)
    x_m = jnp.where(row == 0, prev_row, pltpu.roll(x, 1, 0))
    x_p = jnp.where(row == tm - 1, next_row, pltpu.roll(x, tm - 1, 0))
    w = wc_ref[...]
    y = x_m * w[0:1, :] + x * w[1:2, :] + x_p * w[2:3, :]
    y = y / (1.0 + jnp.exp(-y))
    half = ML_HEADS * ML_DIM
    mq_out[...] = (y[:, :half] * (ML_DIM ** -0.5)).astype(mq_out.dtype)
    mkt_out[...] = y[:, half:].T.astype(mkt_out.dtype)

    cosf = cos_ref[...]
    sinf = sin_ref[...]
    width = RET_HEADS * RET_QK
    r = rqk_ref[...].astype(F32)
    rq_out[...] = _rope_grouped(r[:, :width], cosf, sinf).astype(rq_out.dtype)
    rkt_out[...] = (_rope_grouped(r[:, width:], cosf, sinf) * (RET_QK ** -0.5)).T.astype(rkt_out.dtype)


def _prep(proj, w_conv, layer, cos4, sin4, *, tm):
    S = proj.shape[0]
    nblk = S // tm
    qk_w = 2 * ML_HEADS * ML_DIM
    rqk_w = 2 * RET_HEADS * RET_QK
    hb = tm // BF16_ROWS
    return pl.pallas_call(
        functools.partial(_prep_kernel, tm=tm, nblk=nblk),
        name="prep",
        grid=(nblk,),
        in_specs=[
            pl.BlockSpec((tm, qk_w), lambda i: (i, 0)),
            pl.BlockSpec((BF16_ROWS, qk_w), lambda i: (jnp.maximum(i * hb - 1, 0), 0)),
            pl.BlockSpec((BF16_ROWS, qk_w), lambda i: (jnp.minimum((i + 1) * hb, S // BF16_ROWS - 1), 0)),
            pl.BlockSpec((None, 3, qk_w), lambda i: (layer, 0, 0)),
            pl.BlockSpec((tm, rqk_w), lambda i: (i, COL_RQK * LANE // rqk_w)),
            pl.BlockSpec((tm, rqk_w // 2), lambda i: (i, 0)),
            pl.BlockSpec((tm, rqk_w // 2), lambda i: (i, 0)),
        ],
        out_specs=[
            pl.BlockSpec((tm, qk_w // 2), lambda i: (i, 0)), pl.BlockSpec((qk_w // 2, tm), lambda i: (0, i)),
            pl.BlockSpec((tm, rqk_w // 2), lambda i: (i, 0)), pl.BlockSpec((rqk_w // 2, tm), lambda i: (0, i)),
        ],
        out_shape=[
            jax.ShapeDtypeStruct((S, qk_w // 2), BF16), jax.ShapeDtypeStruct((qk_w // 2, S), BF16),
            jax.ShapeDtypeStruct((S, rqk_w // 2), BF16), jax.ShapeDtypeStruct((rqk_w // 2, S), BF16),
        ],
        compiler_params=_params(("parallel",)),
    )(proj, proj, proj, w_conv, proj, cos4, sin4)


def _log_sigmoid(t):
    return jnp.minimum(t, 0.0) - jnp.log1p(jnp.exp(-jnp.abs(t)))


def _split3(t):
    hi = t.astype(BF16)
    r1 = t - hi.astype(F32)
    mid = r1.astype(BF16)
    lo = (r1 - mid.astype(F32)).astype(BF16)
    return hi, mid, lo


GATE_ROWS = 16


def _mlstm_gates(g_t, tri):
    logf_t = _log_sigmoid(g_t)
    cs = jnp.dot(jnp.concatenate(_split3(logf_t), axis=0), tri, preferred_element_type=F32)
    b_t = cs[0:GATE_ROWS] + cs[GATE_ROWS:2 * GATE_ROWS] + cs[2 * GATE_ROWS:]
    tot_t = jnp.sum(logf_t, axis=1, keepdims=True)
    r_t = pltpu.roll(g_t, 4, 0) - b_t
    wend_t = tot_t + r_t
    mloc_t = jnp.max(wend_t, axis=1, keepdims=True)
    e_t = jnp.exp(wend_t - mloc_t)
    return b_t, r_t, tot_t, mloc_t, e_t


def _mixers_kernel(mqf, mktf, mvf, gf, rqf, rktf, rvf, mqb, mktb, mvb, gb, rqb, rktb, rvb,
                   bias_ref, tri_ref, sel_ref, decay_ref, inner_ref, zeta_ref,
                   hf_ref, hb_ref, yf_ref, yb_ref, c_ref, m_ref, r_ref, *, chunk_decay, cps):
    @pl.when(pl.program_id(0) == 0)
    def _():
        c_ref[...] = jnp.zeros(c_ref.shape, F32)
        m_ref[...] = jnp.full(m_ref.shape, NEG_INIT, F32)
        r_ref[...] = jnp.zeros(r_ref.shape, F32)

    L = CHUNK
    row = lax.broadcasted_iota(jnp.int32, (L, L), 0)
    col = lax.broadcasted_iota(jnp.int32, (L, L), 1)
    ones = jnp.ones((L, ML_DIM), BF16)
    bias = bias_ref[...]

    all_spans = [[slice(sub * L, (sub + 1) * L) for sub in (t, cps - 1 - t)] for t in range(cps)]

    items = [(d, h) for d in range(2) for h in range(ML_HEADS)]
    m_refs = ((mqf, mktf, mvf, hf_ref), (mqb, mktb, mvb, hb_ref))
    r_refs = ((rqf, rktf, rvf, yf_ref), (rqb, rktb, rvb, yb_ref))

    def m_operands(spans, d, h):
        q_ref, kt_ref, v_ref, _ = m_refs[d]
        sl = slice(h * ML_DIM, (h + 1) * ML_DIM)
        return q_ref[spans[d], sl], kt_ref[sl, spans[d]], v_ref[spans[d], sl]

    def r_operands(spans, d, h):
        q_ref, kt_ref, v_ref, _ = r_refs[d]
        return (q_ref[spans[d], h * RET_QK:(h + 1) * RET_QK], kt_ref[h * RET_QK:(h + 1) * RET_QK, spans[d]],
                v_ref[spans[d], h * RET_V:(h + 1) * RET_V])

    m_raw = [[jnp.dot(*m_operands(spans, d, h)[:2], preferred_element_type=F32) for d, h in items]
             for spans in all_spans]
    r_raw = [[jnp.dot(*r_operands(spans, d, h)[:2], preferred_element_type=F32) for d, h in items]
             for spans in all_spans]

    all_gates, all_bcast = [], []
    for spans in all_spans:
        gates = [_mlstm_gates(g_ref[:, spans[d]] + bias, tri_ref[1 - d]) for d, g_ref in enumerate((gf, gb))]
        x_rows = jnp.concatenate([part for g in gates for part in _split3(g[0])], axis=0)
        all_gates.append(gates)
        all_bcast.append(lax.dot_general(x_rows, sel_ref[...], _TN, preferred_element_type=F32))

    m_cur = [m_ref[st:st + 1, :] for st in range(2 * ML_HEADS)]
    m_at, keep_at, gain_at = [], [], []
    for gates in all_gates:
        m_at.append(list(m_cur))
        keeps, gains = [], []
        for d, h in items:
            cf, st = 8 * d + 4 + h, ML_HEADS * d + h
            tot_t, mloc_t = gates[d][2], gates[d][3]
            blast = jnp.broadcast_to(tot_t[cf:cf + 1, :], (1, LANE))
            m_loc = jnp.broadcast_to(mloc_t[cf:cf + 1, :], (1, LANE))
            m_new = jnp.maximum(blast + m_cur[st], m_loc)
            keeps.append(jnp.exp(blast + m_cur[st] - m_new))
            gains.append(jnp.exp(m_loc - m_new))
            m_cur[st] = m_new
        keep_at.append(keeps)
        gain_at.append(gains)
    for st in range(2 * ML_HEADS):
        m_ref[st:st + 1, :] = m_cur[st]

    m_s, m_kw, m_winter, m_floor, r_s, r_kw = [], [], [], [], [], []
    for t, spans in enumerate(all_spans):
        ms_t, mkw_t, mwi_t, mfl_t, rs_t, rkw_t = [], [], [], [], [], []
        for j, (d, h) in enumerate(items):
            cf, st = 8 * d + 4 + h, ML_HEADS * d + h
            _, r_t, _, _, e_t = all_gates[t][d]
            mask = (row >= col) if d == 0 else (row <= col)
            bc = all_bcast[t][:, st * LANE:(st + 1) * LANE]
            dlog = jnp.where(mask, bc + r_t[cf:cf + 1, :], -jnp.inf)
            inter_log = bc + m_at[t][st]
            m_t = jnp.maximum(jnp.max(dlog, axis=1, keepdims=True), inter_log)
            ms_t.append((m_raw[t][j] * jnp.exp(dlog - m_t)).astype(BF16))
            mwi_t.append(jnp.exp(inter_log - m_t))
            mfl_t.append(jnp.exp(-m_t))
            kt = m_operands(spans, d, h)[1]
            mkw_t.append((kt.astype(F32) * e_t[cf:cf + 1, :]).astype(BF16))
            rs_t.append((r_raw[t][j] * decay_ref[st]).astype(BF16))
            rkt = r_operands(spans, d, h)[1]
            rkw_t.append((rkt.astype(F32) * zeta_ref[st:st + 1, :]).astype(BF16))
        m_s.append(ms_t), m_kw.append(mkw_t), m_winter.append(mwi_t), m_floor.append(mfl_t)
        r_s.append(rs_t), r_kw.append(rkw_t)

    c_cur = [c_ref[st] for st in range(2 * ML_HEADS)]
    r_cur = [r_ref[st] for st in range(2 * RET_HEADS)]
    for t, spans in enumerate(all_spans):
        v1s = [jnp.concatenate([m_operands(spans, d, h)[2], ones], axis=1) for d, h in items]
        intra = [jnp.dot(m_s[t][j], v1s[j], preferred_element_type=F32) for j in range(len(items))]
        inter = [jnp.dot(m_operands(spans, d, h)[0], c_cur[ML_HEADS * d + h].astype(BF16),
                         preferred_element_type=F32) for d, h in items]
        c_loc = [jnp.dot(m_kw[t][j], v1s[j], preferred_element_type=F32) for j in range(len(items))]
        ry_intra = [jnp.dot(r_s[t][j], r_operands(spans, d, h)[2], preferred_element_type=F32)
                    for j, (d, h) in enumerate(items)]
        ry_inter = [jnp.dot(r_operands(spans, d, h)[0], r_cur[RET_HEADS * d + h].astype(BF16),
                            preferred_element_type=F32) for d, h in items]
        r_loc = [jnp.dot(r_kw[t][j], r_operands(spans, d, h)[2], preferred_element_type=F32)
                 for j, (d, h) in enumerate(items)]
        for j, (d, h) in enumerate(items):
            st = ML_HEADS * d + h
            w_inter = m_winter[t][j]
            num = intra[j][:, :ML_DIM] + w_inter * inter[j][:, :ML_DIM]
            den = intra[j][:, ML_DIM:] + w_inter * inter[j][:, ML_DIM:]
            m_refs[d][3][spans[d], h * ML_DIM:(h + 1) * ML_DIM] = num / jnp.maximum(jnp.abs(den), m_floor[t][j])
            keep, gain = keep_at[t][j], gain_at[t][j]
            c_cur[st] = (jnp.concatenate([keep, keep], axis=1) * c_cur[st]
                         + jnp.concatenate([gain, gain], axis=1) * c_loc[j])
            r_refs[d][3][spans[d], h * RET_V:(h + 1) * RET_V] = ry_intra[j] + ry_inter[j] * inner_ref[st]
            r_cur[st] = chunk_decay[st] * r_cur[st] + r_loc[j]
    for st in range(2 * ML_HEADS):
        c_ref[st] = c_cur[st]
        r_ref[st] = r_cur[st]


def _gate_select_matrix():
    sel = np.zeros((2 * 3 * GATE_ROWS, 2 * ML_HEADS * LANE), np.float32)
    for d in range(2):
        for part in range(3):
            for h in range(ML_HEADS):
                st = ML_HEADS * d + h
                sel[(3 * d + part) * GATE_ROWS + 8 * d + 4 + h, st * LANE:(st + 1) * LANE] = 1.0
    return jnp.asarray(sel, BF16)


def _mixers(mq, mkt, rq, rkt, proj, gates_t, bias_t, layer, tri, sel, ret_consts, *, cps):
    S = proj.shape[0]
    rows = cps * CHUNK
    nb = S // rows
    MW = ML_HEADS * ML_DIM
    RQ = RET_HEADS * RET_QK
    RV = RET_HEADS * RET_V
    decay, inner, zeta, chunk_decay = ret_consts
    fwd = lambda n: n
    bwd = lambda n: nb - 1 - n

    def specs(idx):
        return [
            pl.BlockSpec((rows, MW), lambda n: (idx(n), 0)),
            pl.BlockSpec((MW, rows), lambda n: (0, idx(n))),
            pl.BlockSpec((rows, MW), lambda n: (idx(n), COL_ML_V * LANE // MW)),
            pl.BlockSpec((GATE_ROWS, rows), lambda n: (0, idx(n))),
            pl.BlockSpec((rows, RQ), lambda n: (idx(n), 0)),
            pl.BlockSpec((RQ, rows), lambda n: (0, idx(n))),
            pl.BlockSpec((rows, RV), lambda n: (idx(n), COL_RV * LANE // RV)),
        ]

    whole = lambda a: pl.BlockSpec(a.shape, lambda n: (0,) * a.ndim)
    out = lambda idx, w: pl.BlockSpec((rows, w), lambda n: (idx(n), 0))
    operands = (mq, mkt, proj, gates_t, rq, rkt, proj)
    return pl.pallas_call(
        functools.partial(_mixers_kernel, chunk_decay=chunk_decay, cps=cps),
        name="mixers",
        grid=(nb,),
        in_specs=specs(fwd) + specs(bwd) + [
            pl.BlockSpec((None, GATE_ROWS, CHUNK), lambda n: (layer, 0, 0)),
            whole(tri), whole(sel), whole(decay), whole(inner), whole(zeta),
        ],
        out_specs=[out(fwd, MW), out(bwd, MW), out(fwd, RV), out(bwd, RV)],
        out_shape=[jax.ShapeDtypeStruct((S, MW), F32), jax.ShapeDtypeStruct((S, MW), F32),
                   jax.ShapeDtypeStruct((S, RV), F32), jax.ShapeDtypeStruct((S, RV), F32)],
        scratch_shapes=[pltpu.VMEM((2 * ML_HEADS, ML_DIM, 2 * ML_DIM), F32), pltpu.VMEM((2 * ML_HEADS, LANE), F32),
                        pltpu.VMEM((2 * RET_HEADS, RET_QK, RET_V), F32)],
        compiler_params=_params(("arbitrary",)),
    )(*operands, *operands, bias_t, tri, sel, decay, inner, zeta)


def _retention_consts():
    log_gamma = jnp.log1p(-jnp.exp2(-RET_DECAY_BASE - jnp.arange(RET_HEADS, dtype=F32)))
    idx = jnp.arange(CHUNK, dtype=F32)
    diff = idx[:, None] - idx[None, :]
    decay, inner, zeta = [], [], []
    for d in range(2):
        lg_d = log_gamma if d == 0 else log_gamma[::-1]
        dd = diff if d == 0 else -diff
        pos = idx if d == 0 else (CHUNK - 1.0 - idx)
        keep = dd >= 0
        for h in range(RET_HEADS):
            lg = lg_d[h]
            decay.append(jnp.where(keep, jnp.exp(jnp.where(keep, dd, 0.0) * lg), 0.0))
            inner.append(jnp.broadcast_to(jnp.exp((pos + 1.0) * lg)[:, None], (CHUNK, LANE)))
            zeta.append(jnp.exp((CHUNK - 1.0 - pos) * lg))
    lg_np = np.log1p(-np.exp2(-RET_DECAY_BASE - np.arange(RET_HEADS, dtype=np.float32))).astype(np.float32)
    chunk_decay = tuple(float(np.exp(np.float32(CHUNK) * lg)) for lg in list(lg_np) + list(lg_np[::-1]))
    return jnp.stack(decay), jnp.stack(inner), jnp.stack(zeta), chunk_decay


def _combine_kernel(hf, hb, o_ref, gml_ref, yf, yb, rg_ref, gret_ref, yml_out, yret_out):
    for h in range(ML_HEADS):
        sl = slice(h * ML_DIM, (h + 1) * ML_DIM)
        y = _rmsnorm(hf[:, sl] + hb[:, sl], gml_ref[:, sl])
        yml_out[:, sl] = (y / (1.0 + jnp.exp(-o_ref[:, sl].astype(F32)))).astype(yml_out.dtype)
    for h in range(RET_HEADS):
        sl = slice(h * RET_V, (h + 1) * RET_V)
        y = _rmsnorm(yf[:, sl] + yb[:, sl], gret_ref[:, sl])
        gate = rg_ref[:, sl].astype(F32)
        yret_out[:, sl] = (gate / (1.0 + jnp.exp(-gate)) * y).astype(yret_out.dtype)


def _combine(h_f, h_b, y_f, y_b, proj, g_ml, g_ret, layer, *, tm):
    S = proj.shape[0]
    W = ML_HEADS * ML_DIM
    blk = lambda c: pl.BlockSpec((tm, W), lambda i: (i, c))
    gain = pl.BlockSpec((None, 1, W), lambda i: (layer, 0, 0))
    return pl.pallas_call(
        _combine_kernel,
        name="combine",
        grid=(S // tm,),
        in_specs=[blk(0), blk(0), blk(COL_ML_O * LANE // W), gain, blk(0), blk(0), blk(COL_RG * LANE // W), gain],
        out_specs=[blk(0), blk(0)],
        out_shape=[jax.ShapeDtypeStruct((S, W), BF16), jax.ShapeDtypeStruct((S, W), BF16)],
        compiler_params=_params(("parallel",)),
    )(h_f, h_b, proj, g_ml, y_f, y_b, proj, g_ret)


def _mla_prep_kernel(cq_ref, ckv_ref, kr_ref, gq_ref, gkv_ref, wq_ref, wkv_ref, cos_ref, sin_ref,
                     q_out, k_out, v_out):
    tm = cq_ref.shape[0]
    cos4 = cos_ref[...]
    sin4 = sin_ref[...]
    qn = _rmsnorm(cq_ref[...].astype(F32), gq_ref[...]).astype(BF16)
    kvn = _rmsnorm(ckv_ref[...].astype(F32), gkv_ref[...]).astype(BF16)
    q_all = jnp.dot(qn, wq_ref[...], preferred_element_type=F32)
    kv_all = jnp.dot(kvn, wkv_ref[...], preferred_element_type=F32)
    q_scale = (MLA_QK ** -0.5) * math.log2(math.e)
    nope_w = MLA_HEADS * MLA_NOPE
    q_rope = _rope_grouped(q_all[:, nope_w:], jnp.concatenate([cos4, cos4], axis=1),
                           jnp.concatenate([sin4, sin4], axis=1)) * q_scale
    k_rope = _rope_grouped(kr_ref[...].astype(F32), cos4[:, :LANE], sin4[:, :LANE])[:, :ROPE_DIM].astype(BF16)
    ones = jnp.ones((tm, MLA_V), BF16)
    for h in range(MLA_HEADS):
        q_out[h, :, :MLA_NOPE] = (q_all[:, h * MLA_NOPE:(h + 1) * MLA_NOPE] * q_scale).astype(BF16)
        q_out[h, :, MLA_NOPE:] = q_rope[:, h * ROPE_DIM:(h + 1) * ROPE_DIM].astype(BF16)
        kv0 = h * (MLA_NOPE + MLA_V)
        k_out[h, :, :MLA_NOPE] = kv_all[:, kv0:kv0 + MLA_NOPE].astype(BF16)
        k_out[h, :, MLA_NOPE:] = k_rope
        v_out[h, :, :MLA_V] = kv_all[:, kv0 + MLA_NOPE:kv0 + MLA_NOPE + MLA_V].astype(BF16)
        v_out[h, :, MLA_V:] = ones


def _mla_prep(proj, g_q, g_kv, w_q, w_kv, layer, cos4, sin4, *, tm):
    S = proj.shape[0]
    q_rank, kv_rank = w_q.shape[1], w_kv.shape[1]
    return pl.pallas_call(
        _mla_prep_kernel,
        name="mla_prep",
        grid=(S // tm,),
        in_specs=[
            pl.BlockSpec((tm, q_rank), lambda i: (i, COL_CQ * LANE // q_rank)),
            pl.BlockSpec((tm, kv_rank), lambda i: (i, COL_CKV * LANE // kv_rank)),
            pl.BlockSpec((tm, LANE), lambda i: (i, COL_KROPE)),
            pl.BlockSpec((None, 1, q_rank), lambda i: (layer, 0, 0)),
            pl.BlockSpec((None, 1, kv_rank), lambda i: (layer, 0, 0)),
            pl.BlockSpec((None, q_rank, w_q.shape[2]), lambda i: (layer, 0, 0)),
            pl.BlockSpec((None, kv_rank, w_kv.shape[2]), lambda i: (layer, 0, 0)),
            pl.BlockSpec((tm, 2 * LANE), lambda i: (i, 0)),
            pl.BlockSpec((tm, 2 * LANE), lambda i: (i, 0)),
        ],
        out_specs=[
            pl.BlockSpec((MLA_HEADS, tm, MLA_QK), lambda i: (0, i, 0)),
            pl.BlockSpec((MLA_HEADS, tm, MLA_QK), lambda i: (0, i, 0)),
            pl.BlockSpec((MLA_HEADS, tm, 2 * MLA_V), lambda i: (0, i, 0)),
        ],
        out_shape=[
            jax.ShapeDtypeStruct((MLA_HEADS, S, MLA_QK), BF16),
            jax.ShapeDtypeStruct((MLA_HEADS, S, MLA_QK), BF16),
            jax.ShapeDtypeStruct((MLA_HEADS, S, 2 * MLA_V), BF16),
        ],
        compiler_params=_params(("parallel",)),
    )(proj, proj, proj, g_q, g_kv, w_q, w_kv, cos4, sin4)


def _attn_kernel(q_ref, k_ref, v_ref, *rest, tk, nk, n_cast):
    w_refs, o_ref, wb_refs = rest[:n_cast], rest[n_cast], rest[n_cast + 1:]
    for w_ref, wb_ref in zip(w_refs, wb_refs):
        wb_ref[...] = w_ref[...].astype(wb_ref.dtype)

    q = q_ref[...]
    m = acc = None
    for c in range(nk):
        k = k_ref[c * tk:(c + 1) * tk, :]
        v = v_ref[c * tk:(c + 1) * tk, :]
        s = lax.dot_general(q, k, _NT, preferred_element_type=F32)
        m_c = jnp.max(s, axis=-1, keepdims=True)
        if c == 0:
            m = m_c
            acc = jnp.dot(jnp.exp2(s - m).astype(BF16), v, preferred_element_type=F32)
        else:
            m_new = jnp.maximum(m, m_c)
            acc = jnp.exp2(m - m_new) * acc + jnp.dot(jnp.exp2(s - m_new).astype(BF16), v, preferred_element_type=F32)
            m = m_new
    o_ref[...] = (acc[:, :MLA_V] / acc[:, MLA_V:]).astype(o_ref.dtype)


def _attention(q, k, v, cast_weights, layer, *, tq, tk):
    S = q.shape[1]
    nq = S // tq
    n_steps = MLA_HEADS * nq
    w_specs, wb_specs, wb_shapes = [], [], []
    for w in cast_weights:
        _, rows, cols = w.shape
        rb = rows // n_steps
        assert rb * n_steps == rows and rb % BF16_ROWS == 0
        w_specs.append(pl.BlockSpec((None, rb, cols), lambda h, i: (layer, h * nq + i, 0)))
        wb_specs.append(pl.BlockSpec((rb, cols), lambda h, i: (h * nq + i, 0)))
        wb_shapes.append(jax.ShapeDtypeStruct((rows, cols), BF16))
    outs = pl.pallas_call(
        functools.partial(_attn_kernel, tk=tk, nk=S // tk, n_cast=len(cast_weights)),
        name="attention",
        grid=(MLA_HEADS, nq),
        in_specs=[
            pl.BlockSpec((None, tq, MLA_QK), lambda h, i: (h, i, 0)),
            pl.BlockSpec((None, S, MLA_QK), lambda h, i: (h, 0, 0)),
            pl.BlockSpec((None, S, 2 * MLA_V), lambda h, i: (h, 0, 0)),
        ] + w_specs,
        out_specs=[pl.BlockSpec((tq, MLA_V), lambda h, i: (i, h))] + wb_specs,
        out_shape=[jax.ShapeDtypeStruct((S, MLA_HEADS * MLA_V), BF16)] + wb_shapes,
        compiler_params=_params(("parallel", "arbitrary")),
    )(q, k, v, *cast_weights)
    return outs[0], outs[1:]


def _pack_w_in_kernel(w_ref, o_ref):
    n_in, tc = w_ref.shape
    ml_end = 4 * ML_HEADS * ML_DIM
    gates = 4 * ML_HEADS
    rest = n_in - ml_end - gates
    gate_row = IN_WIDTH_PACKED - LANE
    assert ml_end + rest + (LANE - ROPE_DIM) == gate_row
    o_ref[:ml_end, :] = w_ref[:ml_end, :].astype(BF16)
    o_ref[ml_end:ml_end + rest, :] = w_ref[ml_end + gates:, :].astype(BF16)
    o_ref[ml_end + rest:gate_row, :] = jnp.zeros((LANE - ROPE_DIM, tc), BF16)
    o_ref[gate_row:gate_row + gates, :] = w_ref[ml_end:ml_end + gates, :].astype(BF16)
    o_ref[gate_row + gates:, :] = jnp.zeros((LANE - gates, tc), BF16)


def _pack_w_in(w_in_t, *, tc):
    L, n_in, K = w_in_t.shape
    return pl.pallas_call(
        _pack_w_in_kernel,
        name="pack_w_in",
        grid=(L, K // tc),
        in_specs=[pl.BlockSpec((None, n_in, tc), lambda l, i: (l, 0, i))],
        out_specs=pl.BlockSpec((None, IN_WIDTH_PACKED, tc), lambda l, i: (l, 0, i)),
        out_shape=jax.ShapeDtypeStruct((L, IN_WIDTH_PACKED, K), BF16),
        compiler_params=_params(("parallel", "parallel")),
    )(w_in_t)


def kernel(x, positions, g_mix, w_in, b_gates, w_conv, g_ml_out, g_ret_out, g_q_norm, w_q_up, g_kv_norm, w_kv_up,
           w_out, g_ffn, w_ff1, w_ff2, g_final):
    B, S, D = x.shape
    assert B == 1
    depth = w_in.shape[0]
    tm = min(512, S)
    tm_big = min(1024, S)

    half = ROPE_DIM // 2
    inv = ROPE_THETA ** (-jnp.arange(half, dtype=F32) / half)
    ang = positions[0].astype(F32)[:, None] * inv
    cos4 = jnp.tile(jnp.concatenate([jnp.cos(ang), jnp.cos(ang)], axis=-1), (1, RET_HEADS))
    sin4 = jnp.tile(jnp.concatenate([-jnp.sin(ang), jnp.sin(ang)], axis=-1), (1, RET_HEADS))

    w_in_p = _pack_w_in(jnp.swapaxes(w_in, 1, 2), tc=256)
    q_rank = w_q_up.shape[1]
    w_q4 = w_q_up.astype(BF16).reshape(depth, q_rank, MLA_HEADS, MLA_QK)
    w_q = jnp.concatenate([w_q4[..., :MLA_NOPE].reshape(depth, q_rank, MLA_HEADS * MLA_NOPE),
                           w_q4[..., MLA_NOPE:].reshape(depth, q_rank, MLA_HEADS * ROPE_DIM)], axis=-1)
    w_kv = w_kv_up.astype(BF16)
    bias_t = jnp.broadcast_to(b_gates[:, :, None], (depth, GATE_ROWS, CHUNK))
    idx = jnp.arange(CHUNK)
    tri = jnp.stack([idx[:, None] >= idx[None, :], idx[:, None] <= idx[None, :]]).astype(BF16)
    sel = _gate_select_matrix()
    ret_consts = _retention_consts()
    r3 = lambda g: g[:, None, :]

    xs = x[0]
    for l in range(depth):
        proj, gates_t = _in_proj(xs, r3(g_mix), w_in_p, l, tm=tm_big, tn=1536)
        mq, mkt, rq, rkt = _prep(proj, w_conv, l, cos4, sin4, tm=tm)
        h_f, h_b, y_f, y_b = _mixers(mq, mkt, rq, rkt, proj, gates_t, bias_t, l, tri, sel, ret_consts, cps=2)
        y_ml, y_ret = _combine(h_f, h_b, y_f, y_b, proj, r3(g_ml_out), r3(g_ret_out), l, tm=tm)
        q, k, v = _mla_prep(proj, r3(g_q_norm), r3(g_kv_norm), w_q, w_kv, l, cos4, sin4, tm=tm)
        y_mla, (w_out_b, w_ff1_b, w_ff2_b) = _attention(q, k, v, (w_out, w_ff1, w_ff2), l,
                                                        tq=min(1024, S), tk=min(256, S))
        xs = _mm_res([y_ml, y_ret, y_mla], w_out_b, xs, tm=tm_big, tn=1024)
        act = _ffn1(xs, r3(g_ffn), w_ff1_b, l, tm=tm_big, tn=2048)
        xs = _ffn2(act, w_ff2_b, xs, tm=tm_big, tn=1024, tk=4096)
    return _final_rms(xs, g_final[None, :], tm=tm)[None]
```

```python
import functools
import math

import numpy as np
import jax
import jax.numpy as jnp
from jax import lax
from jax.experimental import pallas as pl
from jax.experimental.pallas import tpu as pltpu

F32 = jnp.float32
BF16 = jnp.bfloat16

EPS = 1e-6
NEG_INIT = -1e30
ROPE_THETA = 10000.0
CHUNK = 128
LANE = 128
BF16_ROWS = 16
ML_HEADS = 4
ML_DIM = 128
RET_HEADS = 4
RET_QK = 64
RET_V = 128
RET_DECAY_BASE = 5.0
MLA_HEADS = 8
MLA_NOPE = 128
ROPE_DIM = 64
MLA_V = 128
MLA_QK = MLA_NOPE + ROPE_DIM
VMEM_LIMIT = 56 * 1024 * 1024

COL_ML_Q, COL_ML_K, COL_ML_V, COL_ML_O = 0, 4, 8, 12
COL_RQK, COL_RV, COL_RG = 16, 20, 24
COL_CQ, COL_CKV, COL_KROPE, COL_GATES = 28, 32, 34, 35
IN_WIDTH_PACKED = 36 * LANE

_NT = (((1,), (1,)), ((), ()))
_TN = (((0,), (0,)), ((), ()))


def _params(sem):
    return pltpu.CompilerParams(dimension_semantics=sem, vmem_limit_bytes=VMEM_LIMIT)


def _rmsnorm(x, g):
    return x * lax.rsqrt(jnp.mean(x * x, axis=-1, keepdims=True) + EPS) * g


def _in_proj_kernel(x_ref, g_ref, w_ref, o_ref, gates_ref, hn_ref):
    j = pl.program_id(1)

    @pl.when(j == 0)
    def _():
        hn_ref[...] = _rmsnorm(x_ref[...], g_ref[...]).astype(hn_ref.dtype)

    y = lax.dot_general(hn_ref[...], w_ref[...], _NT, preferred_element_type=F32)
    o_ref[...] = y.astype(o_ref.dtype)

    @pl.when(j == pl.num_programs(1) - 1)
    def _():
        gates_ref[...] = y[:, y.shape[1] - LANE:].T


def _in_proj(x, g, w, layer, *, tm, tn):
    S, K = x.shape
    N = w.shape[1]
    return pl.pallas_call(
        _in_proj_kernel,
        name="in_proj",
        grid=(S // tm, N // tn),
        in_specs=[
            pl.BlockSpec((tm, K), lambda i, j: (i, 0)),
            pl.BlockSpec((None, 1, K), lambda i, j: (layer, 0, 0)),
            pl.BlockSpec((None, tn, K), lambda i, j: (layer, j, 0)),
        ],
        out_specs=[pl.BlockSpec((tm, tn), lambda i, j: (i, j)), pl.BlockSpec((LANE, tm), lambda i, j: (0, i))],
        out_shape=[jax.ShapeDtypeStruct((S, N), BF16), jax.ShapeDtypeStruct((LANE, S), F32)],
        scratch_shapes=[pltpu.VMEM((tm, K), BF16)],
        compiler_params=_params(("parallel", "arbitrary")),
    )(x, g, w)


def _out_proj_kernel(hf, hb, o_ref, gml_ref, yf, yb, rg_ref, gret_ref, ymla_ref, w_ref, x_ref, out_ref):
    w_ml = ML_HEADS * ML_DIM
    w_ret = RET_HEADS * RET_V
    acc = x_ref[...] + jnp.dot(ymla_ref[...], w_ref[w_ml + w_ret:, :], preferred_element_type=F32)
    y_ml, y_ret = [], []
    for h in range(ML_HEADS):
        sl = slice(h * ML_DIM, (h + 1) * ML_DIM)
        y = _rmsnorm(hf[:, sl] + hb[:, sl], gml_ref[:, sl])
        y_ml.append((y / (1.0 + jnp.exp(-o_ref[:, sl].astype(F32)))).astype(BF16))
    for h in range(RET_HEADS):
        sl = slice(h * RET_V, (h + 1) * RET_V)
        y = _rmsnorm(yf[:, sl] + yb[:, sl], gret_ref[:, sl])
        gate = rg_ref[:, sl].astype(F32)
        y_ret.append((gate / (1.0 + jnp.exp(-gate)) * y).astype(BF16))
    acc = acc + jnp.dot(jnp.concatenate(y_ml, axis=1), w_ref[:w_ml, :], preferred_element_type=F32)
    acc = acc + jnp.dot(jnp.concatenate(y_ret, axis=1), w_ref[w_ml:w_ml + w_ret, :], preferred_element_type=F32)
    out_ref[...] = acc


def _out_proj(h_f, h_b, y_f, y_b, proj, y_mla, g_ml, g_ret, layer, w, x, *, tm):
    S, N = x.shape
    W = ML_HEADS * ML_DIM
    blk = lambda c: pl.BlockSpec((tm, W), lambda i: (i, c))
    gain = pl.BlockSpec((None, 1, W), lambda i: (layer, 0, 0))
    return pl.pallas_call(
        _out_proj_kernel,
        name="out_proj",
        grid=(S // tm,),
        in_specs=[blk(0), blk(0), blk(COL_ML_O * LANE // W), gain, blk(0), blk(0), blk(COL_RG * LANE // W), gain,
                  pl.BlockSpec((tm, y_mla.shape[1]), lambda i: (i, 0)),
                  pl.BlockSpec(w.shape, lambda i: (0, 0)),
                  pl.BlockSpec((tm, N), lambda i: (i, 0))],
        out_specs=pl.BlockSpec((tm, N), lambda i: (i, 0)),
        out_shape=jax.ShapeDtypeStruct((S, N), F32),
        compiler_params=_params(("parallel",)),
    )(h_f, h_b, proj, g_ml, y_f, y_b, proj, g_ret, y_mla, w, x)


def _ffn1_kernel(x_ref, g_ref, w_ref, o_ref, hn_ref):
    @pl.when(pl.program_id(1) == 0)
    def _():
        hn_ref[...] = _rmsnorm(x_ref[...], g_ref[...]).astype(hn_ref.dtype)

    y = jnp.dot(hn_ref[...], w_ref[...], preferred_element_type=F32)
    o_ref[...] = jnp.square(jnp.maximum(y, 0.0)).astype(o_ref.dtype)


def _ffn1(x, g, w, layer, *, tm, tn):
    S, K = x.shape
    N = w.shape[-1]
    return pl.pallas_call(
        _ffn1_kernel,
        name="ffn1",
        grid=(S // tm, N // tn),
        in_specs=[
            pl.BlockSpec((tm, K), lambda i, j: (i, 0)),
            pl.BlockSpec((None, 1, K), lambda i, j: (layer, 0, 0)),
            pl.BlockSpec((K, tn), lambda i, j: (0, j)),
        ],
        out_specs=pl.BlockSpec((tm, tn), lambda i, j: (i, j)),
        out_shape=jax.ShapeDtypeStruct((S, N), BF16),
        scratch_shapes=[pltpu.VMEM((tm, K), BF16)],
        compiler_params=_params(("parallel", "arbitrary")),
    )(x, g, w)


def _ffn2_kernel(a_ref, w_ref, x_ref, o_ref):
    @pl.when(pl.program_id(2) == 0)
    def _():
        o_ref[...] = x_ref[...]

    o_ref[...] += jnp.dot(a_ref[...], w_ref[...], preferred_element_type=F32)


def _ffn2(a, w, x, *, tm, tn, tk):
    S, N = x.shape
    K = a.shape[1]
    return pl.pallas_call(
        _ffn2_kernel,
        name="ffn2",
        grid=(S // tm, N // tn, K // tk),
        in_specs=[
            pl.BlockSpec((tm, tk), lambda i, j, k: (i, k)),
            pl.BlockSpec((tk, tn), lambda i, j, k: (k, j)),
            pl.BlockSpec((tm, tn), lambda i, j, k: (i, j)),
        ],
        out_specs=pl.BlockSpec((tm, tn), lambda i, j, k: (i, j)),
        out_shape=jax.ShapeDtypeStruct((S, N), F32),
        compiler_params=_params(("parallel", "arbitrary", "arbitrary")),
    )(a, w, x)


def _final_rms_kernel(x_ref, g_ref, o_ref):
    o_ref[...] = _rmsnorm(x_ref[...], g_ref[...])


def _final_rms(x, g, *, tm):
    S, K = x.shape
    return pl.pallas_call(
        _final_rms_kernel,
        name="final_rms",
        grid=(S // tm,),
        in_specs=[pl.BlockSpec((tm, K), lambda i: (i, 0)), pl.BlockSpec((1, K), lambda i: (0, 0))],
        out_specs=pl.BlockSpec((tm, K), lambda i: (i, 0)),
        out_shape=jax.ShapeDtypeStruct((S, K), F32),
        compiler_params=_params(("parallel",)),
    )(x, g)


def _rope_grouped(t, cosf, sinf):
    width = t.shape[1]
    lane = lax.broadcasted_iota(jnp.int32, t.shape, 1)
    first_half = (lane % ROPE_DIM) < (ROPE_DIM // 2)
    partner = jnp.where(first_half, pltpu.roll(t, width - ROPE_DIM // 2, 1), pltpu.roll(t, ROPE_DIM // 2, 1))
    return t * cosf + partner * sinf


def _prep_kernel(qk_ref, prev_ref, next_ref, wc_ref, rqk_ref, cos_ref, sin_ref, mq_out, mkt_out, rq_out, rkt_out,
                 *, tm, nblk):
    i = pl.program_id(0)
    x = qk_ref[...].astype(F32)
    row = lax.broadcasted_iota(jnp.int32, x.shape, 0)
    prev_row = jnp.where(i > 0, prev_ref[BF16_ROWS - 1:BF16_ROWS, :].astype(F32), 0.0)
    next_row = jnp.where(i < nblk - 1, next_ref[0:1, :].astype(F32), 0.0)
    x_m = jnp.where(row == 0, prev_row, pltpu.roll(x, 1, 0))
    x_p = jnp.where(row == tm - 1, next_row, pltpu.roll(x, tm - 1, 0))
    w = wc_ref[...]
    y = x_m * w[0:1, :] + x * w[1:2, :] + x_p * w[2:3, :]
    y = y / (1.0 + jnp.exp(-y))
    half = ML_HEADS * ML_DIM
    mq_out[...] = (y[:, :half] * (ML_DIM ** -0.5)).astype(mq_out.dtype)
    mkt_out[...] = y[:, half:].T.astype(mkt_out.dtype)

    cosf = cos_ref[...]
    sinf = sin_ref[...]
    width = RET_HEADS * RET_QK
    r = rqk_ref[...].astype(F32)
    rq_out[...] = _rope_grouped(r[:, :width], cosf, sinf).astype(rq_out.dtype)
    rkt_out[...] = (_rope_grouped(r[:, width:], cosf, sinf) * (RET_QK ** -0.5)).T.astype(rkt_out.dtype)


def _prep(proj, w_conv, layer, cos4, sin4, *, tm):
    S = proj.shape[0]
    nblk = S // tm
    qk_w = 2 * ML_HEADS * ML_DIM
    rqk_w = 2 * RET_HEADS * RET_QK
    hb = tm // BF16_ROWS
    return pl.pallas_call(
        functools.partial(_prep_kernel, tm=tm, nblk=nblk),
        name="prep",
        grid=(nblk,),
        in_specs=[
            pl.BlockSpec((tm, qk_w), lambda i: (i, 0)),
            pl.BlockSpec((BF16_ROWS, qk_w), lambda i: (jnp.maximum(i * hb - 1, 0), 0)),
            pl.BlockSpec((BF16_ROWS, qk_w), lambda i: (jnp.minimum((i + 1) * hb, S // BF16_ROWS - 1), 0)),
            pl.BlockSpec((None, 3, qk_w), lambda i: (layer, 0, 0)),
            pl.BlockSpec((tm, rqk_w), lambda i: (i, COL_RQK * LANE // rqk_w)),
            pl.BlockSpec((tm, rqk_w // 2), lambda i: (i, 0)),
            pl.BlockSpec((tm, rqk_w // 2), lambda i: (i, 0)),
        ],
        out_specs=[
            pl.BlockSpec((tm, qk_w // 2), lambda i: (i, 0)), pl.BlockSpec((qk_w // 2, tm), lambda i: (0, i)),
            pl.BlockSpec((tm, rqk_w // 2), lambda i: (i, 0)), pl.BlockSpec((rqk_w // 2, tm), lambda i: (0, i)),
        ],
        out_shape=[
            jax.ShapeDtypeStruct((S, qk_w // 2), BF16), jax.ShapeDtypeStruct((qk_w // 2, S), BF16),
            jax.ShapeDtypeStruct((S, rqk_w // 2), BF16), jax.ShapeDtypeStruct((rqk_w // 2, S), BF16),
        ],
        compiler_params=_params(("parallel",)),
    )(proj, proj, proj, w_conv, proj, cos4, sin4)


def _log_sigmoid(t):
    return jnp.minimum(t, 0.0) - jnp.log1p(jnp.exp(-jnp.abs(t)))


def _split3(t):
    hi = t.astype(BF16)
    r1 = t - hi.astype(F32)
    mid = r1.astype(BF16)
    lo = (r1 - mid.astype(F32)).astype(BF16)
    return hi, mid, lo


GATE_ROWS = 16


def _mlstm_gates(g_t, tri):
    logf_t = _log_sigmoid(g_t)
    cs = jnp.dot(jnp.concatenate(_split3(logf_t), axis=0), tri, preferred_element_type=F32)
    b_t = cs[0:GATE_ROWS] + cs[GATE_ROWS:2 * GATE_ROWS] + cs[2 * GATE_ROWS:]
    tot_t = jnp.sum(logf_t, axis=1, keepdims=True)
    r_t = pltpu.roll(g_t, 4, 0) - b_t
    wend_t = tot_t + r_t
    mloc_t = jnp.max(wend_t, axis=1, keepdims=True)
    e_t = jnp.exp(wend_t - mloc_t)
    return b_t, r_t, tot_t, mloc_t, e_t


def _mixers_kernel(mqf, mktf, mvf, gf, rqf, rktf, rvf, mqb, mktb, mvb, gb, rqb, rktb, rvb,
                   bias_ref, tri_ref, sel_ref, decay_ref, inner_ref, zeta_ref,
                   hf_ref, hb_ref, yf_ref, yb_ref, c_ref, m_ref, r_ref, *, chunk_decay, cps):
    @pl.when(pl.program_id(0) == 0)
    def _():
        c_ref[...] = jnp.zeros(c_ref.shape, F32)
        m_ref[...] = jnp.full(m_ref.shape, NEG_INIT, F32)
        r_ref[...] = jnp.zeros(r_ref.shape, F32)

    L = CHUNK
    row = lax.broadcasted_iota(jnp.int32, (L, L), 0)
    col = lax.broadcasted_iota(jnp.int32, (L, L), 1)
    ones = jnp.ones((L, ML_DIM), BF16)
    bias = bias_ref[...]

    all_spans = [[slice(sub * L, (sub + 1) * L) for sub in (t, cps - 1 - t)] for t in range(cps)]

    items = [(d, h) for d in range(2) for h in range(ML_HEADS)]
    m_refs = ((mqf, mktf, mvf, hf_ref), (mqb, mktb, mvb, hb_ref))
    r_refs = ((rqf, rktf, rvf, yf_ref), (rqb, rktb, rvb, yb_ref))

    def m_operands(spans, d, h):
        q_ref, kt_ref, v_ref, _ = m_refs[d]
        sl = slice(h * ML_DIM, (h + 1) * ML_DIM)
        return q_ref[spans[d], sl], kt_ref[sl, spans[d]], v_ref[spans[d], sl]

    def r_operands(spans, d, h):
        q_ref, kt_ref, v_ref, _ = r_refs[d]
        return (q_ref[spans[d], h * RET_QK:(h + 1) * RET_QK], kt_ref[h * RET_QK:(h + 1) * RET_QK, spans[d]],
                v_ref[spans[d], h * RET_V:(h + 1) * RET_V])

    m_raw = [[jnp.dot(*m_operands(spans, d, h)[:2], preferred_element_type=F32) for d, h in items]
             for spans in all_spans]
    r_raw = [[jnp.dot(*r_operands(spans, d, h)[:2], preferred_element_type=F32) for d, h in items]
             for spans in all_spans]

    all_gates, all_bcast = [], []
    for spans in all_spans:
        gates = [_mlstm_gates(g_ref[:, spans[d]] + bias, tri_ref[1 - d]) for d, g_ref in enumerate((gf, gb))]
        x_rows = jnp.concatenate([part for g in gates for part in _split3(g[0])], axis=0)
        all_gates.append(gates)
        all_bcast.append(lax.dot_general(x_rows, sel_ref[...], _TN, preferred_element_type=F32))

    m_cur = [m_ref[st:st + 1, :] for st in range(2 * ML_HEADS)]
    m_at, keep_at, gain_at = [], [], []
    for gates in all_gates:
        m_at.append(list(m_cur))
        keeps, gains = [], []
        for d, h in items:
            cf, st = 8 * d + 4 + h, ML_HEADS * d + h
            tot_t, mloc_t = gates[d][2], gates[d][3]
            blast = jnp.broadcast_to(tot_t[cf:cf + 1, :], (1, LANE))
            m_loc = jnp.broadcast_to(mloc_t[cf:cf + 1, :], (1, LANE))
            m_new = jnp.maximum(blast + m_cur[st], m_loc)
            keeps.append(jnp.exp(blast + m_cur[st] - m_new))
            gains.append(jnp.exp(m_loc - m_new))
            m_cur[st] = m_new
        keep_at.append(keeps)
        gain_at.append(gains)
    for st in range(2 * ML_HEADS):
        m_ref[st:st + 1, :] = m_cur[st]

    m_s, m_kw, m_winter, m_floor, r_s, r_kw = [], [], [], [], [], []
    for t, spans in enumerate(all_spans):
        ms_t, mkw_t, mwi_t, mfl_t, rs_t, rkw_t = [], [], [], [], [], []
        for j, (d, h) in enumerate(items):
            cf, st = 8 * d + 4 + h, ML_HEADS * d + h
            _, r_t, _, _, e_t = all_gates[t][d]
            mask = (row >= col) if d == 0 else (row <= col)
            bc = all_bcast[t][:, st * LANE:(st + 1) * LANE]
            dlog = jnp.where(mask, bc + r_t[cf:cf + 1, :], -jnp.inf)
            inter_log = bc + m_at[t][st]
            m_t = jnp.maximum(jnp.max(dlog, axis=1, keepdims=True), inter_log)
            ms_t.append((m_raw[t][j] * jnp.exp(dlog - m_t)).astype(BF16))
            mwi_t.append(jnp.exp(inter_log - m_t))
            mfl_t.append(jnp.exp(-m_t))
            kt = m_operands(spans, d, h)[1]
            mkw_t.append((kt.astype(F32) * e_t[cf:cf + 1, :]).astype(BF16))
            rs_t.append((r_raw[t][j] * decay_ref[st]).astype(BF16))
            rkt = r_operands(spans, d, h)[1]
            rkw_t.append((rkt.astype(F32) * zeta_ref[st:st + 1, :]).astype(BF16))
        m_s.append(ms_t), m_kw.append(mkw_t), m_winter.append(mwi_t), m_floor.append(mfl_t)
        r_s.append(rs_t), r_kw.append(rkw_t)

    c_cur = [c_ref[st] for st in range(2 * ML_HEADS)]
    r_cur = [r_ref[st] for st in range(2 * RET_HEADS)]
    for t, spans in enumerate(all_spans):
        v1s = [jnp.concatenate([m_operands(spans, d, h)[2], ones], axis=1) for d, h in items]
        intra = [jnp.dot(m_s[t][j], v1s[j], preferred_element_type=F32) for j in range(len(items))]
        inter = [jnp.dot(m_operands(spans, d, h)[0], c_cur[ML_HEADS * d + h].astype(BF16),
                         preferred_element_type=F32) for d, h in items]
        c_loc = [jnp.dot(m_kw[t][j], v1s[j], preferred_element_type=F32) for j in range(len(items))]
        ry_intra = [jnp.dot(r_s[t][j], r_operands(spans, d, h)[2], preferred_element_type=F32)
                    for j, (d, h) in enumerate(items)]
        ry_inter = [jnp.dot(r_operands(spans, d, h)[0], r_cur[RET_HEADS * d + h].astype(BF16),
                            preferred_element_type=F32) for d, h in items]
        r_loc = [jnp.dot(r_kw[t][j], r_operands(spans, d, h)[2], preferred_element_type=F32)
                 for j, (d, h) in enumerate(items)]
        for j, (d, h) in enumerate(items):
            st = ML_HEADS * d + h
            w_inter = m_winter[t][j]
            num = intra[j][:, :ML_DIM] + w_inter * inter[j][:, :ML_DIM]
            den = intra[j][:, ML_DIM:] + w_inter * inter[j][:, ML_DIM:]
            m_refs[d][3][spans[d], h * ML_DIM:(h + 1) * ML_DIM] = num / jnp.maximum(jnp.abs(den), m_floor[t][j])
            keep, gain = keep_at[t][j], gain_at[t][j]
            c_cur[st] = (jnp.concatenate([keep, keep], axis=1) * c_cur[st]
                         + jnp.concatenate([gain, gain], axis=1) * c_loc[j])
            r_refs[d][3][spans[d], h * RET_V:(h + 1) * RET_V] = ry_intra[j] + ry_inter[j] * inner_ref[st]
            r_cur[st] = chunk_decay[st] * r_cur[st] + r_loc[j]
    for st in range(2 * ML_HEADS):
        c_ref[st] = c_cur[st]
        r_ref[st] = r_cur[st]


def _gate_select_matrix():
    sel = np.zeros((2 * 3 * GATE_ROWS, 2 * ML_HEADS * LANE), np.float32)
    for d in range(2):
        for part in range(3):
            for h in range(ML_HEADS):
                st = ML_HEADS * d + h
                sel[(3 * d + part) * GATE_ROWS + 8 * d + 4 + h, st * LANE:(st + 1) * LANE] = 1.0
    return jnp.asarray(sel, BF16)


def _mixers(mq, mkt, rq, rkt, proj, gates_t, bias_t, layer, tri, sel, ret_consts, *, cps):
    S = proj.shape[0]
    rows = cps * CHUNK
    nb = S // rows
    MW = ML_HEADS * ML_DIM
    RQ = RET_HEADS * RET_QK
    RV = RET_HEADS * RET_V
    decay, inner, zeta, chunk_decay = ret_consts
    fwd = lambda n: n
    bwd = lambda n: nb - 1 - n

    def specs(idx):
        return [
            pl.BlockSpec((rows, MW), lambda n: (idx(n), 0)),
            pl.BlockSpec((MW, rows), lambda n: (0, idx(n))),
            pl.BlockSpec((rows, MW), lambda n: (idx(n), COL_ML_V * LANE // MW)),
            pl.BlockSpec((GATE_ROWS, rows), lambda n: (0, idx(n))),
            pl.BlockSpec((rows, RQ), lambda n: (idx(n), 0)),
            pl.BlockSpec((RQ, rows), lambda n: (0, idx(n))),
            pl.BlockSpec((rows, RV), lambda n: (idx(n), COL_RV * LANE // RV)),
        ]

    whole = lambda a: pl.BlockSpec(a.shape, lambda n: (0,) * a.ndim)
    out = lambda idx, w: pl.BlockSpec((rows, w), lambda n: (idx(n), 0))
    operands = (mq, mkt, proj, gates_t, rq, rkt, proj)
    return pl.pallas_call(
        functools.partial(_mixers_kernel, chunk_decay=chunk_decay, cps=cps),
        name="mixers",
        grid=(nb,),
        in_specs=specs(fwd) + specs(bwd) + [
            pl.BlockSpec((None, GATE_ROWS, CHUNK), lambda n: (layer, 0, 0)),
            whole(tri), whole(sel), whole(decay), whole(inner), whole(zeta),
        ],
        out_specs=[out(fwd, MW), out(bwd, MW), out(fwd, RV), out(bwd, RV)],
        out_shape=[jax.ShapeDtypeStruct((S, MW), F32), jax.ShapeDtypeStruct((S, MW), F32),
                   jax.ShapeDtypeStruct((S, RV), F32), jax.ShapeDtypeStruct((S, RV), F32)],
        scratch_shapes=[pltpu.VMEM((2 * ML_HEADS, ML_DIM, 2 * ML_DIM), F32), pltpu.VMEM((2 * ML_HEADS, LANE), F32),
                        pltpu.VMEM((2 * RET_HEADS, RET_QK, RET_V), F32)],
        compiler_params=_params(("arbitrary",)),
    )(*operands, *operands, bias_t, tri, sel, decay, inner, zeta)


def _retention_consts():
    log_gamma = jnp.log1p(-jnp.exp2(-RET_DECAY_BASE - jnp.arange(RET_HEADS, dtype=F32)))
    idx = jnp.arange(CHUNK, dtype=F32)
    diff = idx[:, None] - idx[None, :]
    decay, inner, zeta = [], [], []
    for d in range(2):
        lg_d = log_gamma if d == 0 else log_gamma[::-1]
        dd = diff if d == 0 else -diff
        pos = idx if d == 0 else (CHUNK - 1.0 - idx)
        keep = dd >= 0
        for h in range(RET_HEADS):
            lg = lg_d[h]
            decay.append(jnp.where(keep, jnp.exp(jnp.where(keep, dd, 0.0) * lg), 0.0))
            inner.append(jnp.broadcast_to(jnp.exp((pos + 1.0) * lg)[:, None], (CHUNK, LANE)))
            zeta.append(jnp.exp((CHUNK - 1.0 - pos) * lg))
    lg_np = np.log1p(-np.exp2(-RET_DECAY_BASE - np.arange(RET_HEADS, dtype=np.float32))).astype(np.float32)
    chunk_decay = tuple(float(np.exp(np.float32(CHUNK) * lg)) for lg in list(lg_np) + list(lg_np[::-1]))
    return jnp.stack(decay), jnp.stack(inner), jnp.stack(zeta), chunk_decay


def _mla_prep_kernel(cq_ref, ckv_ref, kr_ref, gq_ref, gkv_ref, wq_ref, wkv_ref, cos_ref, sin_ref,
                     q_out, k_out, v_out):
    tm = cq_ref.shape[0]
    cos4 = cos_ref[...]
    sin4 = sin_ref[...]
    qn = _rmsnorm(cq_ref[...].astype(F32), gq_ref[...]).astype(BF16)
    kvn = _rmsnorm(ckv_ref[...].astype(F32), gkv_ref[...]).astype(BF16)
    q_all = jnp.dot(qn, wq_ref[...], preferred_element_type=F32)
    kv_all = jnp.dot(kvn, wkv_ref[...], preferred_element_type=F32)
    q_scale = (MLA_QK ** -0.5) * math.log2(math.e)
    nope_w = MLA_HEADS * MLA_NOPE
    q_rope = _rope_grouped(q_all[:, nope_w:], jnp.concatenate([cos4, cos4], axis=1),
                           jnp.concatenate([sin4, sin4], axis=1)) * q_scale
    k_rope = _rope_grouped(kr_ref[...].astype(F32), cos4[:, :LANE], sin4[:, :LANE])[:, :ROPE_DIM].astype(BF16)
    ones = jnp.ones((tm, MLA_V), BF16)
    for h in range(MLA_HEADS):
        q_out[h, :, :MLA_NOPE] = (q_all[:, h * MLA_NOPE:(h + 1) * MLA_NOPE] * q_scale).astype(BF16)
        q_out[h, :, MLA_NOPE:] = q_rope[:, h * ROPE_DIM:(h + 1) * ROPE_DIM].astype(BF16)
        kv0 = h * (MLA_NOPE + MLA_V)
        k_out[h, :, :MLA_NOPE] = kv_all[:, kv0:kv0 + MLA_NOPE].astype(BF16)
        k_out[h, :, MLA_NOPE:] = k_rope
        v_out[h, :, :MLA_V] = kv_all[:, kv0 + MLA_NOPE:kv0 + MLA_NOPE + MLA_V].astype(BF16)
        v_out[h, :, MLA_V:] = ones


def _mla_prep(proj, g_q, g_kv, w_q, w_kv, layer, cos4, sin4, *, tm):
    S = proj.shape[0]
    q_rank, kv_rank = w_q.shape[1], w_kv.shape[1]
    return pl.pallas_call(
        _mla_prep_kernel,
        name="mla_prep",
        grid=(S // tm,),
        in_specs=[
            pl.BlockSpec((tm, q_rank), lambda i: (i, COL_CQ * LANE // q_rank)),
            pl.BlockSpec((tm, kv_rank), lambda i: (i, COL_CKV * LANE // kv_rank)),
            pl.BlockSpec((tm, LANE), lambda i: (i, COL_KROPE)),
            pl.BlockSpec((None, 1, q_rank), lambda i: (layer, 0, 0)),
            pl.BlockSpec((None, 1, kv_rank), lambda i: (layer, 0, 0)),
            pl.BlockSpec((None, q_rank, w_q.shape[2]), lambda i: (layer, 0, 0)),
            pl.BlockSpec((None, kv_rank, w_kv.shape[2]), lambda i: (layer, 0, 0)),
            pl.BlockSpec((tm, 2 * LANE), lambda i: (i, 0)),
            pl.BlockSpec((tm, 2 * LANE), lambda i: (i, 0)),
        ],
        out_specs=[
            pl.BlockSpec((MLA_HEADS, tm, MLA_QK), lambda i: (0, i, 0)),
            pl.BlockSpec((MLA_HEADS, tm, MLA_QK), lambda i: (0, i, 0)),
            pl.BlockSpec((MLA_HEADS, tm, 2 * MLA_V), lambda i: (0, i, 0)),
        ],
        out_shape=[
            jax.ShapeDtypeStruct((MLA_HEADS, S, MLA_QK), BF16),
            jax.ShapeDtypeStruct((MLA_HEADS, S, MLA_QK), BF16),
            jax.ShapeDtypeStruct((MLA_HEADS, S, 2 * MLA_V), BF16),
        ],
        compiler_params=_params(("parallel",)),
    )(proj, proj, proj, g_q, g_kv, w_q, w_kv, cos4, sin4)


def _attn_kernel(q_ref, k_ref, v_ref, *rest, tk, nk, n_cast):
    w_refs, o_ref, wb_refs = rest[:n_cast], rest[n_cast], rest[n_cast + 1:]
    for w_ref, wb_ref in zip(w_refs, wb_refs):
        wb_ref[...] = w_ref[...].astype(wb_ref.dtype)

    q = q_ref[...]
    m = acc = None
    for c in range(nk):
        k = k_ref[c * tk:(c + 1) * tk, :]
        v = v_ref[c * tk:(c + 1) * tk, :]
        s = lax.dot_general(q, k, _NT, preferred_element_type=F32)
        m_c = jnp.max(s, axis=-1, keepdims=True)
        if c == 0:
            m = m_c
            acc = jnp.dot(jnp.exp2(s - m).astype(BF16), v, preferred_element_type=F32)
        else:
            m_new = jnp.maximum(m, m_c)
            acc = jnp.exp2(m - m_new) * acc + jnp.dot(jnp.exp2(s - m_new).astype(BF16), v, preferred_element_type=F32)
            m = m_new
    o_ref[...] = (acc[:, :MLA_V] / acc[:, MLA_V:]).astype(o_ref.dtype)


def _attention(q, k, v, cast_weights, layer, *, tq, tk):
    S = q.shape[1]
    nq = S // tq
    n_steps = MLA_HEADS * nq
    w_specs, wb_specs, wb_shapes = [], [], []
    for w in cast_weights:
        _, rows, cols = w.shape
        rb = rows // n_steps
        assert rb * n_steps == rows and rb % BF16_ROWS == 0
        w_specs.append(pl.BlockSpec((None, rb, cols), lambda h, i: (layer, h * nq + i, 0)))
        wb_specs.append(pl.BlockSpec((rb, cols), lambda h, i: (h * nq + i, 0)))
        wb_shapes.append(jax.ShapeDtypeStruct((rows, cols), BF16))
    outs = pl.pallas_call(
        functools.partial(_attn_kernel, tk=tk, nk=S // tk, n_cast=len(cast_weights)),
        name="attention",
        grid=(MLA_HEADS, nq),
        in_specs=[
            pl.BlockSpec((None, tq, MLA_QK), lambda h, i: (h, i, 0)),
            pl.BlockSpec((None, S, MLA_QK), lambda h, i: (h, 0, 0)),
            pl.BlockSpec((None, S, 2 * MLA_V), lambda h, i: (h, 0, 0)),
        ] + w_specs,
        out_specs=[pl.BlockSpec((tq, MLA_V), lambda h, i: (i, h))] + wb_specs,
        out_shape=[jax.ShapeDtypeStruct((S, MLA_HEADS * MLA_V), BF16)] + wb_shapes,
        compiler_params=_params(("parallel", "arbitrary")),
    )(q, k, v, *cast_weights)
    return outs[0], outs[1:]


def _pack_w_in_kernel(w_ref, o_ref):
    n_in, tc = w_ref.shape
    ml_end = 4 * ML_HEADS * ML_DIM
    gates = 4 * ML_HEADS
    rest = n_in - ml_end - gates
    gate_row = IN_WIDTH_PACKED - LANE
    assert ml_end + rest + (LANE - ROPE_DIM) == gate_row
    o_ref[:ml_end, :] = w_ref[:ml_end, :].astype(BF16)
    o_ref[ml_end:ml_end + rest, :] = w_ref[ml_end + gates:, :].astype(BF16)
    o_ref[ml_end + rest:gate_row, :] = jnp.zeros((LANE - ROPE_DIM, tc), BF16)
    o_ref[gate_row:gate_row + gates, :] = w_ref[ml_end:ml_end + gates, :].astype(BF16)
    o_ref[gate_row + gates:, :] = jnp.zeros((LANE - gates, tc), BF16)


def _pack_w_in(w_in_t, *, tc):
    L, n_in, K = w_in_t.shape
    return pl.pallas_call(
        _pack_w_in_kernel,
        name="pack_w_in",
        grid=(L, K // tc),
        in_specs=[pl.BlockSpec((None, n_in, tc), lambda l, i: (l, 0, i))],
        out_specs=pl.BlockSpec((None, IN_WIDTH_PACKED, tc), lambda l, i: (l, 0, i)),
        out_shape=jax.ShapeDtypeStruct((L, IN_WIDTH_PACKED, K), BF16),
        compiler_params=_params(("parallel", "parallel")),
    )(w_in_t)


def kernel(x, positions, g_mix, w_in, b_gates, w_conv, g_ml_out, g_ret_out, g_q_norm, w_q_up, g_kv_norm, w_kv_up,
           w_out, g_ffn, w_ff1, w_ff2, g_final):
    B, S, D = x.shape
    assert B == 1
    depth = w_in.shape[0]
    tm = min(512, S)
    tm_big = min(1024, S)

    half = ROPE_DIM // 2
    inv = ROPE_THETA ** (-jnp.arange(half, dtype=F32) / half)
    ang = positions[0].astype(F32)[:, None] * inv
    cos4 = jnp.tile(jnp.concatenate([jnp.cos(ang), jnp.cos(ang)], axis=-1), (1, RET_HEADS))
    sin4 = jnp.tile(jnp.concatenate([-jnp.sin(ang), jnp.sin(ang)], axis=-1), (1, RET_HEADS))

    w_in_p = _pack_w_in(jnp.swapaxes(w_in, 1, 2), tc=256)
    q_rank = w_q_up.shape[1]
    w_q4 = w_q_up.astype(BF16).reshape(depth, q_rank, MLA_HEADS, MLA_QK)
    w_q = jnp.concatenate([w_q4[..., :MLA_NOPE].reshape(depth, q_rank, MLA_HEADS * MLA_NOPE),
                           w_q4[..., MLA_NOPE:].reshape(depth, q_rank, MLA_HEADS * ROPE_DIM)], axis=-1)
    w_kv = w_kv_up.astype(BF16)
    bias_t = jnp.broadcast_to(b_gates[:, :, None], (depth, GATE_ROWS, CHUNK))
    idx = jnp.arange(CHUNK)
    tri = jnp.stack([idx[:, None] >= idx[None, :], idx[:, None] <= idx[None, :]]).astype(BF16)
    sel = _gate_select_matrix()
    ret_consts = _retention_consts()
    r3 = lambda g: g[:, None, :]

    xs = x[0]
    for l in range(depth):
        proj, gates_t = _in_proj(xs, r3(g_mix), w_in_p, l, tm=tm_big, tn=1536)
        mq, mkt, rq, rkt = _prep(proj, w_conv, l, cos4, sin4, tm=tm)
        h_f, h_b, y_f, y_b = _mixers(mq, mkt, rq, rkt, proj, gates_t, bias_t, l, tri, sel, ret_consts, cps=2)
        q, k, v = _mla_prep(proj, r3(g_q_norm), r3(g_kv_norm), w_q, w_kv, l, cos4, sin4, tm=tm)
        y_mla, (w_out_b, w_ff1_b, w_ff2_b) = _attention(q, k, v, (w_out, w_ff1, w_ff2), l,
                                                        tq=min(2048, S), tk=min(256, S))
        xs = _out_proj(h_f, h_b, y_f, y_b, proj, y_mla, r3(g_ml_out), r3(g_ret_out), l, w_out_b, xs, tm=tm)
        act = _ffn1(xs, r3(g_ffn), w_ff1_b, l, tm=tm_big, tn=2048)
        xs = _ffn2(act, w_ff2_b, xs, tm=tm_big, tn=1024, tk=4096)
    return _final_rms(xs, g_final[None, :], tm=tm)[None]
```

```python
import functools
import math

import numpy as np
import jax
import jax.numpy as jnp
from jax import lax
from jax.experimental import pallas as pl
from jax.experimental.pallas import tpu as pltpu

F32 = jnp.float32
BF16 = jnp.bfloat16

EPS = 1e-6
NEG_INIT = -1e30
ROPE_THETA = 10000.0
CHUNK = 128
LANE = 128
BF16_ROWS = 16
ML_HEADS = 4
ML_DIM = 128
RET_HEADS = 4
RET_QK = 64
RET_V = 128
RET_DECAY_BASE = 5.0
MLA_HEADS = 8
MLA_NOPE = 128
ROPE_DIM = 64
MLA_V = 128
MLA_QK = MLA_NOPE + ROPE_DIM
VMEM_LIMIT = 56 * 1024 * 1024

COL_ML_Q, COL_ML_K, COL_ML_V, COL_ML_O = 0, 4, 8, 12
COL_RQK, COL_RV, COL_RG = 16, 20, 24
COL_CQ, COL_CKV, COL_KROPE, COL_GATES = 28, 32, 34, 35
IN_WIDTH_PACKED = 36 * LANE

_NT = (((1,), (1,)), ((), ()))
_TN = (((0,), (0,)), ((), ()))


def _params(sem):
    return pltpu.CompilerParams(dimension_semantics=sem, vmem_limit_bytes=VMEM_LIMIT)


def _rmsnorm(x, g):
    return x * lax.rsqrt(jnp.mean(x * x, axis=-1, keepdims=True) + EPS) * g


def _in_proj_kernel(x_ref, g_ref, w_ref, o_ref, gates_ref, hn_ref):
    j = pl.program_id(1)

    @pl.when(j == 0)
    def _():
        hn_ref[...] = _rmsnorm(x_ref[...], g_ref[...]).astype(hn_ref.dtype)

    y = lax.dot_general(hn_ref[...], w_ref[...], _NT, preferred_element_type=F32)
    o_ref[...] = y.astype(o_ref.dtype)

    @pl.when(j == pl.num_programs(1) - 1)
    def _():
        gates_ref[...] = y[:, y.shape[1] - LANE:].T


def _in_proj(x, g, w, layer, *, tm, tn):
    S, K = x.shape
    N = w.shape[1]
    return pl.pallas_call(
        _in_proj_kernel,
        name="in_proj",
        grid=(S // tm, N // tn),
        in_specs=[
            pl.BlockSpec((tm, K), lambda i, j: (i, 0)),
            pl.BlockSpec((None, 1, K), lambda i, j: (layer, 0, 0)),
            pl.BlockSpec((None, tn, K), lambda i, j: (layer, j, 0)),
        ],
        out_specs=[pl.BlockSpec((tm, tn), lambda i, j: (i, j)), pl.BlockSpec((LANE, tm), lambda i, j: (0, i))],
        out_shape=[jax.ShapeDtypeStruct((S, N), BF16), jax.ShapeDtypeStruct((LANE, S), F32)],
        scratch_shapes=[pltpu.VMEM((tm, K), BF16)],
        compiler_params=_params(("parallel", "arbitrary")),
    )(x, g, w)


def _out_proj_kernel(hf, hb, o_ref, gml_ref, yf, yb, rg_ref, gret_ref, ymla_ref, w_ref, x_ref, out_ref):
    w_ml = ML_HEADS * ML_DIM
    w_ret = RET_HEADS * RET_V
    acc = x_ref[...] + jnp.dot(ymla_ref[...], w_ref[w_ml + w_ret:, :], preferred_element_type=F32)
    y_ml, y_ret = [], []
    for h in range(ML_HEADS):
        sl = slice(h * ML_DIM, (h + 1) * ML_DIM)
        y = _rmsnorm(hf[:, sl].astype(F32) + hb[:, sl].astype(F32), gml_ref[:, sl])
        y_ml.append((y / (1.0 + jnp.exp(-o_ref[:, sl].astype(F32)))).astype(BF16))
    for h in range(RET_HEADS):
        sl = slice(h * RET_V, (h + 1) * RET_V)
        y = _rmsnorm(yf[:, sl].astype(F32) + yb[:, sl].astype(F32), gret_ref[:, sl])
        gate = rg_ref[:, sl].astype(F32)
        y_ret.append((gate / (1.0 + jnp.exp(-gate)) * y).astype(BF16))
    acc = acc + jnp.dot(jnp.concatenate(y_ml, axis=1), w_ref[:w_ml, :], preferred_element_type=F32)
    acc = acc + jnp.dot(jnp.concatenate(y_ret, axis=1), w_ref[w_ml:w_ml + w_ret, :], preferred_element_type=F32)
    out_ref[...] = acc


def _out_proj(h_f, h_b, y_f, y_b, proj, y_mla, g_ml, g_ret, layer, w, x, *, tm):
    S, N = x.shape
    W = ML_HEADS * ML_DIM
    blk = lambda c: pl.BlockSpec((tm, W), lambda i: (i, c))
    gain = pl.BlockSpec((None, 1, W), lambda i: (layer, 0, 0))
    return pl.pallas_call(
        _out_proj_kernel,
        name="out_proj",
        grid=(S // tm,),
        in_specs=[blk(0), blk(0), blk(COL_ML_O * LANE // W), gain, blk(0), blk(0), blk(COL_RG * LANE // W), gain,
                  pl.BlockSpec((tm, y_mla.shape[1]), lambda i: (i, 0)),
                  pl.BlockSpec(w.shape, lambda i: (0, 0)),
                  pl.BlockSpec((tm, N), lambda i: (i, 0))],
        out_specs=pl.BlockSpec((tm, N), lambda i: (i, 0)),
        out_shape=jax.ShapeDtypeStruct((S, N), F32),
        compiler_params=_params(("parallel",)),
    )(h_f, h_b, proj, g_ml, y_f, y_b, proj, g_ret, y_mla, w, x)


def _ffn1_kernel(x_ref, g_ref, w_ref, o_ref, hn_ref):
    @pl.when(pl.program_id(1) == 0)
    def _():
        hn_ref[...] = _rmsnorm(x_ref[...], g_ref[...]).astype(hn_ref.dtype)

    y = jnp.dot(hn_ref[...], w_ref[...], preferred_element_type=F32)
    o_ref[...] = jnp.square(jnp.maximum(y, 0.0)).astype(o_ref.dtype)


def _ffn1(x, g, w, layer, *, tm, tn):
    S, K = x.shape
    N = w.shape[-1]
    return pl.pallas_call(
        _ffn1_kernel,
        name="ffn1",
        grid=(S // tm, N // tn),
        in_specs=[
            pl.BlockSpec((tm, K), lambda i, j: (i, 0)),
            pl.BlockSpec((None, 1, K), lambda i, j: (layer, 0, 0)),
            pl.BlockSpec((K, tn), lambda i, j: (0, j)),
        ],
        out_specs=pl.BlockSpec((tm, tn), lambda i, j: (i, j)),
        out_shape=jax.ShapeDtypeStruct((S, N), BF16),
        scratch_shapes=[pltpu.VMEM((tm, K), BF16)],
        compiler_params=_params(("parallel", "arbitrary")),
    )(x, g, w)


def _ffn2_kernel(a_ref, w_ref, x_ref, o_ref):
    @pl.when(pl.program_id(2) == 0)
    def _():
        o_ref[...] = x_ref[...]

    o_ref[...] += jnp.dot(a_ref[...], w_ref[...], preferred_element_type=F32)


def _ffn2(a, w, x, *, tm, tn, tk):
    S, N = x.shape
    K = a.shape[1]
    return pl.pallas_call(
        _ffn2_kernel,
        name="ffn2",
        grid=(S // tm, N // tn, K // tk),
        in_specs=[
            pl.BlockSpec((tm, tk), lambda i, j, k: (i, k)),
            pl.BlockSpec((tk, tn), lambda i, j, k: (k, j)),
            pl.BlockSpec((tm, tn), lambda i, j, k: (i, j)),
        ],
        out_specs=pl.BlockSpec((tm, tn), lambda i, j, k: (i, j)),
        out_shape=jax.ShapeDtypeStruct((S, N), F32),
        compiler_params=_params(("parallel", "arbitrary", "arbitrary")),
    )(a, w, x)


def _final_rms_kernel(x_ref, g_ref, o_ref):
    o_ref[...] = _rmsnorm(x_ref[...], g_ref[...])


def _final_rms(x, g, *, tm):
    S, K = x.shape
    return pl.pallas_call(
        _final_rms_kernel,
        name="final_rms",
        grid=(S // tm,),
        in_specs=[pl.BlockSpec((tm, K), lambda i: (i, 0)), pl.BlockSpec((1, K), lambda i: (0, 0))],
        out_specs=pl.BlockSpec((tm, K), lambda i: (i, 0)),
        out_shape=jax.ShapeDtypeStruct((S, K), F32),
        compiler_params=_params(("parallel",)),
    )(x, g)


def _rope_grouped(t, cosf, sinf):
    width = t.shape[1]
    lane = lax.broadcasted_iota(jnp.int32, t.shape, 1)
    first_half = (lane % ROPE_DIM) < (ROPE_DIM // 2)
    partner = jnp.where(first_half, pltpu.roll(t, width - ROPE_DIM // 2, 1), pltpu.roll(t, ROPE_DIM // 2, 1))
    return t * cosf + partner * sinf


def _prep_kernel(qk_ref, prev_ref, next_ref, wc_ref, rqk_ref, cos_ref, sin_ref, mq_out, mkt_out, rq_out, rkt_out,
                 *, tm, nblk):
    i = pl.program_id(0)
    x = qk_ref[...].astype(F32)
    row = lax.broadcasted_iota(jnp.int32, x.shape, 0)
    prev_row = jnp.where(i > 0, prev_ref[BF16_ROWS - 1:BF16_ROWS, :].astype(F32), 0.0)
    next_row = jnp.where(i < nblk - 1, next_ref[0:1, :].astype(F32), 0.0)
    x_m = jnp.where(row == 0, prev_row, pltpu.roll(x, 1, 0))
    x_p = jnp.where(row == tm - 1, next_row, pltpu.roll(x, tm - 1, 0))
    w = wc_ref[...]
    y = x_m * w[0:1, :] + x * w[1:2, :] + x_p * w[2:3, :]
    y = y / (1.0 + jnp.exp(-y))
    half = ML_HEADS * ML_DIM
    mq_out[...] = (y[:, :half] * (ML_DIM ** -0.5)).astype(mq_out.dtype)
    mkt_out[...] = y[:, half:].T.astype(mkt_out.dtype)

    cosf = cos_ref[...]
    sinf = sin_ref[...]
    width = RET_HEADS * RET_QK
    r = rqk_ref[...].astype(F32)
    rq_out[...] = _rope_grouped(r[:, :width], cosf, sinf).astype(rq_out.dtype)
    rkt_out[...] = (_rope_grouped(r[:, width:], cosf, sinf) * (RET_QK ** -0.5)).T.astype(rkt_out.dtype)


def _prep(proj, w_conv, layer, cos4, sin4, *, tm):
    S = proj.shape[0]
    nblk = S // tm
    qk_w = 2 * ML_HEADS * ML_DIM
    rqk_w = 2 * RET_HEADS * RET_QK
    hb = tm // BF16_ROWS
    return pl.pallas_call(
        functools.partial(_prep_kernel, tm=tm, nblk=nblk),
        name="prep",
        grid=(nblk,),
        in_specs=[
            pl.BlockSpec((tm, qk_w), lambda i: (i, 0)),
            pl.BlockSpec((BF16_ROWS, qk_w), lambda i: (jnp.maximum(i * hb - 1, 0), 0)),
            pl.BlockSpec((BF16_ROWS, qk_w), lambda i: (jnp.minimum((i + 1) * hb, S // BF16_ROWS - 1), 0)),
            pl.BlockSpec((None, 3, qk_w), lambda i: (layer, 0, 0)),
            pl.BlockSpec((tm, rqk_w), lambda i: (i, COL_RQK * LANE // rqk_w)),
            pl.BlockSpec((tm, rqk_w // 2), lambda i: (i, 0)),
            pl.BlockSpec((tm, rqk_w // 2), lambda i: (i, 0)),
        ],
        out_specs=[
            pl.BlockSpec((tm, qk_w // 2), lambda i: (i, 0)), pl.BlockSpec((qk_w // 2, tm), lambda i: (0, i)),
            pl.BlockSpec((tm, rqk_w // 2), lambda i: (i, 0)), pl.BlockSpec((rqk_w // 2, tm), lambda i: (0, i)),
        ],
        out_shape=[
            jax.ShapeDtypeStruct((S, qk_w // 2), BF16), jax.ShapeDtypeStruct((qk_w // 2, S), BF16),
            jax.ShapeDtypeStruct((S, rqk_w // 2), BF16), jax.ShapeDtypeStruct((rqk_w // 2, S), BF16),
        ],
        compiler_params=_params(("parallel",)),
    )(proj, proj, proj, w_conv, proj, cos4, sin4)


def _log_sigmoid(t):
    return jnp.minimum(t, 0.0) - jnp.log1p(jnp.exp(-jnp.abs(t)))


def _split3(t):
    hi = t.astype(BF16)
    r1 = t - hi.astype(F32)
    mid = r1.astype(BF16)
    lo = (r1 - mid.astype(F32)).astype(BF16)
    return hi, mid, lo


GATE_ROWS = 16


def _mlstm_gates(g_t, tri):
    logf_t = _log_sigmoid(g_t)
    cs = jnp.dot(jnp.concatenate(_split3(logf_t), axis=0), tri, preferred_element_type=F32)
    b_t = cs[0:GATE_ROWS] + cs[GATE_ROWS:2 * GATE_ROWS] + cs[2 * GATE_ROWS:]
    tot_t = jnp.sum(logf_t, axis=1, keepdims=True)
    r_t = pltpu.roll(g_t, 4, 0) - b_t
    wend_t = tot_t + r_t
    mloc_t = jnp.max(wend_t, axis=1, keepdims=True)
    e_t = jnp.exp(wend_t - mloc_t)
    return b_t, r_t, tot_t, mloc_t, e_t


def _mixers_kernel(mqf, mktf, mvf, gf, rqf, rktf, rvf, mqb, mktb, mvb, gb, rqb, rktb, rvb,
                   bias_ref, tri_ref, sel_ref, decay_ref, inner_ref, zeta_ref,
                   hf_ref, hb_ref, yf_ref, yb_ref, c_ref, m_ref, r_ref, *, chunk_decay, cps):
    @pl.when(pl.program_id(0) == 0)
    def _():
        c_ref[...] = jnp.zeros(c_ref.shape, F32)
        m_ref[...] = jnp.full(m_ref.shape, NEG_INIT, F32)
        r_ref[...] = jnp.zeros(r_ref.shape, F32)

    L = CHUNK
    row = lax.broadcasted_iota(jnp.int32, (L, L), 0)
    col = lax.broadcasted_iota(jnp.int32, (L, L), 1)
    ones = jnp.ones((L, ML_DIM), BF16)
    bias = bias_ref[...]

    all_spans = [[slice(sub * L, (sub + 1) * L) for sub in (t, cps - 1 - t)] for t in range(cps)]

    items = [(d, h) for d in range(2) for h in range(ML_HEADS)]
    m_refs = ((mqf, mktf, mvf, hf_ref), (mqb, mktb, mvb, hb_ref))
    r_refs = ((rqf, rktf, rvf, yf_ref), (rqb, rktb, rvb, yb_ref))

    def m_operands(spans, d, h):
        q_ref, kt_ref, v_ref, _ = m_refs[d]
        sl = slice(h * ML_DIM, (h + 1) * ML_DIM)
        return q_ref[spans[d], sl], kt_ref[sl, spans[d]], v_ref[spans[d], sl]

    def r_operands(spans, d, h):
        q_ref, kt_ref, v_ref, _ = r_refs[d]
        return (q_ref[spans[d], h * RET_QK:(h + 1) * RET_QK], kt_ref[h * RET_QK:(h + 1) * RET_QK, spans[d]],
                v_ref[spans[d], h * RET_V:(h + 1) * RET_V])

    m_raw = [[jnp.dot(*m_operands(spans, d, h)[:2], preferred_element_type=F32) for d, h in items]
             for spans in all_spans]
    r_raw = [[jnp.dot(*r_operands(spans, d, h)[:2], preferred_element_type=F32) for d, h in items]
             for spans in all_spans]

    all_gates, all_bcast = [], []
    for spans in all_spans:
        gates = [_mlstm_gates(g_ref[:, spans[d]] + bias, tri_ref[1 - d]) for d, g_ref in enumerate((gf, gb))]
        x_rows = jnp.concatenate([part for g in gates for part in _split3(g[0])], axis=0)
        all_gates.append(gates)
        all_bcast.append(lax.dot_general(x_rows, sel_ref[...], _TN, preferred_element_type=F32))

    m_cur = [m_ref[st:st + 1, :] for st in range(2 * ML_HEADS)]
    m_at, keep_at, gain_at = [], [], []
    for gates in all_gates:
        m_at.append(list(m_cur))
        keeps, gains = [], []
        for d, h in items:
            cf, st = 8 * d + 4 + h, ML_HEADS * d + h
            tot_t, mloc_t = gates[d][2], gates[d][3]
            blast = jnp.broadcast_to(tot_t[cf:cf + 1, :], (1, LANE))
            m_loc = jnp.broadcast_to(mloc_t[cf:cf + 1, :], (1, LANE))
            m_new = jnp.maximum(blast + m_cur[st], m_loc)
            keeps.append(jnp.exp(blast + m_cur[st] - m_new))
            gains.append(jnp.exp(m_loc - m_new))
            m_cur[st] = m_new
        keep_at.append(keeps)
        gain_at.append(gains)
    for st in range(2 * ML_HEADS):
        m_ref[st:st + 1, :] = m_cur[st]

    m_s, m_kw, m_winter, m_floor, r_s, r_kw = [], [], [], [], [], []
    for t, spans in enumerate(all_spans):
        ms_t, mkw_t, mwi_t, mfl_t, rs_t, rkw_t = [], [], [], [], [], []
        for j, (d, h) in enumerate(items):
            cf, st = 8 * d + 4 + h, ML_HEADS * d + h
            _, r_t, _, _, e_t = all_gates[t][d]
            mask = (row >= col) if d == 0 else (row <= col)
            bc = all_bcast[t][:, st * LANE:(st + 1) * LANE]
            dlog = jnp.where(mask, bc + r_t[cf:cf + 1, :], -jnp.inf)
            inter_log = bc + m_at[t][st]
            m_t = jnp.maximum(jnp.max(dlog, axis=1, keepdims=True), inter_log)
            ms_t.append((m_raw[t][j] * jnp.exp(dlog - m_t)).astype(BF16))
            mwi_t.append(jnp.exp(inter_log - m_t))
            mfl_t.append(jnp.exp(-m_t))
            kt = m_operands(spans, d, h)[1]
            mkw_t.append((kt.astype(F32) * e_t[cf:cf + 1, :]).astype(BF16))
            rs_t.append((r_raw[t][j] * decay_ref[st]).astype(BF16))
            rkt = r_operands(spans, d, h)[1]
            rkw_t.append((rkt.astype(F32) * zeta_ref[st:st + 1, :]).astype(BF16))
        m_s.append(ms_t), m_kw.append(mkw_t), m_winter.append(mwi_t), m_floor.append(mfl_t)
        r_s.append(rs_t), r_kw.append(rkw_t)

    c_cur = [c_ref[st] for st in range(2 * ML_HEADS)]
    r_cur = [r_ref[st] for st in range(2 * RET_HEADS)]
    for t, spans in enumerate(all_spans):
        v1s = [jnp.concatenate([m_operands(spans, d, h)[2], ones], axis=1) for d, h in items]
        intra = [jnp.dot(m_s[t][j], v1s[j], preferred_element_type=F32) for j in range(len(items))]
        inter = [jnp.dot(m_operands(spans, d, h)[0], c_cur[ML_HEADS * d + h].astype(BF16),
                         preferred_element_type=F32) for d, h in items]
        c_loc = [jnp.dot(m_kw[t][j], v1s[j], preferred_element_type=F32) for j in range(len(items))]
        ry_intra = [jnp.dot(r_s[t][j], r_operands(spans, d, h)[2], preferred_element_type=F32)
                    for j, (d, h) in enumerate(items)]
        ry_inter = [jnp.dot(r_operands(spans, d, h)[0], r_cur[RET_HEADS * d + h].astype(BF16),
                            preferred_element_type=F32) for d, h in items]
        r_loc = [jnp.dot(r_kw[t][j], r_operands(spans, d, h)[2], preferred_element_type=F32)
                 for j, (d, h) in enumerate(items)]
        for j, (d, h) in enumerate(items):
            st = ML_HEADS * d + h
            w_inter = m_winter[t][j]
            num = intra[j][:, :ML_DIM] + w_inter * inter[j][:, :ML_DIM]
            den = intra[j][:, ML_DIM:] + w_inter * inter[j][:, ML_DIM:]
            m_refs[d][3][spans[d], h * ML_DIM:(h + 1) * ML_DIM] = (
                num / jnp.maximum(jnp.abs(den), m_floor[t][j])).astype(BF16)
            keep, gain = keep_at[t][j], gain_at[t][j]
            c_cur[st] = (jnp.concatenate([keep, keep], axis=1) * c_cur[st]
                         + jnp.concatenate([gain, gain], axis=1) * c_loc[j])
            r_refs[d][3][spans[d], h * RET_V:(h + 1) * RET_V] = (
                ry_intra[j] + ry_inter[j] * inner_ref[st]).astype(BF16)
            r_cur[st] = chunk_decay[st] * r_cur[st] + r_loc[j]
    for st in range(2 * ML_HEADS):
        c_ref[st] = c_cur[st]
        r_ref[st] = r_cur[st]


def _gate_select_matrix():
    sel = np.zeros((2 * 3 * GATE_ROWS, 2 * ML_HEADS * LANE), np.float32)
    for d in range(2):
        for part in range(3):
            for h in range(ML_HEADS):
                st = ML_HEADS * d + h
                sel[(3 * d + part) * GATE_ROWS + 8 * d + 4 + h, st * LANE:(st + 1) * LANE] = 1.0
    return jnp.asarray(sel, BF16)


def _mixers(mq, mkt, rq, rkt, proj, gates_t, bias_t, layer, tri, sel, ret_consts, *, cps):
    S = proj.shape[0]
    rows = cps * CHUNK
    nb = S // rows
    MW = ML_HEADS * ML_DIM
    RQ = RET_HEADS * RET_QK
    RV = RET_HEADS * RET_V
    decay, inner, zeta, chunk_decay = ret_consts
    fwd = lambda n: n
    bwd = lambda n: nb - 1 - n

    def specs(idx):
        return [
            pl.BlockSpec((rows, MW), lambda n: (idx(n), 0)),
            pl.BlockSpec((MW, rows), lambda n: (0, idx(n))),
            pl.BlockSpec((rows, MW), lambda n: (idx(n), COL_ML_V * LANE // MW)),
            pl.BlockSpec((GATE_ROWS, rows), lambda n: (0, idx(n))),
            pl.BlockSpec((rows, RQ), lambda n: (idx(n), 0)),
            pl.BlockSpec((RQ, rows), lambda n: (0, idx(n))),
            pl.BlockSpec((rows, RV), lambda n: (idx(n), COL_RV * LANE // RV)),
        ]

    whole = lambda a: pl.BlockSpec(a.shape, lambda n: (0,) * a.ndim)
    out = lambda idx, w: pl.BlockSpec((rows, w), lambda n: (idx(n), 0))
    operands = (mq, mkt, proj, gates_t, rq, rkt, proj)
    return pl.pallas_call(
        functools.partial(_mixers_kernel, chunk_decay=chunk_decay, cps=cps),
        name="mixers",
        grid=(nb,),
        in_specs=specs(fwd) + specs(bwd) + [
            pl.BlockSpec((None, GATE_ROWS, CHUNK), lambda n: (layer, 0, 0)),
            whole(tri), whole(sel), whole(decay), whole(inner), whole(zeta),
        ],
        out_specs=[out(fwd, MW), out(bwd, MW), out(fwd, RV), out(bwd, RV)],
        out_shape=[jax.ShapeDtypeStruct((S, MW), BF16), jax.ShapeDtypeStruct((S, MW), BF16),
                   jax.ShapeDtypeStruct((S, RV), BF16), jax.ShapeDtypeStruct((S, RV), BF16)],
        scratch_shapes=[pltpu.VMEM((2 * ML_HEADS, ML_DIM, 2 * ML_DIM), F32), pltpu.VMEM((2 * ML_HEADS, LANE), F32),
                        pltpu.VMEM((2 * RET_HEADS, RET_QK, RET_V), F32)],
        compiler_params=_params(("arbitrary",)),
    )(*operands, *operands, bias_t, tri, sel, decay, inner, zeta)


def _retention_consts():
    log_gamma = jnp.log1p(-jnp.exp2(-RET_DECAY_BASE - jnp.arange(RET_HEADS, dtype=F32)))
    idx = jnp.arange(CHUNK, dtype=F32)
    diff = idx[:, None] - idx[None, :]
    decay, inner, zeta = [], [], []
    for d in range(2):
        lg_d = log_gamma if d == 0 else log_gamma[::-1]
        dd = diff if d == 0 else -diff
        pos = idx if d == 0 else (CHUNK - 1.0 - idx)
        keep = dd >= 0
        for h in range(RET_HEADS):
            lg = lg_d[h]
            decay.append(jnp.where(keep, jnp.exp(jnp.where(keep, dd, 0.0) * lg), 0.0))
            inner.append(jnp.broadcast_to(jnp.exp((pos + 1.0) * lg)[:, None], (CHUNK, LANE)))
            zeta.append(jnp.exp((CHUNK - 1.0 - pos) * lg))
    lg_np = np.log1p(-np.exp2(-RET_DECAY_BASE - np.arange(RET_HEADS, dtype=np.float32))).astype(np.float32)
    chunk_decay = tuple(float(np.exp(np.float32(CHUNK) * lg)) for lg in list(lg_np) + list(lg_np[::-1]))
    return jnp.stack(decay), jnp.stack(inner), jnp.stack(zeta), chunk_decay


def _mla_prep_kernel(cq_ref, ckv_ref, kr_ref, gq_ref, gkv_ref, wq_ref, wkv_ref, cos_ref, sin_ref,
                     q_out, k_out, v_out):
    tm = cq_ref.shape[0]
    cos4 = cos_ref[...]
    sin4 = sin_ref[...]
    qn = _rmsnorm(cq_ref[...].astype(F32), gq_ref[...]).astype(BF16)
    kvn = _rmsnorm(ckv_ref[...].astype(F32), gkv_ref[...]).astype(BF16)
    q_all = jnp.dot(qn, wq_ref[...], preferred_element_type=F32)
    kv_all = jnp.dot(kvn, wkv_ref[...], preferred_element_type=F32)
    q_scale = (MLA_QK ** -0.5) * math.log2(math.e)
    nope_w = MLA_HEADS * MLA_NOPE
    q_rope = _rope_grouped(q_all[:, nope_w:], jnp.concatenate([cos4, cos4], axis=1),
                           jnp.concatenate([sin4, sin4], axis=1)) * q_scale
    k_rope = _rope_grouped(kr_ref[...].astype(F32), cos4[:, :LANE], sin4[:, :LANE])[:, :ROPE_DIM].astype(BF16)
    for h in range(MLA_HEADS):
        q_out[h, :, :MLA_NOPE] = (q_all[:, h * MLA_NOPE:(h + 1) * MLA_NOPE] * q_scale).astype(BF16)
        q_out[h, :, MLA_NOPE:] = q_rope[:, h * ROPE_DIM:(h + 1) * ROPE_DIM].astype(BF16)
        kv0 = h * (MLA_NOPE + MLA_V)
        k_out[h, :, :MLA_NOPE] = kv_all[:, kv0:kv0 + MLA_NOPE].astype(BF16)
        k_out[h, :, MLA_NOPE:] = k_rope
        v_out[h] = kv_all[:, kv0 + MLA_NOPE:kv0 + MLA_NOPE + MLA_V].astype(BF16)


def _mla_prep(proj, g_q, g_kv, w_q, w_kv, layer, cos4, sin4, *, tm):
    S = proj.shape[0]
    q_rank, kv_rank = w_q.shape[1], w_kv.shape[1]
    return pl.pallas_call(
        _mla_prep_kernel,
        name="mla_prep",
        grid=(S // tm,),
        in_specs=[
            pl.BlockSpec((tm, q_rank), lambda i: (i, COL_CQ * LANE // q_rank)),
            pl.BlockSpec((tm, kv_rank), lambda i: (i, COL_CKV * LANE // kv_rank)),
            pl.BlockSpec((tm, LANE), lambda i: (i, COL_KROPE)),
            pl.BlockSpec((None, 1, q_rank), lambda i: (layer, 0, 0)),
            pl.BlockSpec((None, 1, kv_rank), lambda i: (layer, 0, 0)),
            pl.BlockSpec((None, q_rank, w_q.shape[2]), lambda i: (layer, 0, 0)),
            pl.BlockSpec((None, kv_rank, w_kv.shape[2]), lambda i: (layer, 0, 0)),
            pl.BlockSpec((tm, 2 * LANE), lambda i: (i, 0)),
            pl.BlockSpec((tm, 2 * LANE), lambda i: (i, 0)),
        ],
        out_specs=[
            pl.BlockSpec((MLA_HEADS, tm, MLA_QK), lambda i: (0, i, 0)),
            pl.BlockSpec((MLA_HEADS, tm, MLA_QK), lambda i: (0, i, 0)),
            pl.BlockSpec((MLA_HEADS, tm, MLA_V), lambda i: (0, i, 0)),
        ],
        out_shape=[
            jax.ShapeDtypeStruct((MLA_HEADS, S, MLA_QK), BF16),
            jax.ShapeDtypeStruct((MLA_HEADS, S, MLA_QK), BF16),
            jax.ShapeDtypeStruct((MLA_HEADS, S, MLA_V), BF16),
        ],
        compiler_params=_params(("parallel",)),
    )(proj, proj, proj, g_q, g_kv, w_q, w_kv, cos4, sin4)


def _attn_kernel(q_ref, k_ref, v_ref, *rest, tk, nk, n_cast):
    w_refs, o_ref, wb_refs, v1_ref = rest[:n_cast], rest[n_cast], rest[n_cast + 1:-1], rest[-1]
    for w_ref, wb_ref in zip(w_refs, wb_refs):
        wb_ref[...] = w_ref[...].astype(wb_ref.dtype)

    @pl.when(pl.program_id(1) == 0)
    def _():
        v1_ref[:, :MLA_V] = v_ref[...]
        v1_ref[:, MLA_V:] = jnp.ones(v_ref.shape, v1_ref.dtype)

    q = q_ref[...]
    m = acc = None
    for c in range(nk):
        k = k_ref[c * tk:(c + 1) * tk, :]
        v = v1_ref[c * tk:(c + 1) * tk, :]
        s = lax.dot_general(q, k, _NT, preferred_element_type=F32)
        m_c = jnp.max(s, axis=-1, keepdims=True)
        if c == 0:
            m = m_c
            acc = jnp.dot(jnp.exp2(s - m).astype(BF16), v, preferred_element_type=F32)
        else:
            m_new = jnp.maximum(m, m_c)
            acc = jnp.exp2(m - m_new) * acc + jnp.dot(jnp.exp2(s - m_new).astype(BF16), v, preferred_element_type=F32)
            m = m_new
    o_ref[...] = (acc[:, :MLA_V] / acc[:, MLA_V:]).astype(o_ref.dtype)


def _attention(q, k, v, cast_weights, layer, *, tq, tk):
    S = q.shape[1]
    nq = S // tq
    n_steps = MLA_HEADS * nq
    w_specs, wb_specs, wb_shapes = [], [], []
    for w in cast_weights:
        _, rows, cols = w.shape
        rb = rows // n_steps
        assert rb * n_steps == rows and rb % BF16_ROWS == 0
        w_specs.append(pl.BlockSpec((None, rb, cols), lambda h, i: (layer, h * nq + i, 0)))
        wb_specs.append(pl.BlockSpec((rb, cols), lambda h, i: (h * nq + i, 0)))
        wb_shapes.append(jax.ShapeDtypeStruct((rows, cols), BF16))
    outs = pl.pallas_call(
        functools.partial(_attn_kernel, tk=tk, nk=S // tk, n_cast=len(cast_weights)),
        name="attention",
        grid=(MLA_HEADS, nq),
        in_specs=[
            pl.BlockSpec((None, tq, MLA_QK), lambda h, i: (h, i, 0)),
            pl.BlockSpec((None, S, MLA_QK), lambda h, i: (h, 0, 0)),
            pl.BlockSpec((None, S, MLA_V), lambda h, i: (h, 0, 0)),
        ] + w_specs,
        out_specs=[pl.BlockSpec((tq, MLA_V), lambda h, i: (i, h))] + wb_specs,
        out_shape=[jax.ShapeDtypeStruct((S, MLA_HEADS * MLA_V), BF16)] + wb_shapes,
        scratch_shapes=[pltpu.VMEM((S, 2 * MLA_V), BF16)],
        compiler_params=_params(("arbitrary", "arbitrary")),
    )(q, k, v, *cast_weights)
    return outs[0], outs[1:]


def _pack_w_in_kernel(w_ref, o_ref):
    n_in, tc = w_ref.shape
    ml_end = 4 * ML_HEADS * ML_DIM
    gates = 4 * ML_HEADS
    rest = n_in - ml_end - gates
    gate_row = IN_WIDTH_PACKED - LANE
    assert ml_end + rest + (LANE - ROPE_DIM) == gate_row
    o_ref[:ml_end, :] = w_ref[:ml_end, :].astype(BF16)
    o_ref[ml_end:ml_end + rest, :] = w_ref[ml_end + gates:, :].astype(BF16)
    o_ref[ml_end + rest:gate_row, :] = jnp.zeros((LANE - ROPE_DIM, tc), BF16)
    o_ref[gate_row:gate_row + gates, :] = w_ref[ml_end:ml_end + gates, :].astype(BF16)
    o_ref[gate_row + gates:, :] = jnp.zeros((LANE - gates, tc), BF16)


def _pack_w_in(w_in_t, *, tc):
    L, n_in, K = w_in_t.shape
    return pl.pallas_call(
        _pack_w_in_kernel,
        name="pack_w_in",
        grid=(L, K // tc),
        in_specs=[pl.BlockSpec((None, n_in, tc), lambda l, i: (l, 0, i))],
        out_specs=pl.BlockSpec((None, IN_WIDTH_PACKED, tc), lambda l, i: (l, 0, i)),
        out_shape=jax.ShapeDtypeStruct((L, IN_WIDTH_PACKED, K), BF16),
        compiler_params=_params(("parallel", "parallel")),
    )(w_in_t)


def kernel(x, positions, g_mix, w_in, b_gates, w_conv, g_ml_out, g_ret_out, g_q_norm, w_q_up, g_kv_norm, w_kv_up,
           w_out, g_ffn, w_ff1, w_ff2, g_final):
    B, S, D = x.shape
    assert B == 1
    depth = w_in.shape[0]
    tm = min(512, S)
    tm_big = min(1024, S)

    half = ROPE_DIM // 2
    inv = ROPE_THETA ** (-jnp.arange(half, dtype=F32) / half)
    ang = positions[0].astype(F32)[:, None] * inv
    cos4 = jnp.tile(jnp.concatenate([jnp.cos(ang), jnp.cos(ang)], axis=-1), (1, RET_HEADS))
    sin4 = jnp.tile(jnp.concatenate([-jnp.sin(ang), jnp.sin(ang)], axis=-1), (1, RET_HEADS))

    w_in_p = _pack_w_in(jnp.swapaxes(w_in, 1, 2), tc=256)
    q_rank = w_q_up.shape[1]
    w_q4 = w_q_up.astype(BF16).reshape(depth, q_rank, MLA_HEADS, MLA_QK)
    w_q = jnp.concatenate([w_q4[..., :MLA_NOPE].reshape(depth, q_rank, MLA_HEADS * MLA_NOPE),
                           w_q4[..., MLA_NOPE:].reshape(depth, q_rank, MLA_HEADS * ROPE_DIM)], axis=-1)
    w_kv = w_kv_up.astype(BF16)
    bias_t = jnp.broadcast_to(b_gates[:, :, None], (depth, GATE_ROWS, CHUNK))
    idx = jnp.arange(CHUNK)
    tri = jnp.stack([idx[:, None] >= idx[None, :], idx[:, None] <= idx[None, :]]).astype(BF16)
    sel = _gate_select_matrix()
    ret_consts = _retention_consts()
    r3 = lambda g: g[:, None, :]

    xs = x[0]
    for l in range(depth):
        proj, gates_t = _in_proj(xs, r3(g_mix), w_in_p, l, tm=tm_big, tn=1536)
        mq, mkt, rq, rkt = _prep(proj, w_conv, l, cos4, sin4, tm=tm)
        h_f, h_b, y_f, y_b = _mixers(mq, mkt, rq, rkt, proj, gates_t, bias_t, l, tri, sel, ret_consts, cps=2)
        q, k, v = _mla_prep(proj, r3(g_q_norm), r3(g_kv_norm), w_q, w_kv, l, cos4, sin4, tm=tm)
        y_mla, (w_out_b, w_ff1_b, w_ff2_b) = _attention(q, k, v, (w_out, w_ff1, w_ff2), l,
                                                        tq=min(2048, S), tk=min(256, S))
        xs = _out_proj(h_f, h_b, y_f, y_b, proj, y_mla, r3(g_ml_out), r3(g_ret_out), l, w_out_b, xs, tm=tm)
        act = _ffn1(xs, r3(g_ffn), w_ff1_b, l, tm=tm_big, tn=2048)
        xs = _ffn2(act, w_ff2_b, xs, tm=tm_big, tn=1024, tk=4096)
    return _final_rms(xs, g_final[None, :], tm=tm)[None]
```

```python
import functools
import math

import numpy as np
import jax
import jax.numpy as jnp
from jax import lax
from jax.experimental import pallas as pl
from jax.experimental.pallas import tpu as pltpu

F32 = jnp.float32
BF16 = jnp.bfloat16

EPS = 1e-6
NEG_INIT = -1e30
ROPE_THETA = 10000.0
CHUNK = 128
LANE = 128
BF16_ROWS = 16
ML_HEADS = 4
ML_DIM = 128
RET_HEADS = 4
RET_QK = 64
RET_V = 128
RET_DECAY_BASE = 5.0
MLA_HEADS = 8
MLA_NOPE = 128
ROPE_DIM = 64
MLA_V = 128
MLA_QK = MLA_NOPE + ROPE_DIM
VMEM_LIMIT = 56 * 1024 * 1024

COL_ML_Q, COL_ML_K, COL_ML_V, COL_ML_O = 0, 4, 8, 12
COL_RQK, COL_RV, COL_RG = 16, 20, 24
COL_CQ, COL_CKV, COL_KROPE, COL_GATES = 28, 32, 34, 35
IN_WIDTH_PACKED = 36 * LANE

_NT = (((1,), (1,)), ((), ()))
_TN = (((0,), (0,)), ((), ()))


def _params(sem):
    return pltpu.CompilerParams(dimension_semantics=sem, vmem_limit_bytes=VMEM_LIMIT)


def _rmsnorm(x, g):
    return x * lax.rsqrt(jnp.mean(x * x, axis=-1, keepdims=True) + EPS) * g


def _in_proj_kernel(x_ref, g_ref, w_ref, o_ref, gates_ref, hn_ref):
    j = pl.program_id(1)

    @pl.when(j == 0)
    def _():
        hn_ref[...] = _rmsnorm(x_ref[...], g_ref[...]).astype(hn_ref.dtype)

    y = lax.dot_general(hn_ref[...], w_ref[...], _NT, preferred_element_type=F32)
    o_ref[...] = y.astype(o_ref.dtype)

    @pl.when(j == pl.num_programs(1) - 1)
    def _():
        gates_ref[...] = y[:, y.shape[1] - LANE:].T


def _in_proj(x, g, w, layer, *, tm, tn):
    S, K = x.shape
    N = w.shape[1]
    return pl.pallas_call(
        _in_proj_kernel,
        name="in_proj",
        grid=(S // tm, N // tn),
        in_specs=[
            pl.BlockSpec((tm, K), lambda i, j: (i, 0)),
            pl.BlockSpec((None, 1, K), lambda i, j: (layer, 0, 0)),
            pl.BlockSpec((None, tn, K), lambda i, j: (layer, j, 0)),
        ],
        out_specs=[pl.BlockSpec((tm, tn), lambda i, j: (i, j)), pl.BlockSpec((LANE, tm), lambda i, j: (0, i))],
        out_shape=[jax.ShapeDtypeStruct((S, N), BF16), jax.ShapeDtypeStruct((LANE, S), F32)],
        scratch_shapes=[pltpu.VMEM((tm, K), BF16)],
        compiler_params=_params(("parallel", "arbitrary")),
    )(x, g, w)


def _out_proj_kernel(hf, hb, o_ref, gml_ref, yf, yb, rg_ref, gret_ref, ymla_ref, w_ref, x_ref, out_ref):
    w_ml = ML_HEADS * ML_DIM
    w_ret = RET_HEADS * RET_V
    acc = x_ref[...] + jnp.dot(ymla_ref[...], w_ref[w_ml + w_ret:, :], preferred_element_type=F32)
    y_ml, y_ret = [], []
    for h in range(ML_HEADS):
        sl = slice(h * ML_DIM, (h + 1) * ML_DIM)
        y = _rmsnorm(hf[:, sl].astype(F32) + hb[:, sl].astype(F32), gml_ref[:, sl])
        y_ml.append((y / (1.0 + jnp.exp(-o_ref[:, sl].astype(F32)))).astype(BF16))
    for h in range(RET_HEADS):
        sl = slice(h * RET_V, (h + 1) * RET_V)
        y = _rmsnorm(yf[:, sl].astype(F32) + yb[:, sl].astype(F32), gret_ref[:, sl])
        gate = rg_ref[:, sl].astype(F32)
        y_ret.append((gate / (1.0 + jnp.exp(-gate)) * y).astype(BF16))
    acc = acc + jnp.dot(jnp.concatenate(y_ml, axis=1), w_ref[:w_ml, :], preferred_element_type=F32)
    acc = acc + jnp.dot(jnp.concatenate(y_ret, axis=1), w_ref[w_ml:w_ml + w_ret, :], preferred_element_type=F32)
    out_ref[...] = acc


def _out_proj(h_f, h_b, y_f, y_b, proj, y_mla, g_ml, g_ret, layer, w, x, *, tm):
    S, N = x.shape
    W = ML_HEADS * ML_DIM
    blk = lambda c: pl.BlockSpec((tm, W), lambda i: (i, c))
    gain = pl.BlockSpec((None, 1, W), lambda i: (layer, 0, 0))
    return pl.pallas_call(
        _out_proj_kernel,
        name="out_proj",
        grid=(S // tm,),
        in_specs=[blk(0), blk(0), blk(COL_ML_O * LANE // W), gain, blk(0), blk(0), blk(COL_RG * LANE // W), gain,
                  pl.BlockSpec((tm, y_mla.shape[1]), lambda i: (i, 0)),
                  pl.BlockSpec(w.shape, lambda i: (0, 0)),
                  pl.BlockSpec((tm, N), lambda i: (i, 0))],
        out_specs=pl.BlockSpec((tm, N), lambda i: (i, 0)),
        out_shape=jax.ShapeDtypeStruct((S, N), F32),
        compiler_params=_params(("parallel",)),
    )(h_f, h_b, proj, g_ml, y_f, y_b, proj, g_ret, y_mla, w, x)


def _ffn1_kernel(x_ref, g_ref, w_ref, o_ref, hn_ref):
    @pl.when(pl.program_id(1) == 0)
    def _():
        hn_ref[...] = _rmsnorm(x_ref[...], g_ref[...]).astype(hn_ref.dtype)

    y = jnp.dot(hn_ref[...], w_ref[...], preferred_element_type=F32)
    o_ref[...] = jnp.square(jnp.maximum(y, 0.0)).astype(o_ref.dtype)


def _ffn1(x, g, w, layer, *, tm, tn):
    S, K = x.shape
    N = w.shape[-1]
    return pl.pallas_call(
        _ffn1_kernel,
        name="ffn1",
        grid=(S // tm, N // tn),
        in_specs=[
            pl.BlockSpec((tm, K), lambda i, j: (i, 0)),
            pl.BlockSpec((None, 1, K), lambda i, j: (layer, 0, 0)),
            pl.BlockSpec((K, tn), lambda i, j: (0, j)),
        ],
        out_specs=pl.BlockSpec((tm, tn), lambda i, j: (i, j)),
        out_shape=jax.ShapeDtypeStruct((S, N), BF16),
        scratch_shapes=[pltpu.VMEM((tm, K), BF16)],
        compiler_params=_params(("parallel", "arbitrary")),
    )(x, g, w)


def _ffn2_kernel(a_ref, w_ref, x_ref, o_ref):
    @pl.when(pl.program_id(2) == 0)
    def _():
        o_ref[...] = x_ref[...]

    o_ref[...] += jnp.dot(a_ref[...], w_ref[...], preferred_element_type=F32)


def _ffn2(a, w, x, *, tm, tn, tk):
    S, N = x.shape
    K = a.shape[1]
    return pl.pallas_call(
        _ffn2_kernel,
        name="ffn2",
        grid=(S // tm, N // tn, K // tk),
        in_specs=[
            pl.BlockSpec((tm, tk), lambda i, j, k: (i, k)),
            pl.BlockSpec((tk, tn), lambda i, j, k: (k, j)),
            pl.BlockSpec((tm, tn), lambda i, j, k: (i, j)),
        ],
        out_specs=pl.BlockSpec((tm, tn), lambda i, j, k: (i, j)),
        out_shape=jax.ShapeDtypeStruct((S, N), F32),
        compiler_params=_params(("parallel", "arbitrary", "arbitrary")),
    )(a, w, x)


def _final_rms_kernel(x_ref, g_ref, o_ref):
    o_ref[...] = _rmsnorm(x_ref[...], g_ref[...])


def _final_rms(x, g, *, tm):
    S, K = x.shape
    return pl.pallas_call(
        _final_rms_kernel,
        name="final_rms",
        grid=(S // tm,),
        in_specs=[pl.BlockSpec((tm, K), lambda i: (i, 0)), pl.BlockSpec((1, K), lambda i: (0, 0))],
        out_specs=pl.BlockSpec((tm, K), lambda i: (i, 0)),
        out_shape=jax.ShapeDtypeStruct((S, K), F32),
        compiler_params=_params(("parallel",)),
    )(x, g)


def _rope_grouped(t, cosf, sinf):
    width = t.shape[1]
    lane = lax.broadcasted_iota(jnp.int32, t.shape, 1)
    first_half = (lane % ROPE_DIM) < (ROPE_DIM // 2)
    partner = jnp.where(first_half, pltpu.roll(t, width - ROPE_DIM // 2, 1), pltpu.roll(t, ROPE_DIM // 2, 1))
    return t * cosf + partner * sinf


def _prep_kernel(qk_ref, prev_ref, next_ref, wc_ref, rqk_ref, cos_ref, sin_ref, mq_out, mkt_out, rq_out, rkt_out,
                 *, tm, nblk):
    i = pl.program_id(0)
    x = qk_ref[...].astype(F32)
    row = lax.broadcasted_iota(jnp.int32, x.shape, 0)
    prev_row = jnp.where(i > 0, prev_ref[BF16_ROWS - 1:BF16_ROWS, :].astype(F32), 0.0)
    next_row = jnp.where(i < nblk - 1, next_ref[0:1, :].astype(F32), 0.0)
    x_m = jnp.where(row == 0, prev_row, pltpu.roll(x, 1, 0))
    x_p = jnp.where(row == tm - 1, next_row, pltpu.roll(x, tm - 1, 0))
    w = wc_ref[...]
    y = x_m * w[0:1, :] + x * w[1:2, :] + x_p * w[2:3, :]
    y = y / (1.0 + jnp.exp(-y))
    half = ML_HEADS * ML_DIM
    mq_out[...] = (y[:, :half] * (ML_DIM ** -0.5)).astype(mq_out.dtype)
    mkt_out[...] = y[:, half:].T.astype(mkt_out.dtype)

    cosf = cos_ref[...]
    sinf = sin_ref[...]
    width = RET_HEADS * RET_QK
    r = rqk_ref[...].astype(F32)
    rq_out[...] = _rope_grouped(r[:, :width], cosf, sinf).astype(rq_out.dtype)
    rkt_out[...] = (_rope_grouped(r[:, width:], cosf, sinf) * (RET_QK ** -0.5)).T.astype(rkt_out.dtype)


def _prep(proj, w_conv, layer, cos4, sin4, *, tm):
    S = proj.shape[0]
    nblk = S // tm
    qk_w = 2 * ML_HEADS * ML_DIM
    rqk_w = 2 * RET_HEADS * RET_QK
    hb = tm // BF16_ROWS
    return pl.pallas_call(
        functools.partial(_prep_kernel, tm=tm, nblk=nblk),
        name="prep",
        grid=(nblk,),
        in_specs=[
            pl.BlockSpec((tm, qk_w), lambda i: (i, 0)),
            pl.BlockSpec((BF16_ROWS, qk_w), lambda i: (jnp.maximum(i * hb - 1, 0), 0)),
            pl.BlockSpec((BF16_ROWS, qk_w), lambda i: (jnp.minimum((i + 1) * hb, S // BF16_ROWS - 1), 0)),
            pl.BlockSpec((None, 3, qk_w), lambda i: (layer, 0, 0)),
            pl.BlockSpec((tm, rqk_w), lambda i: (i, COL_RQK * LANE // rqk_w)),
            pl.BlockSpec((tm, rqk_w // 2), lambda i: (i, 0)),
            pl.BlockSpec((tm, rqk_w // 2), lambda i: (i, 0)),
        ],
        out_specs=[
            pl.BlockSpec((tm, qk_w // 2), lambda i: (i, 0)), pl.BlockSpec((qk_w // 2, tm), lambda i: (0, i)),
            pl.BlockSpec((tm, rqk_w // 2), lambda i: (i, 0)), pl.BlockSpec((rqk_w // 2, tm), lambda i: (0, i)),
        ],
        out_shape=[
            jax.ShapeDtypeStruct((S, qk_w // 2), BF16), jax.ShapeDtypeStruct((qk_w // 2, S), BF16),
            jax.ShapeDtypeStruct((S, rqk_w // 2), BF16), jax.ShapeDtypeStruct((rqk_w // 2, S), BF16),
        ],
        compiler_params=_params(("parallel",)),
    )(proj, proj, proj, w_conv, proj, cos4, sin4)


def _log_sigmoid(t):
    return jnp.minimum(t, 0.0) - jnp.log1p(jnp.exp(-jnp.abs(t)))


def _split3(t):
    hi = t.astype(BF16)
    r1 = t - hi.astype(F32)
    mid = r1.astype(BF16)
    lo = (r1 - mid.astype(F32)).astype(BF16)
    return hi, mid, lo


GATE_ROWS = 16


def _mlstm_cumsum(g_t, tri):
    logf_t = _log_sigmoid(g_t)
    return logf_t, jnp.dot(jnp.concatenate(_split3(logf_t), axis=0), tri, preferred_element_type=F32)


def _mlstm_gates(g_t, logf_t, cs):
    b_t = cs[0:GATE_ROWS] + cs[GATE_ROWS:2 * GATE_ROWS] + cs[2 * GATE_ROWS:]
    tot_t = jnp.sum(logf_t, axis=1, keepdims=True)
    r_t = pltpu.roll(g_t, 4, 0) - b_t
    wend_t = tot_t + r_t
    mloc_t = jnp.max(wend_t, axis=1, keepdims=True)
    e_t = jnp.exp(wend_t - mloc_t)
    return b_t, r_t, tot_t, mloc_t, e_t


def _mixers_kernel(mqf, mktf, mvf, gf, rqf, rktf, rvf, mqb, mktb, mvb, gb, rqb, rktb, rvb,
                   bias_ref, tri_ref, sel_ref, decay_ref, inner_ref, zeta_ref,
                   cq_ref, ckv_ref, kr_ref, gq_ref, gkv_ref, wq_ref, wkv_ref, cos_ref, sin_ref,
                   hf_ref, hb_ref, yf_ref, yb_ref, q_out, k_out, v_out, c_ref, m_ref, r_ref, *, chunk_decay, cps):
    mla_refs = (cq_ref, ckv_ref, kr_ref, gq_ref, gkv_ref, wq_ref, wkv_ref, cos_ref, sin_ref, q_out, k_out, v_out)

    @pl.when(pl.program_id(0) == 0)
    def _():
        c_ref[...] = jnp.zeros(c_ref.shape, F32)
        m_ref[...] = jnp.full(m_ref.shape, NEG_INIT, F32)
        r_ref[...] = jnp.zeros(r_ref.shape, F32)

    L = CHUNK
    row = lax.broadcasted_iota(jnp.int32, (L, L), 0)
    col = lax.broadcasted_iota(jnp.int32, (L, L), 1)
    ones = jnp.ones((L, ML_DIM), BF16)
    bias = bias_ref[...]

    all_spans = [[slice(sub * L, (sub + 1) * L) for sub in (t, cps - 1 - t)] for t in range(cps)]

    items = [(d, h) for d in range(2) for h in range(ML_HEADS)]
    m_refs = ((mqf, mktf, mvf, hf_ref), (mqb, mktb, mvb, hb_ref))
    r_refs = ((rqf, rktf, rvf, yf_ref), (rqb, rktb, rvb, yb_ref))

    def m_operands(spans, d, h):
        q_ref, kt_ref, v_ref, _ = m_refs[d]
        sl = slice(h * ML_DIM, (h + 1) * ML_DIM)
        return q_ref[spans[d], sl], kt_ref[sl, spans[d]], v_ref[spans[d], sl]

    def r_operands(spans, d, h):
        q_ref, kt_ref, v_ref, _ = r_refs[d]
        return (q_ref[spans[d], h * RET_QK:(h + 1) * RET_QK], kt_ref[h * RET_QK:(h + 1) * RET_QK, spans[d]],
                v_ref[spans[d], h * RET_V:(h + 1) * RET_V])

    g_ts = [[g_ref[:, spans[d]] + bias for d, g_ref in enumerate((gf, gb))] for spans in all_spans]
    cums = [[_mlstm_cumsum(g_ts[t][d], tri_ref[1 - d]) for d in range(2)] for t in range(cps)]

    m_raw = [[jnp.dot(*m_operands(spans, d, h)[:2], preferred_element_type=F32) for d, h in items]
             for spans in all_spans]
    r_raw = [[jnp.dot(*r_operands(spans, d, h)[:2], preferred_element_type=F32) for d, h in items]
             for spans in all_spans]

    all_gates, all_bcast = [], []
    for t in range(cps):
        gates = [_mlstm_gates(g_ts[t][d], *cums[t][d]) for d in range(2)]
        x_rows = jnp.concatenate([part for g in gates for part in _split3(g[0])], axis=0)
        all_gates.append(gates)
        all_bcast.append(lax.dot_general(x_rows, sel_ref[...], _TN, preferred_element_type=F32))

    _mla_project(*mla_refs)

    m_cur = [m_ref[st:st + 1, :] for st in range(2 * ML_HEADS)]
    m_at, keep_at, gain_at = [], [], []
    for gates in all_gates:
        m_at.append(list(m_cur))
        keeps, gains = [], []
        for d, h in items:
            cf, st = 8 * d + 4 + h, ML_HEADS * d + h
            tot_t, mloc_t = gates[d][2], gates[d][3]
            blast = jnp.broadcast_to(tot_t[cf:cf + 1, :], (1, LANE))
            m_loc = jnp.broadcast_to(mloc_t[cf:cf + 1, :], (1, LANE))
            m_new = jnp.maximum(blast + m_cur[st], m_loc)
            keeps.append(jnp.exp(blast + m_cur[st] - m_new))
            gains.append(jnp.exp(m_loc - m_new))
            m_cur[st] = m_new
        keep_at.append(keeps)
        gain_at.append(gains)
    for st in range(2 * ML_HEADS):
        m_ref[st:st + 1, :] = m_cur[st]

    m_s, m_kw, m_winter, m_floor, r_s, r_kw = [], [], [], [], [], []
    for t, spans in enumerate(all_spans):
        ms_t, mkw_t, mwi_t, mfl_t, rs_t, rkw_t = [], [], [], [], [], []
        for j, (d, h) in enumerate(items):
            cf, st = 8 * d + 4 + h, ML_HEADS * d + h
            _, r_t, _, _, e_t = all_gates[t][d]
            mask = (row >= col) if d == 0 else (row <= col)
            bc = all_bcast[t][:, st * LANE:(st + 1) * LANE]
            dlog = jnp.where(mask, bc + r_t[cf:cf + 1, :], -jnp.inf)
            inter_log = bc + m_at[t][st]
            m_t = jnp.maximum(jnp.max(dlog, axis=1, keepdims=True), inter_log)
            ms_t.append((m_raw[t][j] * jnp.exp(dlog - m_t)).astype(BF16))
            mwi_t.append(jnp.exp(inter_log - m_t))
            mfl_t.append(jnp.exp(-m_t))
            kt = m_operands(spans, d, h)[1]
            mkw_t.append((kt.astype(F32) * e_t[cf:cf + 1, :]).astype(BF16))
            rs_t.append((r_raw[t][j] * decay_ref[st]).astype(BF16))
            rkt = r_operands(spans, d, h)[1]
            rkw_t.append((rkt.astype(F32) * zeta_ref[st:st + 1, :]).astype(BF16))
        m_s.append(ms_t), m_kw.append(mkw_t), m_winter.append(mwi_t), m_floor.append(mfl_t)
        r_s.append(rs_t), r_kw.append(rkw_t)

    c_cur = [c_ref[st] for st in range(2 * ML_HEADS)]
    r_cur = [r_ref[st] for st in range(2 * RET_HEADS)]
    for t, spans in enumerate(all_spans):
        v1s = [jnp.concatenate([m_operands(spans, d, h)[2], ones], axis=1) for d, h in items]
        intra = [jnp.dot(m_s[t][j], v1s[j], preferred_element_type=F32) for j in range(len(items))]
        inter = [jnp.dot(m_operands(spans, d, h)[0], c_cur[ML_HEADS * d + h].astype(BF16),
                         preferred_element_type=F32) for d, h in items]
        c_loc = [jnp.dot(m_kw[t][j], v1s[j], preferred_element_type=F32) for j in range(len(items))]
        ry_intra = [jnp.dot(r_s[t][j], r_operands(spans, d, h)[2], preferred_element_type=F32)
                    for j, (d, h) in enumerate(items)]
        ry_inter = [jnp.dot(r_operands(spans, d, h)[0], r_cur[RET_HEADS * d + h].astype(BF16),
                            preferred_element_type=F32) for d, h in items]
        r_loc = [jnp.dot(r_kw[t][j], r_operands(spans, d, h)[2], preferred_element_type=F32)
                 for j, (d, h) in enumerate(items)]
        for j, (d, h) in enumerate(items):
            st = ML_HEADS * d + h
            w_inter = m_winter[t][j]
            num = intra[j][:, :ML_DIM] + w_inter * inter[j][:, :ML_DIM]
            den = intra[j][:, ML_DIM:] + w_inter * inter[j][:, ML_DIM:]
            m_refs[d][3][spans[d], h * ML_DIM:(h + 1) * ML_DIM] = (
                num / jnp.maximum(jnp.abs(den), m_floor[t][j])).astype(BF16)
            keep, gain = keep_at[t][j], gain_at[t][j]
            c_cur[st] = (jnp.concatenate([keep, keep], axis=1) * c_cur[st]
                         + jnp.concatenate([gain, gain], axis=1) * c_loc[j])
            r_refs[d][3][spans[d], h * RET_V:(h + 1) * RET_V] = (
                ry_intra[j] + ry_inter[j] * inner_ref[st]).astype(BF16)
            r_cur[st] = chunk_decay[st] * r_cur[st] + r_loc[j]
    for st in range(2 * ML_HEADS):
        c_ref[st] = c_cur[st]
        r_ref[st] = r_cur[st]


def _gate_select_matrix():
    sel = np.zeros((2 * 3 * GATE_ROWS, 2 * ML_HEADS * LANE), np.float32)
    for d in range(2):
        for part in range(3):
            for h in range(ML_HEADS):
                st = ML_HEADS * d + h
                sel[(3 * d + part) * GATE_ROWS + 8 * d + 4 + h, st * LANE:(st + 1) * LANE] = 1.0
    return jnp.asarray(sel, BF16)


def _mixers(mq, mkt, rq, rkt, proj, gates_t, bias_t, layer, tri, sel, ret_consts, g_q, g_kv, w_q, w_kv, cos4, sin4,
            *, cps):
    S = proj.shape[0]
    q_rank, kv_rank = w_q.shape[1], w_kv.shape[1]
    rows = cps * CHUNK
    nb = S // rows
    MW = ML_HEADS * ML_DIM
    RQ = RET_HEADS * RET_QK
    RV = RET_HEADS * RET_V
    decay, inner, zeta, chunk_decay = ret_consts
    fwd = lambda n: n
    bwd = lambda n: nb - 1 - n

    def specs(idx):
        return [
            pl.BlockSpec((rows, MW), lambda n: (idx(n), 0)),
            pl.BlockSpec((MW, rows), lambda n: (0, idx(n))),
            pl.BlockSpec((rows, MW), lambda n: (idx(n), COL_ML_V * LANE // MW)),
            pl.BlockSpec((GATE_ROWS, rows), lambda n: (0, idx(n))),
            pl.BlockSpec((rows, RQ), lambda n: (idx(n), 0)),
            pl.BlockSpec((RQ, rows), lambda n: (0, idx(n))),
            pl.BlockSpec((rows, RV), lambda n: (idx(n), COL_RV * LANE // RV)),
        ]

    whole = lambda a: pl.BlockSpec(a.shape, lambda n: (0,) * a.ndim)
    out = lambda idx, w: pl.BlockSpec((rows, w), lambda n: (idx(n), 0))
    operands = (mq, mkt, proj, gates_t, rq, rkt, proj)
    return pl.pallas_call(
        functools.partial(_mixers_kernel, chunk_decay=chunk_decay, cps=cps),
        name="mixers",
        grid=(nb,),
        in_specs=specs(fwd) + specs(bwd) + [
            pl.BlockSpec((None, GATE_ROWS, CHUNK), lambda n: (layer, 0, 0)),
            whole(tri), whole(sel), whole(decay), whole(inner), whole(zeta),
            pl.BlockSpec((rows, q_rank), lambda n: (n, COL_CQ * LANE // q_rank)),
            pl.BlockSpec((rows, kv_rank), lambda n: (n, COL_CKV * LANE // kv_rank)),
            pl.BlockSpec((rows, LANE), lambda n: (n, COL_KROPE)),
            pl.BlockSpec((None, 1, q_rank), lambda n: (layer, 0, 0)),
            pl.BlockSpec((None, 1, kv_rank), lambda n: (layer, 0, 0)),
            pl.BlockSpec((None, q_rank, w_q.shape[2]), lambda n: (layer, 0, 0)),
            pl.BlockSpec((None, kv_rank, w_kv.shape[2]), lambda n: (layer, 0, 0)),
            pl.BlockSpec((rows, 2 * LANE), lambda n: (n, 0)),
            pl.BlockSpec((rows, 2 * LANE), lambda n: (n, 0)),
        ],
        out_specs=[out(fwd, MW), out(bwd, MW), out(fwd, RV), out(bwd, RV),
                   pl.BlockSpec((MLA_HEADS, rows, MLA_QK), lambda n: (0, n, 0)),
                   pl.BlockSpec((MLA_HEADS, rows, MLA_QK), lambda n: (0, n, 0)),
                   pl.BlockSpec((MLA_HEADS, rows, MLA_V), lambda n: (0, n, 0))],
        out_shape=[jax.ShapeDtypeStruct((S, MW), BF16), jax.ShapeDtypeStruct((S, MW), BF16),
                   jax.ShapeDtypeStruct((S, RV), BF16), jax.ShapeDtypeStruct((S, RV), BF16),
                   jax.ShapeDtypeStruct((MLA_HEADS, S, MLA_QK), BF16),
                   jax.ShapeDtypeStruct((MLA_HEADS, S, MLA_QK), BF16),
                   jax.ShapeDtypeStruct((MLA_HEADS, S, MLA_V), BF16)],
        scratch_shapes=[pltpu.VMEM((2 * ML_HEADS, ML_DIM, 2 * ML_DIM), F32), pltpu.VMEM((2 * ML_HEADS, LANE), F32),
                        pltpu.VMEM((2 * RET_HEADS, RET_QK, RET_V), F32)],
        compiler_params=_params(("arbitrary",)),
    )(*operands, *operands, bias_t, tri, sel, decay, inner, zeta, proj, proj, proj, g_q, g_kv, w_q, w_kv, cos4, sin4)


def _retention_consts():
    log_gamma = jnp.log1p(-jnp.exp2(-RET_DECAY_BASE - jnp.arange(RET_HEADS, dtype=F32)))
    idx = jnp.arange(CHUNK, dtype=F32)
    diff = idx[:, None] - idx[None, :]
    decay, inner, zeta = [], [], []
    for d in range(2):
        lg_d = log_gamma if d == 0 else log_gamma[::-1]
        dd = diff if d == 0 else -diff
        pos = idx if d == 0 else (CHUNK - 1.0 - idx)
        keep = dd >= 0
        for h in range(RET_HEADS):
            lg = lg_d[h]
            decay.append(jnp.where(keep, jnp.exp(jnp.where(keep, dd, 0.0) * lg), 0.0))
            inner.append(jnp.broadcast_to(jnp.exp((pos + 1.0) * lg)[:, None], (CHUNK, LANE)))
            zeta.append(jnp.exp((CHUNK - 1.0 - pos) * lg))
    lg_np = np.log1p(-np.exp2(-RET_DECAY_BASE - np.arange(RET_HEADS, dtype=np.float32))).astype(np.float32)
    chunk_decay = tuple(float(np.exp(np.float32(CHUNK) * lg)) for lg in list(lg_np) + list(lg_np[::-1]))
    return jnp.stack(decay), jnp.stack(inner), jnp.stack(zeta), chunk_decay


def _mla_project(cq_ref, ckv_ref, kr_ref, gq_ref, gkv_ref, wq_ref, wkv_ref, cos_ref, sin_ref,
                 q_out, k_out, v_out):
    tm = cq_ref.shape[0]
    cos4 = cos_ref[...]
    sin4 = sin_ref[...]
    qn = _rmsnorm(cq_ref[...].astype(F32), gq_ref[...]).astype(BF16)
    kvn = _rmsnorm(ckv_ref[...].astype(F32), gkv_ref[...]).astype(BF16)
    q_all = jnp.dot(qn, wq_ref[...], preferred_element_type=F32)
    kv_all = jnp.dot(kvn, wkv_ref[...], preferred_element_type=F32)
    q_scale = (MLA_QK ** -0.5) * math.log2(math.e)
    nope_w = MLA_HEADS * MLA_NOPE
    q_rope = _rope_grouped(q_all[:, nope_w:], jnp.concatenate([cos4, cos4], axis=1),
                           jnp.concatenate([sin4, sin4], axis=1)) * q_scale
    k_rope = _rope_grouped(kr_ref[...].astype(F32), cos4[:, :LANE], sin4[:, :LANE])[:, :ROPE_DIM].astype(BF16)
    for h in range(MLA_HEADS):
        q_out[h, :, :MLA_NOPE] = (q_all[:, h * MLA_NOPE:(h + 1) * MLA_NOPE] * q_scale).astype(BF16)
        q_out[h, :, MLA_NOPE:] = q_rope[:, h * ROPE_DIM:(h + 1) * ROPE_DIM].astype(BF16)
        kv0 = h * (MLA_NOPE + MLA_V)
        k_out[h, :, :MLA_NOPE] = kv_all[:, kv0:kv0 + MLA_NOPE].astype(BF16)
        k_out[h, :, MLA_NOPE:] = k_rope
        v_out[h] = kv_all[:, kv0 + MLA_NOPE:kv0 + MLA_NOPE + MLA_V].astype(BF16)


def _attn_kernel(q_ref, k_ref, v_ref, *rest, tk, nk, n_cast):
    w_refs, o_ref, wb_refs, v1_ref = rest[:n_cast], rest[n_cast], rest[n_cast + 1:-1], rest[-1]
    for w_ref, wb_ref in zip(w_refs, wb_refs):
        wb_ref[...] = w_ref[...].astype(wb_ref.dtype)

    @pl.when(pl.program_id(1) == 0)
    def _():
        v1_ref[:, :MLA_V] = v_ref[...]
        v1_ref[:, MLA_V:] = jnp.ones(v_ref.shape, v1_ref.dtype)

    q = q_ref[...]
    m = acc = None
    for c in range(nk):
        k = k_ref[c * tk:(c + 1) * tk, :]
        v = v1_ref[c * tk:(c + 1) * tk, :]
        s = lax.dot_general(q, k, _NT, preferred_element_type=F32)
        m_c = jnp.max(s, axis=-1, keepdims=True)
        if c == 0:
            m = m_c
            acc = jnp.dot(jnp.exp2(s - m).astype(BF16), v, preferred_element_type=F32)
        else:
            m_new = jnp.maximum(m, m_c)
            acc = jnp.exp2(m - m_new) * acc + jnp.dot(jnp.exp2(s - m_new).astype(BF16), v, preferred_element_type=F32)
            m = m_new
    o_ref[...] = (acc[:, :MLA_V] / acc[:, MLA_V:]).astype(o_ref.dtype)


def _attention(q, k, v, cast_weights, layer, *, tq, tk):
    S = q.shape[1]
    nq = S // tq
    n_steps = MLA_HEADS * nq
    w_specs, wb_specs, wb_shapes = [], [], []
    for w in cast_weights:
        _, rows, cols = w.shape
        rb = rows // n_steps
        assert rb * n_steps == rows and rb % BF16_ROWS == 0
        w_specs.append(pl.BlockSpec((None, rb, cols), lambda h, i: (layer, h * nq + i, 0)))
        wb_specs.append(pl.BlockSpec((rb, cols), lambda h, i: (h * nq + i, 0)))
        wb_shapes.append(jax.ShapeDtypeStruct((rows, cols), BF16))
    outs = pl.pallas_call(
        functools.partial(_attn_kernel, tk=tk, nk=S // tk, n_cast=len(cast_weights)),
        name="attention",
        grid=(MLA_HEADS, nq),
        in_specs=[
            pl.BlockSpec((None, tq, MLA_QK), lambda h, i: (h, i, 0)),
            pl.BlockSpec((None, S, MLA_QK), lambda h, i: (h, 0, 0)),
            pl.BlockSpec((None, S, MLA_V), lambda h, i: (h, 0, 0)),
        ] + w_specs,
        out_specs=[pl.BlockSpec((tq, MLA_V), lambda h, i: (i, h))] + wb_specs,
        out_shape=[jax.ShapeDtypeStruct((S, MLA_HEADS * MLA_V), BF16)] + wb_shapes,
        scratch_shapes=[pltpu.VMEM((S, 2 * MLA_V), BF16)],
        compiler_params=_params(("arbitrary", "arbitrary")),
    )(q, k, v, *cast_weights)
    return outs[0], outs[1:]


def _pack_w_in_kernel(w_ref, o_ref):
    n_in, tc = w_ref.shape
    ml_end = 4 * ML_HEADS * ML_DIM
    gates = 4 * ML_HEADS
    rest = n_in - ml_end - gates
    gate_row = IN_WIDTH_PACKED - LANE
    assert ml_end + rest + (LANE - ROPE_DIM) == gate_row
    o_ref[:ml_end, :] = w_ref[:ml_end, :].astype(BF16)
    o_ref[ml_end:ml_end + rest, :] = w_ref[ml_end + gates:, :].astype(BF16)
    o_ref[ml_end + rest:gate_row, :] = jnp.zeros((LANE - ROPE_DIM, tc), BF16)
    o_ref[gate_row:gate_row + gates, :] = w_ref[ml_end:ml_end + gates, :].astype(BF16)
    o_ref[gate_row + gates:, :] = jnp.zeros((LANE - gates, tc), BF16)


def _pack_w_in(w_in_t, *, tc):
    L, n_in, K = w_in_t.shape
    return pl.pallas_call(
        _pack_w_in_kernel,
        name="pack_w_in",
        grid=(L, K // tc),
        in_specs=[pl.BlockSpec((None, n_in, tc), lambda l, i: (l, 0, i))],
        out_specs=pl.BlockSpec((None, IN_WIDTH_PACKED, tc), lambda l, i: (l, 0, i)),
        out_shape=jax.ShapeDtypeStruct((L, IN_WIDTH_PACKED, K), BF16),
        compiler_params=_params(("parallel", "parallel")),
    )(w_in_t)


def kernel(x, positions, g_mix, w_in, b_gates, w_conv, g_ml_out, g_ret_out, g_q_norm, w_q_up, g_kv_norm, w_kv_up,
           w_out, g_ffn, w_ff1, w_ff2, g_final):
    B, S, D = x.shape
    assert B == 1
    depth = w_in.shape[0]
    tm = min(512, S)
    tm_big = min(1024, S)

    half = ROPE_DIM // 2
    inv = ROPE_THETA ** (-jnp.arange(half, dtype=F32) / half)
    ang = positions[0].astype(F32)[:, None] * inv
    cos4 = jnp.tile(jnp.concatenate([jnp.cos(ang), jnp.cos(ang)], axis=-1), (1, RET_HEADS))
    sin4 = jnp.tile(jnp.concatenate([-jnp.sin(ang), jnp.sin(ang)], axis=-1), (1, RET_HEADS))

    w_in_p = _pack_w_in(jnp.swapaxes(w_in, 1, 2), tc=256)
    q_rank = w_q_up.shape[1]
    w_q4 = w_q_up.astype(BF16).reshape(depth, q_rank, MLA_HEADS, MLA_QK)
    w_q = jnp.concatenate([w_q4[..., :MLA_NOPE].reshape(depth, q_rank, MLA_HEADS * MLA_NOPE),
                           w_q4[..., MLA_NOPE:].reshape(depth, q_rank, MLA_HEADS * ROPE_DIM)], axis=-1)
    w_kv = w_kv_up.astype(BF16)
    bias_t = jnp.broadcast_to(b_gates[:, :, None], (depth, GATE_ROWS, CHUNK))
    idx = jnp.arange(CHUNK)
    tri = jnp.stack([idx[:, None] >= idx[None, :], idx[:, None] <= idx[None, :]]).astype(BF16)
    sel = _gate_select_matrix()
    ret_consts = _retention_consts()
    r3 = lambda g: g[:, None, :]

    xs = x[0]
    for l in range(depth):
        proj, gates_t = _in_proj(xs, r3(g_mix), w_in_p, l, tm=tm_big, tn=1536)
        mq, mkt, rq, rkt = _prep(proj, w_conv, l, cos4, sin4, tm=tm)
        h_f, h_b, y_f, y_b, q, k, v = _mixers(mq, mkt, rq, rkt, proj, gates_t, bias_t, l, tri, sel, ret_consts,
                                              r3(g_q_norm), r3(g_kv_norm), w_q, w_kv, cos4, sin4, cps=4)
        y_mla, (w_out_b, w_ff1_b, w_ff2_b) = _attention(q, k, v, (w_out, w_ff1, w_ff2), l,
                                                        tq=min(2048, S), tk=min(256, S))
        xs = _out_proj(h_f, h_b, y_f, y_b, proj, y_mla, r3(g_ml_out), r3(g_ret_out), l, w_out_b, xs, tm=tm)
        act = _ffn1(xs, r3(g_ffn), w_ff1_b, l, tm=tm_big, tn=2048)
        xs = _ffn2(act, w_ff2_b, xs, tm=tm_big, tn=1024, tk=4096)
    return _final_rms(xs, g_final[None, :], tm=tm)[None]
```

```python
import functools
import math

import numpy as np
import jax
import jax.numpy as jnp
from jax import lax
from jax.experimental import pallas as pl
from jax.experimental.pallas import tpu as pltpu

F32 = jnp.float32
BF16 = jnp.bfloat16

EPS = 1e-6
NEG_INIT = -1e30
ROPE_THETA = 10000.0
CHUNK = 128
LANE = 128
BF16_ROWS = 16
ML_HEADS = 4
ML_DIM = 128
RET_HEADS = 4
RET_QK = 64
RET_V = 128
RET_DECAY_BASE = 5.0
MLA_HEADS = 8
MLA_NOPE = 128
ROPE_DIM = 64
MLA_V = 128
MLA_QK = MLA_NOPE + ROPE_DIM
VMEM_LIMIT = 56 * 1024 * 1024

COL_ML_Q, COL_ML_K, COL_ML_V, COL_ML_O = 0, 4, 8, 12
COL_RQK, COL_RV, COL_RG = 16, 20, 24
COL_CQ, COL_CKV, COL_KROPE, COL_GATES = 28, 32, 34, 35
IN_WIDTH_PACKED = 36 * LANE

_NT = (((1,), (1,)), ((), ()))
_TN = (((0,), (0,)), ((), ()))


def _params(sem):
    return pltpu.CompilerParams(dimension_semantics=sem, vmem_limit_bytes=VMEM_LIMIT)


def _rmsnorm(x, g):
    return x * lax.rsqrt(jnp.mean(x * x, axis=-1, keepdims=True) + EPS) * g


def _rope_grouped(t, cosf, sinf):
    width = t.shape[1]
    lane = lax.broadcasted_iota(jnp.int32, t.shape, 1)
    first_half = (lane % ROPE_DIM) < (ROPE_DIM // 2)
    partner = jnp.where(first_half, pltpu.roll(t, width - ROPE_DIM // 2, 1), pltpu.roll(t, ROPE_DIM // 2, 1))
    return t * cosf + partner * sinf


IN_TN = 12 * LANE
MQK_W = 2 * ML_HEADS * ML_DIM
RQK_W = 2 * RET_HEADS * RET_QK
RQK_OFS = COL_RQK * LANE - IN_TN


def _in_proj_kernel(x_ref, xprev_ref, xnext_ref, g_ref, w_ref, wc_ref, cos_ref, sin_ref,
                    o_ref, gates_ref, mq_out, mkt_out, rq_out, rkt_out, hn_ref, *, nblk):
    i, j = pl.program_id(0), pl.program_id(1)
    tm = x_ref.shape[0]

    def project():
        y = lax.dot_general(hn_ref[...], w_ref[...], _NT, preferred_element_type=F32)
        o_ref[...] = y.astype(o_ref.dtype)
        return y

    @pl.when(j == 0)
    def _():
        g = g_ref[...]
        hn_ref[...] = _rmsnorm(x_ref[...], g).astype(hn_ref.dtype)
        halo = jnp.concatenate([_rmsnorm(xprev_ref[...], g), _rmsnorm(xnext_ref[...], g)], axis=0).astype(BF16)
        y = project()[:, :MQK_W]
        yh = lax.dot_general(halo, w_ref[:MQK_W, :], _NT, preferred_element_type=F32)
        prev_row = jnp.where(i > 0, yh[7:8, :], 0.0)
        next_row = jnp.where(i < nblk - 1, yh[8:9, :], 0.0)
        row = lax.broadcasted_iota(jnp.int32, y.shape, 0)
        y_m = jnp.where(row == 0, prev_row, pltpu.roll(y, 1, 0))
        y_p = jnp.where(row == tm - 1, next_row, pltpu.roll(y, tm - 1, 0))
        wc = wc_ref[...]
        c = y_m * wc[0:1, :] + y * wc[1:2, :] + y_p * wc[2:3, :]
        c = c / (1.0 + jnp.exp(-c))
        half = MQK_W // 2
        mq_out[...] = (c[:, :half] * (ML_DIM ** -0.5)).astype(mq_out.dtype)
        mkt_out[...] = c[:, half:].T.astype(mkt_out.dtype)

    @pl.when(j == 1)
    def _():
        y = project()
        cosf, sinf = cos_ref[...], sin_ref[...]
        half = RQK_W // 2
        rq_out[...] = _rope_grouped(y[:, RQK_OFS:RQK_OFS + half], cosf, sinf).astype(rq_out.dtype)
        rk = _rope_grouped(y[:, RQK_OFS + half:RQK_OFS + RQK_W], cosf, sinf) * (RET_QK ** -0.5)
        rkt_out[...] = rk.T.astype(rkt_out.dtype)

    @pl.when(j == 2)
    def _():
        y = project()
        gates_ref[...] = y[:, y.shape[1] - LANE:].T


def _in_proj(x, g, w, w_conv, cos4, sin4, layer, *, tm):
    S, K = x.shape
    N = w.shape[1]
    assert N == 3 * IN_TN and MQK_W <= IN_TN and 0 <= RQK_OFS and RQK_OFS + RQK_W <= IN_TN
    nblk = S // tm
    hb = tm // 8
    row_blk = lambda width: pl.BlockSpec((tm, width), lambda i, j: (i, 0))
    col_blk = lambda height: pl.BlockSpec((height, tm), lambda i, j: (0, i))
    return pl.pallas_call(
        functools.partial(_in_proj_kernel, nblk=nblk),
        name="in_proj",
        grid=(nblk, N // IN_TN),
        in_specs=[
            row_blk(K),
            pl.BlockSpec((8, K), lambda i, j: (jnp.maximum(i * hb - 1, 0), 0)),
            pl.BlockSpec((8, K), lambda i, j: (jnp.minimum((i + 1) * hb, S // 8 - 1), 0)),
            pl.BlockSpec((None, 1, K), lambda i, j: (layer, 0, 0)),
            pl.BlockSpec((None, IN_TN, K), lambda i, j: (layer, j, 0)),
            pl.BlockSpec((None, 3, MQK_W), lambda i, j: (layer, 0, 0)),
            row_blk(RQK_W // 2), row_blk(RQK_W // 2),
        ],
        out_specs=[pl.BlockSpec((tm, IN_TN), lambda i, j: (i, j)), col_blk(LANE),
                   row_blk(MQK_W // 2), col_blk(MQK_W // 2), row_blk(RQK_W // 2), col_blk(RQK_W // 2)],
        out_shape=[jax.ShapeDtypeStruct((S, N), BF16), jax.ShapeDtypeStruct((LANE, S), F32),
                   jax.ShapeDtypeStruct((S, MQK_W // 2), BF16), jax.ShapeDtypeStruct((MQK_W // 2, S), BF16),
                   jax.ShapeDtypeStruct((S, RQK_W // 2), BF16), jax.ShapeDtypeStruct((RQK_W // 2, S), BF16)],
        scratch_shapes=[pltpu.VMEM((tm, K), BF16)],
        compiler_params=_params(("parallel", "arbitrary")),
    )(x, x, x, g, w, w_conv, cos4, sin4)


def _out_proj_kernel(hf, hb, o_ref, gml_ref, yf, yb, rg_ref, gret_ref, ymla_ref, w_ref, x_ref, out_ref):
    w_ml = ML_HEADS * ML_DIM
    w_ret = RET_HEADS * RET_V
    acc = x_ref[...] + jnp.dot(ymla_ref[...], w_ref[w_ml + w_ret:, :], preferred_element_type=F32)
    y_ml, y_ret = [], []
    for h in range(ML_HEADS):
        sl = slice(h * ML_DIM, (h + 1) * ML_DIM)
        y = _rmsnorm(hf[:, sl].astype(F32) + hb[:, sl].astype(F32), gml_ref[:, sl])
        y_ml.append((y / (1.0 + jnp.exp(-o_ref[:, sl].astype(F32)))).astype(BF16))
    for h in range(RET_HEADS):
        sl = slice(h * RET_V, (h + 1) * RET_V)
        y = _rmsnorm(yf[:, sl].astype(F32) + yb[:, sl].astype(F32), gret_ref[:, sl])
        gate = rg_ref[:, sl].astype(F32)
        y_ret.append((gate / (1.0 + jnp.exp(-gate)) * y).astype(BF16))
    acc = acc + jnp.dot(jnp.concatenate(y_ml, axis=1), w_ref[:w_ml, :], preferred_element_type=F32)
    acc = acc + jnp.dot(jnp.concatenate(y_ret, axis=1), w_ref[w_ml:w_ml + w_ret, :], preferred_element_type=F32)
    out_ref[...] = acc


def _out_proj(h_f, h_b, y_f, y_b, proj, y_mla, g_ml, g_ret, layer, w, x, *, tm):
    S, N = x.shape
    W = ML_HEADS * ML_DIM
    blk = lambda c: pl.BlockSpec((tm, W), lambda i: (i, c))
    gain = pl.BlockSpec((None, 1, W), lambda i: (layer, 0, 0))
    return pl.pallas_call(
        _out_proj_kernel,
        name="out_proj",
        grid=(S // tm,),
        in_specs=[blk(0), blk(0), blk(COL_ML_O * LANE // W), gain, blk(0), blk(0), blk(COL_RG * LANE // W), gain,
                  pl.BlockSpec((tm, y_mla.shape[1]), lambda i: (i, 0)),
                  pl.BlockSpec(w.shape, lambda i: (0, 0)),
                  pl.BlockSpec((tm, N), lambda i: (i, 0))],
        out_specs=pl.BlockSpec((tm, N), lambda i: (i, 0)),
        out_shape=jax.ShapeDtypeStruct((S, N), F32),
        compiler_params=_params(("parallel",)),
    )(h_f, h_b, proj, g_ml, y_f, y_b, proj, g_ret, y_mla, w, x)


def _ffn1_kernel(x_ref, g_ref, w_ref, o_ref, hn_ref):
    @pl.when(pl.program_id(1) == 0)
    def _():
        hn_ref[...] = _rmsnorm(x_ref[...], g_ref[...]).astype(hn_ref.dtype)

    y = jnp.dot(hn_ref[...], w_ref[...], preferred_element_type=F32)
    o_ref[...] = jnp.square(jnp.maximum(y, 0.0)).astype(o_ref.dtype)


def _ffn1(x, g, w, layer, *, tm, tn):
    S, K = x.shape
    N = w.shape[-1]
    return pl.pallas_call(
        _ffn1_kernel,
        name="ffn1",
        grid=(S // tm, N // tn),
        in_specs=[
            pl.BlockSpec((tm, K), lambda i, j: (i, 0)),
            pl.BlockSpec((None, 1, K), lambda i, j: (layer, 0, 0)),
            pl.BlockSpec((K, tn), lambda i, j: (0, j)),
        ],
        out_specs=pl.BlockSpec((tm, tn), lambda i, j: (i, j)),
        out_shape=jax.ShapeDtypeStruct((S, N), BF16),
        scratch_shapes=[pltpu.VMEM((tm, K), BF16)],
        compiler_params=_params(("parallel", "arbitrary")),
    )(x, g, w)


def _ffn2_kernel(a_ref, w_ref, x_ref, o_ref):
    @pl.when(pl.program_id(2) == 0)
    def _():
        o_ref[...] = x_ref[...]

    o_ref[...] += jnp.dot(a_ref[...], w_ref[...], preferred_element_type=F32)


def _ffn2(a, w, x, *, tm, tn, tk):
    S, N = x.shape
    K = a.shape[1]
    return pl.pallas_call(
        _ffn2_kernel,
        name="ffn2",
        grid=(S // tm, N // tn, K // tk),
        in_specs=[
            pl.BlockSpec((tm, tk), lambda i, j, k: (i, k)),
            pl.BlockSpec((tk, tn), lambda i, j, k: (k, j)),
            pl.BlockSpec((tm, tn), lambda i, j, k: (i, j)),
        ],
        out_specs=pl.BlockSpec((tm, tn), lambda i, j, k: (i, j)),
        out_shape=jax.ShapeDtypeStruct((S, N), F32),
        compiler_params=_params(("parallel", "arbitrary", "arbitrary")),
    )(a, w, x)


def _final_rms_kernel(x_ref, g_ref, o_ref):
    o_ref[...] = _rmsnorm(x_ref[...], g_ref[...])


def _final_rms(x, g, *, tm):
    S, K = x.shape
    return pl.pallas_call(
        _final_rms_kernel,
        name="final_rms",
        grid=(S // tm,),
        in_specs=[pl.BlockSpec((tm, K), lambda i: (i, 0)), pl.BlockSpec((1, K), lambda i: (0, 0))],
        out_specs=pl.BlockSpec((tm, K), lambda i: (i, 0)),
        out_shape=jax.ShapeDtypeStruct((S, K), F32),
        compiler_params=_params(("parallel",)),
    )(x, g)


def _log_sigmoid(t):
    return jnp.minimum(t, 0.0) - jnp.log1p(jnp.exp(-jnp.abs(t)))


def _split3(t):
    hi = t.astype(BF16)
    r1 = t - hi.astype(F32)
    mid = r1.astype(BF16)
    lo = (r1 - mid.astype(F32)).astype(BF16)
    return hi, mid, lo


GATE_ROWS = 16


def _mlstm_cumsum(g_t, tri):
    logf_t = _log_sigmoid(g_t)
    return logf_t, jnp.dot(jnp.concatenate(_split3(logf_t), axis=0), tri, preferred_element_type=F32)


def _mlstm_gates(g_t, logf_t, cs):
    b_t = cs[0:GATE_ROWS] + cs[GATE_ROWS:2 * GATE_ROWS] + cs[2 * GATE_ROWS:]
    tot_t = jnp.sum(logf_t, axis=1, keepdims=True)
    r_t = pltpu.roll(g_t, 4, 0) - b_t
    wend_t = tot_t + r_t
    mloc_t = jnp.max(wend_t, axis=1, keepdims=True)
    e_t = jnp.exp(wend_t - mloc_t)
    return b_t, r_t, tot_t, mloc_t, e_t


def _mixers_kernel(mqf, mktf, mvf, gf, rqf, rktf, rvf, mqb, mktb, mvb, gb, rqb, rktb, rvb,
                   bias_ref, tri_ref, sel_ref, decay_ref, inner_ref, zeta_ref,
                   cq_ref, ckv_ref, kr_ref, gq_ref, gkv_ref, wq_ref, wkv_ref, cos_ref, sin_ref,
                   hf_ref, hb_ref, yf_ref, yb_ref, q_out, k_out, v_out, c_ref, m_ref, r_ref, *, chunk_decay, cps):
    mla_refs = (cq_ref, ckv_ref, kr_ref, gq_ref, gkv_ref, wq_ref, wkv_ref, cos_ref, sin_ref, q_out, k_out, v_out)

    @pl.when(pl.program_id(0) == 0)
    def _():
        c_ref[...] = jnp.zeros(c_ref.shape, F32)
        m_ref[...] = jnp.full(m_ref.shape, NEG_INIT, F32)
        r_ref[...] = jnp.zeros(r_ref.shape, F32)

    L = CHUNK
    row = lax.broadcasted_iota(jnp.int32, (L, L), 0)
    col = lax.broadcasted_iota(jnp.int32, (L, L), 1)
    ones = jnp.ones((L, ML_DIM), BF16)
    bias = bias_ref[...]

    all_spans = [[slice(sub * L, (sub + 1) * L) for sub in (t, cps - 1 - t)] for t in range(cps)]

    items = [(d, h) for d in range(2) for h in range(ML_HEADS)]
    m_refs = ((mqf, mktf, mvf, hf_ref), (mqb, mktb, mvb, hb_ref))
    r_refs = ((rqf, rktf, rvf, yf_ref), (rqb, rktb, rvb, yb_ref))

    def m_operands(spans, d, h):
        q_ref, kt_ref, v_ref, _ = m_refs[d]
        sl = slice(h * ML_DIM, (h + 1) * ML_DIM)
        return q_ref[spans[d], sl], kt_ref[sl, spans[d]], v_ref[spans[d], sl]

    def r_operands(spans, d, h):
        q_ref, kt_ref, v_ref, _ = r_refs[d]
        return (q_ref[spans[d], h * RET_QK:(h + 1) * RET_QK], kt_ref[h * RET_QK:(h + 1) * RET_QK, spans[d]],
                v_ref[spans[d], h * RET_V:(h + 1) * RET_V])

    g_ts = [[g_ref[:, spans[d]] + bias for d, g_ref in enumerate((gf, gb))] for spans in all_spans]
    cums = [[_mlstm_cumsum(g_ts[t][d], tri_ref[1 - d]) for d in range(2)] for t in range(cps)]

    m_raw = [[jnp.dot(*m_operands(spans, d, h)[:2], preferred_element_type=F32) for d, h in items]
             for spans in all_spans]
    r_raw = [[jnp.dot(*r_operands(spans, d, h)[:2], preferred_element_type=F32) for d, h in items]
             for spans in all_spans]

    all_gates, all_bcast = [], []
    for t in range(cps):
        gates = [_mlstm_gates(g_ts[t][d], *cums[t][d]) for d in range(2)]
        x_rows = jnp.concatenate([part for g in gates for part in _split3(g[0])], axis=0)
        all_gates.append(gates)
        all_bcast.append(lax.dot_general(x_rows, sel_ref[...], _TN, preferred_element_type=F32))

    _mla_project(*mla_refs)

    m_cur = [m_ref[st:st + 1, :] for st in range(2 * ML_HEADS)]
    m_at, keep_at, gain_at = [], [], []
    for gates in all_gates:
        m_at.append(list(m_cur))
        keeps, gains = [], []
        for d, h in items:
            cf, st = 8 * d + 4 + h, ML_HEADS * d + h
            tot_t, mloc_t = gates[d][2], gates[d][3]
            blast = jnp.broadcast_to(tot_t[cf:cf + 1, :], (1, LANE))
            m_loc = jnp.broadcast_to(mloc_t[cf:cf + 1, :], (1, LANE))
            m_new = jnp.maximum(blast + m_cur[st], m_loc)
            keeps.append(jnp.exp(blast + m_cur[st] - m_new))
            gains.append(jnp.exp(m_loc - m_new))
            m_cur[st] = m_new
        keep_at.append(keeps)
        gain_at.append(gains)
    for st in range(2 * ML_HEADS):
        m_ref[st:st + 1, :] = m_cur[st]

    m_s, m_kw, m_winter, m_floor, r_s, r_kw = [], [], [], [], [], []
    for t, spans in enumerate(all_spans):
        ms_t, mkw_t, mwi_t, mfl_t, rs_t, rkw_t = [], [], [], [], [], []
        for j, (d, h) in enumerate(items):
            cf, st = 8 * d + 4 + h, ML_HEADS * d + h
            _, r_t, _, _, e_t = all_gates[t][d]
            mask = (row >= col) if d == 0 else (row <= col)
            bc = all_bcast[t][:, st * LANE:(st + 1) * LANE]
            dlog = jnp.where(mask, bc + r_t[cf:cf + 1, :], -jnp.inf)
            inter_log = bc + m_at[t][st]
            m_t = jnp.maximum(jnp.max(dlog, axis=1, keepdims=True), inter_log)
            ms_t.append((m_raw[t][j] * jnp.exp(dlog - m_t)).astype(BF16))
            mwi_t.append(jnp.exp(inter_log - m_t))
            mfl_t.append(jnp.exp(-m_t))
            kt = m_operands(spans, d, h)[1]
            mkw_t.append((kt.astype(F32) * e_t[cf:cf + 1, :]).astype(BF16))
            rs_t.append((r_raw[t][j] * decay_ref[st]).astype(BF16))
            rkt = r_operands(spans, d, h)[1]
            rkw_t.append((rkt.astype(F32) * zeta_ref[st:st + 1, :]).astype(BF16))
        m_s.append(ms_t), m_kw.append(mkw_t), m_winter.append(mwi_t), m_floor.append(mfl_t)
        r_s.append(rs_t), r_kw.append(rkw_t)

    c_cur = [c_ref[st] for st in range(2 * ML_HEADS)]
    r_cur = [r_ref[st] for st in range(2 * RET_HEADS)]
    for t, spans in enumerate(all_spans):
        v1s = [jnp.concatenate([m_operands(spans, d, h)[2], ones], axis=1) for d, h in items]
        intra = [jnp.dot(m_s[t][j], v1s[j], preferred_element_type=F32) for j in range(len(items))]
        inter = [jnp.dot(m_operands(spans, d, h)[0], c_cur[ML_HEADS * d + h].astype(BF16),
                         preferred_element_type=F32) for d, h in items]
        c_loc = [jnp.dot(m_kw[t][j], v1s[j], preferred_element_type=F32) for j in range(len(items))]
        ry_intra = [jnp.dot(r_s[t][j], r_operands(spans, d, h)[2], preferred_element_type=F32)
                    for j, (d, h) in enumerate(items)]
        ry_inter = [jnp.dot(r_operands(spans, d, h)[0], r_cur[RET_HEADS * d + h].astype(BF16),
                            preferred_element_type=F32) for d, h in items]
        r_loc = [jnp.dot(r_kw[t][j], r_operands(spans, d, h)[2], preferred_element_type=F32)
                 for j, (d, h) in enumerate(items)]
        for j, (d, h) in enumerate(items):
            st = ML_HEADS * d + h
            w_inter = m_winter[t][j]
            num = intra[j][:, :ML_DIM] + w_inter * inter[j][:, :ML_DIM]
            den = intra[j][:, ML_DIM:] + w_inter * inter[j][:, ML_DIM:]
            m_refs[d][3][spans[d], h * ML_DIM:(h + 1) * ML_DIM] = (
                num / jnp.maximum(jnp.abs(den), m_floor[t][j])).astype(BF16)
            keep, gain = keep_at[t][j], gain_at[t][j]
            c_cur[st] = (jnp.concatenate([keep, keep], axis=1) * c_cur[st]
                         + jnp.concatenate([gain, gain], axis=1) * c_loc[j])
            r_refs[d][3][spans[d], h * RET_V:(h + 1) * RET_V] = (
                ry_intra[j] + ry_inter[j] * inner_ref[st]).astype(BF16)
            r_cur[st] = chunk_decay[st] * r_cur[st] + r_loc[j]
    for st in range(2 * ML_HEADS):
        c_ref[st] = c_cur[st]
        r_ref[st] = r_cur[st]


def _gate_select_matrix():
    sel = np.zeros((2 * 3 * GATE_ROWS, 2 * ML_HEADS * LANE), np.float32)
    for d in range(2):
        for part in range(3):
            for h in range(ML_HEADS):
                st = ML_HEADS * d + h
                sel[(3 * d + part) * GATE_ROWS + 8 * d + 4 + h, st * LANE:(st + 1) * LANE] = 1.0
    return jnp.asarray(sel, BF16)


def _mixers(mq, mkt, rq, rkt, proj, gates_t, bias_t, layer, tri, sel, ret_consts, g_q, g_kv, w_q, w_kv, cos4, sin4,
            *, cps):
    S = proj.shape[0]
    q_rank, kv_rank = w_q.shape[1], w_kv.shape[1]
    rows = cps * CHUNK
    nb = S // rows
    MW = ML_HEADS * ML_DIM
    RQ = RET_HEADS * RET_QK
    RV = RET_HEADS * RET_V
    decay, inner, zeta, chunk_decay = ret_consts
    fwd = lambda n: n
    bwd = lambda n: nb - 1 - n

    def specs(idx):
        return [
            pl.BlockSpec((rows, MW), lambda n: (idx(n), 0)),
            pl.BlockSpec((MW, rows), lambda n: (0, idx(n))),
            pl.BlockSpec((rows, MW), lambda n: (idx(n), COL_ML_V * LANE // MW)),
            pl.BlockSpec((GATE_ROWS, rows), lambda n: (0, idx(n))),
            pl.BlockSpec((rows, RQ), lambda n: (idx(n), 0)),
            pl.BlockSpec((RQ, rows), lambda n: (0, idx(n))),
            pl.BlockSpec((rows, RV), lambda n: (idx(n), COL_RV * LANE // RV)),
        ]

    whole = lambda a: pl.BlockSpec(a.shape, lambda n: (0,) * a.ndim)
    out = lambda idx, w: pl.BlockSpec((rows, w), lambda n: (idx(n), 0))
    operands = (mq, mkt, proj, gates_t, rq, rkt, proj)
    return pl.pallas_call(
        functools.partial(_mixers_kernel, chunk_decay=chunk_decay, cps=cps),
        name="mixers",
        grid=(nb,),
        in_specs=specs(fwd) + specs(bwd) + [
            pl.BlockSpec((None, GATE_ROWS, CHUNK), lambda n: (layer, 0, 0)),
            whole(tri), whole(sel), whole(decay), whole(inner), whole(zeta),
            pl.BlockSpec((rows, q_rank), lambda n: (n, COL_CQ * LANE // q_rank)),
            pl.BlockSpec((rows, kv_rank), lambda n: (n, COL_CKV * LANE // kv_rank)),
            pl.BlockSpec((rows, LANE), lambda n: (n, COL_KROPE)),
            pl.BlockSpec((None, 1, q_rank), lambda n: (layer, 0, 0)),
            pl.BlockSpec((None, 1, kv_rank), lambda n: (layer, 0, 0)),
            pl.BlockSpec((None, q_rank, w_q.shape[2]), lambda n: (layer, 0, 0)),
            pl.BlockSpec((None, kv_rank, w_kv.shape[2]), lambda n: (layer, 0, 0)),
            pl.BlockSpec((rows, 2 * LANE), lambda n: (n, 0)),
            pl.BlockSpec((rows, 2 * LANE), lambda n: (n, 0)),
        ],
        out_specs=[out(fwd, MW), out(bwd, MW), out(fwd, RV), out(bwd, RV),
                   pl.BlockSpec((MLA_HEADS, rows, MLA_QK), lambda n: (0, n, 0)),
                   pl.BlockSpec((MLA_HEADS, rows, MLA_QK), lambda n: (0, n, 0)),
                   pl.BlockSpec((MLA_HEADS, rows, MLA_V), lambda n: (0, n, 0))],
        out_shape=[jax.ShapeDtypeStruct((S, MW), BF16), jax.ShapeDtypeStruct((S, MW), BF16),
                   jax.ShapeDtypeStruct((S, RV), BF16), jax.ShapeDtypeStruct((S, RV), BF16),
                   jax.ShapeDtypeStruct((MLA_HEADS, S, MLA_QK), BF16),
                   jax.ShapeDtypeStruct((MLA_HEADS, S, MLA_QK), BF16),
                   jax.ShapeDtypeStruct((MLA_HEADS, S, MLA_V), BF16)],
        scratch_shapes=[pltpu.VMEM((2 * ML_HEADS, ML_DIM, 2 * ML_DIM), F32), pltpu.VMEM((2 * ML_HEADS, LANE), F32),
                        pltpu.VMEM((2 * RET_HEADS, RET_QK, RET_V), F32)],
        compiler_params=_params(("arbitrary",)),
    )(*operands, *operands, bias_t, tri, sel, decay, inner, zeta, proj, proj, proj, g_q, g_kv, w_q, w_kv, cos4, sin4)


def _retention_consts():
    log_gamma = jnp.log1p(-jnp.exp2(-RET_DECAY_BASE - jnp.arange(RET_HEADS, dtype=F32)))
    idx = jnp.arange(CHUNK, dtype=F32)
    diff = idx[:, None] - idx[None, :]
    decay, inner, zeta = [], [], []
    for d in range(2):
        lg_d = log_gamma if d == 0 else log_gamma[::-1]
        dd = diff if d == 0 else -diff
        pos = idx if d == 0 else (CHUNK - 1.0 - idx)
        keep = dd >= 0
        for h in range(RET_HEADS):
            lg = lg_d[h]
            decay.append(jnp.where(keep, jnp.exp(jnp.where(keep, dd, 0.0) * lg), 0.0))
            inner.append(jnp.broadcast_to(jnp.exp((pos + 1.0) * lg)[:, None], (CHUNK, LANE)))
            zeta.append(jnp.exp((CHUNK - 1.0 - pos) * lg))
    lg_np = np.log1p(-np.exp2(-RET_DECAY_BASE - np.arange(RET_HEADS, dtype=np.float32))).astype(np.float32)
    chunk_decay = tuple(float(np.exp(np.float32(CHUNK) * lg)) for lg in list(lg_np) + list(lg_np[::-1]))
    return jnp.stack(decay), jnp.stack(inner), jnp.stack(zeta), chunk_decay


def _mla_project(cq_ref, ckv_ref, kr_ref, gq_ref, gkv_ref, wq_ref, wkv_ref, cos_ref, sin_ref,
                 q_out, k_out, v_out):
    tm = cq_ref.shape[0]
    cos4 = cos_ref[...]
    sin4 = sin_ref[...]
    qn = _rmsnorm(cq_ref[...].astype(F32), gq_ref[...]).astype(BF16)
    kvn = _rmsnorm(ckv_ref[...].astype(F32), gkv_ref[...]).astype(BF16)
    q_all = jnp.dot(qn, wq_ref[...], preferred_element_type=F32)
    kv_all = jnp.dot(kvn, wkv_ref[...], preferred_element_type=F32)
    q_scale = (MLA_QK ** -0.5) * math.log2(math.e)
    nope_w = MLA_HEADS * MLA_NOPE
    q_rope = _rope_grouped(q_all[:, nope_w:], jnp.concatenate([cos4, cos4], axis=1),
                           jnp.concatenate([sin4, sin4], axis=1)) * q_scale
    k_rope = _rope_grouped(kr_ref[...].astype(F32), cos4[:, :LANE], sin4[:, :LANE])[:, :ROPE_DIM].astype(BF16)
    for h in range(MLA_HEADS):
        q_out[h, :, :MLA_NOPE] = (q_all[:, h * MLA_NOPE:(h + 1) * MLA_NOPE] * q_scale).astype(BF16)
        q_out[h, :, MLA_NOPE:] = q_rope[:, h * ROPE_DIM:(h + 1) * ROPE_DIM].astype(BF16)
        kv0 = h * (MLA_NOPE + MLA_V)
        k_out[h, :, :MLA_NOPE] = kv_all[:, kv0:kv0 + MLA_NOPE].astype(BF16)
        k_out[h, :, MLA_NOPE:] = k_rope
        v_out[h] = kv_all[:, kv0 + MLA_NOPE:kv0 + MLA_NOPE + MLA_V].astype(BF16)


def _attn_kernel(q_ref, k_ref, v_ref, *rest, tk, nk, n_cast):
    w_refs, o_ref, wb_refs, v1_ref = rest[:n_cast], rest[n_cast], rest[n_cast + 1:-1], rest[-1]
    for w_ref, wb_ref in zip(w_refs, wb_refs):
        wb_ref[...] = w_ref[...].astype(wb_ref.dtype)

    @pl.when(pl.program_id(1) == 0)
    def _():
        v1_ref[:, :MLA_V] = v_ref[...]
        v1_ref[:, MLA_V:] = jnp.ones(v_ref.shape, v1_ref.dtype)

    q = q_ref[...]
    m = acc = None
    for c in range(nk):
        k = k_ref[c * tk:(c + 1) * tk, :]
        v = v1_ref[c * tk:(c + 1) * tk, :]
        s = lax.dot_general(q, k, _NT, preferred_element_type=F32)
        m_c = jnp.max(s, axis=-1, keepdims=True)
        if c == 0:
            m = m_c
            acc = jnp.dot(jnp.exp2(s - m).astype(BF16), v, preferred_element_type=F32)
        else:
            m_new = jnp.maximum(m, m_c)
            acc = jnp.exp2(m - m_new) * acc + jnp.dot(jnp.exp2(s - m_new).astype(BF16), v, preferred_element_type=F32)
            m = m_new
    o_ref[...] = (acc[:, :MLA_V] / acc[:, MLA_V:]).astype(o_ref.dtype)


def _attention(q, k, v, cast_weights, layer, *, tq, tk):
    S = q.shape[1]
    nq = S // tq
    n_steps = MLA_HEADS * nq
    w_specs, wb_specs, wb_shapes = [], [], []
    for w in cast_weights:
        _, rows, cols = w.shape
        rb = rows // n_steps
        assert rb * n_steps == rows and rb % BF16_ROWS == 0
        w_specs.append(pl.BlockSpec((None, rb, cols), lambda h, i: (layer, h * nq + i, 0)))
        wb_specs.append(pl.BlockSpec((rb, cols), lambda h, i: (h * nq + i, 0)))
        wb_shapes.append(jax.ShapeDtypeStruct((rows, cols), BF16))
    outs = pl.pallas_call(
        functools.partial(_attn_kernel, tk=tk, nk=S // tk, n_cast=len(cast_weights)),
        name="attention",
        grid=(MLA_HEADS, nq),
        in_specs=[
            pl.BlockSpec((None, tq, MLA_QK), lambda h, i: (h, i, 0)),
            pl.BlockSpec((None, S, MLA_QK), lambda h, i: (h, 0, 0)),
            pl.BlockSpec((None, S, MLA_V), lambda h, i: (h, 0, 0)),
        ] + w_specs,
        out_specs=[pl.BlockSpec((tq, MLA_V), lambda h, i: (i, h))] + wb_specs,
        out_shape=[jax.ShapeDtypeStruct((S, MLA_HEADS * MLA_V), BF16)] + wb_shapes,
        scratch_shapes=[pltpu.VMEM((S, 2 * MLA_V), BF16)],
        compiler_params=_params(("arbitrary", "arbitrary")),
    )(q, k, v, *cast_weights)
    return outs[0], outs[1:]


def _pack_w_in_kernel(w_ref, o_ref):
    n_in, tc = w_ref.shape
    ml_end = 4 * ML_HEADS * ML_DIM
    gates = 4 * ML_HEADS
    rest = n_in - ml_end - gates
    gate_row = IN_WIDTH_PACKED - LANE
    assert ml_end + rest + (LANE - ROPE_DIM) == gate_row
    o_ref[:ml_end, :] = w_ref[:ml_end, :].astype(BF16)
    o_ref[ml_end:ml_end + rest, :] = w_ref[ml_end + gates:, :].astype(BF16)
    o_ref[ml_end + rest:gate_row, :] = jnp.zeros((LANE - ROPE_DIM, tc), BF16)
    o_ref[gate_row:gate_row + gates, :] = w_ref[ml_end:ml_end + gates, :].astype(BF16)
    o_ref[gate_row + gates:, :] = jnp.zeros((LANE - gates, tc), BF16)


def _pack_w_in(w_in_t, *, tc):
    L, n_in, K = w_in_t.shape
    return pl.pallas_call(
        _pack_w_in_kernel,
        name="pack_w_in",
        grid=(L, K // tc),
        in_specs=[pl.BlockSpec((None, n_in, tc), lambda l, i: (l, 0, i))],
        out_specs=pl.BlockSpec((None, IN_WIDTH_PACKED, tc), lambda l, i: (l, 0, i)),
        out_shape=jax.ShapeDtypeStruct((L, IN_WIDTH_PACKED, K), BF16),
        compiler_params=_params(("parallel", "parallel")),
    )(w_in_t)


def kernel(x, positions, g_mix, w_in, b_gates, w_conv, g_ml_out, g_ret_out, g_q_norm, w_q_up, g_kv_norm, w_kv_up,
           w_out, g_ffn, w_ff1, w_ff2, g_final):
    B, S, D = x.shape
    assert B == 1
    depth = w_in.shape[0]
    tm = min(512, S)
    tm_big = min(1024, S)

    half = ROPE_DIM // 2
    inv = ROPE_THETA ** (-jnp.arange(half, dtype=F32) / half)
    ang = positions[0].astype(F32)[:, None] * inv
    cos4 = jnp.tile(jnp.concatenate([jnp.cos(ang), jnp.cos(ang)], axis=-1), (1, RET_HEADS))
    sin4 = jnp.tile(jnp.concatenate([-jnp.sin(ang), jnp.sin(ang)], axis=-1), (1, RET_HEADS))

    w_in_p = _pack_w_in(jnp.swapaxes(w_in, 1, 2), tc=256)
    q_rank = w_q_up.shape[1]
    w_q4 = w_q_up.astype(BF16).reshape(depth, q_rank, MLA_HEADS, MLA_QK)
    w_q = jnp.concatenate([w_q4[..., :MLA_NOPE].reshape(depth, q_rank, MLA_HEADS * MLA_NOPE),
                           w_q4[..., MLA_NOPE:].reshape(depth, q_rank, MLA_HEADS * ROPE_DIM)], axis=-1)
    w_kv = w_kv_up.astype(BF16)
    bias_t = jnp.broadcast_to(b_gates[:, :, None], (depth, GATE_ROWS, CHUNK))
    idx = jnp.arange(CHUNK)
    tri = jnp.stack([idx[:, None] >= idx[None, :], idx[:, None] <= idx[None, :]]).astype(BF16)
    sel = _gate_select_matrix()
    ret_consts = _retention_consts()
    r3 = lambda g: g[:, None, :]

    xs = x[0]
    for l in range(depth):
        proj, gates_t, mq, mkt, rq, rkt = _in_proj(xs, r3(g_mix), w_in_p, w_conv, cos4, sin4, l, tm=tm_big)
        h_f, h_b, y_f, y_b, q, k, v = _mixers(mq, mkt, rq, rkt, proj, gates_t, bias_t, l, tri, sel, ret_consts,
                                              r3(g_q_norm), r3(g_kv_norm), w_q, w_kv, cos4, sin4, cps=4)
        y_mla, (w_out_b, w_ff1_b, w_ff2_b) = _attention(q, k, v, (w_out, w_ff1, w_ff2), l,
                                                        tq=min(2048, S), tk=min(256, S))
        xs = _out_proj(h_f, h_b, y_f, y_b, proj, y_mla, r3(g_ml_out), r3(g_ret_out), l, w_out_b, xs, tm=tm)
        act = _ffn1(xs, r3(g_ffn), w_ff1_b, l, tm=tm_big, tn=2048)
        xs = _ffn2(act, w_ff2_b, xs, tm=tm_big, tn=1024, tk=4096)
    return _final_rms(xs, g_final[None, :], tm=tm)[None]
```

```python
import functools
import math

import numpy as np
import jax
import jax.numpy as jnp
from jax import lax
from jax.experimental import pallas as pl
from jax.experimental.pallas import tpu as pltpu

F32 = jnp.float32
BF16 = jnp.bfloat16

EPS = 1e-6
NEG_INIT = -1e30
ROPE_THETA = 10000.0
CHUNK = 128
LANE = 128
BF16_ROWS = 16
ML_HEADS = 4
ML_DIM = 128
RET_HEADS = 4
RET_QK = 64
RET_V = 128
RET_DECAY_BASE = 5.0
MLA_HEADS = 8
MLA_NOPE = 128
ROPE_DIM = 64
MLA_V = 128
MLA_QK = MLA_NOPE + ROPE_DIM
VMEM_LIMIT = 56 * 1024 * 1024

COL_ML_Q, COL_ML_K, COL_ML_V, COL_ML_O = 0, 4, 8, 12
COL_RQK, COL_RV, COL_RG = 16, 20, 24
COL_CQ, COL_CKV, COL_KROPE, COL_GATES = 28, 32, 34, 35
IN_WIDTH_PACKED = 36 * LANE

_NT = (((1,), (1,)), ((), ()))
_TN = (((0,), (0,)), ((), ()))


def _params(sem):
    return pltpu.CompilerParams(dimension_semantics=sem, vmem_limit_bytes=VMEM_LIMIT)


def _rmsnorm(x, g):
    return x * lax.rsqrt(jnp.mean(x * x, axis=-1, keepdims=True) + EPS) * g


def _rope_grouped(t, cosf, sinf):
    width = t.shape[1]
    lane = lax.broadcasted_iota(jnp.int32, t.shape, 1)
    first_half = (lane % ROPE_DIM) < (ROPE_DIM // 2)
    partner = jnp.where(first_half, pltpu.roll(t, width - ROPE_DIM // 2, 1), pltpu.roll(t, ROPE_DIM // 2, 1))
    return t * cosf + partner * sinf


IN_TN = 12 * LANE
MQK_W = 2 * ML_HEADS * ML_DIM
RQK_W = 2 * RET_HEADS * RET_QK
RQK_OFS = COL_RQK * LANE - IN_TN


def _in_proj_kernel(x_ref, xprev_ref, xnext_ref, g_ref, w_ref, wc_ref, cos_ref, sin_ref,
                    o_ref, gates_ref, mq_out, mkt_out, rq_out, rkt_out, hn_ref, *, nblk):
    i, j = pl.program_id(0), pl.program_id(1)
    tm = x_ref.shape[0]

    def project(hn):
        y = lax.dot_general(hn, w_ref[...], _NT, preferred_element_type=F32)
        o_ref[...] = y.astype(o_ref.dtype)
        return y

    @pl.when(j == 0)
    def _():
        g = g_ref[...]
        hn = _rmsnorm(x_ref[...], g).astype(hn_ref.dtype)
        hn_ref[...] = hn
        halo = jnp.concatenate([_rmsnorm(xprev_ref[...], g), _rmsnorm(xnext_ref[...], g)], axis=0).astype(BF16)
        y = project(hn)[:, :MQK_W]
        yh = lax.dot_general(halo, w_ref[:MQK_W, :], _NT, preferred_element_type=F32)
        prev_row = jnp.where(i > 0, yh[7:8, :], 0.0)
        next_row = jnp.where(i < nblk - 1, yh[8:9, :], 0.0)
        row = lax.broadcasted_iota(jnp.int32, y.shape, 0)
        y_m = jnp.where(row == 0, prev_row, pltpu.roll(y, 1, 0))
        y_p = jnp.where(row == tm - 1, next_row, pltpu.roll(y, tm - 1, 0))
        wc = wc_ref[...]
        c = y_m * wc[0:1, :] + y * wc[1:2, :] + y_p * wc[2:3, :]
        c = c / (1.0 + jnp.exp(-c))
        half = MQK_W // 2
        mq_out[...] = (c[:, :half] * (ML_DIM ** -0.5)).astype(mq_out.dtype)
        mkt_out[...] = c[:, half:].T.astype(mkt_out.dtype)

    @pl.when(j == 1)
    def _():
        y = project(hn_ref[...])
        cosf, sinf = cos_ref[...], sin_ref[...]
        half = RQK_W // 2
        rq_out[...] = _rope_grouped(y[:, RQK_OFS:RQK_OFS + half], cosf, sinf).astype(rq_out.dtype)
        rk = _rope_grouped(y[:, RQK_OFS + half:RQK_OFS + RQK_W], cosf, sinf) * (RET_QK ** -0.5)
        rkt_out[...] = rk.T.astype(rkt_out.dtype)

    @pl.when(j == 2)
    def _():
        y = project(hn_ref[...])
        gates_ref[...] = y[:, y.shape[1] - LANE:].T


def _in_proj(x, g, w, w_conv, cos4, sin4, layer, *, tm):
    S, K = x.shape
    N = w.shape[1]
    assert N == 3 * IN_TN and MQK_W <= IN_TN and 0 <= RQK_OFS and RQK_OFS + RQK_W <= IN_TN
    nblk = S // tm
    hb = tm // 8
    row_blk = lambda width: pl.BlockSpec((tm, width), lambda i, j: (i, 0))
    col_blk = lambda height: pl.BlockSpec((height, tm), lambda i, j: (0, i))
    return pl.pallas_call(
        functools.partial(_in_proj_kernel, nblk=nblk),
        name="in_proj",
        grid=(nblk, N // IN_TN),
        in_specs=[
            row_blk(K),
            pl.BlockSpec((8, K), lambda i, j: (jnp.maximum(i * hb - 1, 0), 0)),
            pl.BlockSpec((8, K), lambda i, j: (jnp.minimum((i + 1) * hb, S // 8 - 1), 0)),
            pl.BlockSpec((None, 1, K), lambda i, j: (layer, 0, 0)),
            pl.BlockSpec((None, IN_TN, K), lambda i, j: (layer, j, 0)),
            pl.BlockSpec((None, 3, MQK_W), lambda i, j: (layer, 0, 0)),
            row_blk(RQK_W // 2), row_blk(RQK_W // 2),
        ],
        out_specs=[pl.BlockSpec((tm, IN_TN), lambda i, j: (i, j)), col_blk(LANE),
                   row_blk(MQK_W // 2), col_blk(MQK_W // 2), row_blk(RQK_W // 2), col_blk(RQK_W // 2)],
        out_shape=[jax.ShapeDtypeStruct((S, N), BF16), jax.ShapeDtypeStruct((LANE, S), F32),
                   jax.ShapeDtypeStruct((S, MQK_W // 2), BF16), jax.ShapeDtypeStruct((MQK_W // 2, S), BF16),
                   jax.ShapeDtypeStruct((S, RQK_W // 2), BF16), jax.ShapeDtypeStruct((RQK_W // 2, S), BF16)],
        scratch_shapes=[pltpu.VMEM((tm, K), BF16)],
        compiler_params=_params(("parallel", "arbitrary")),
    )(x, x, x, g, w, w_conv, cos4, sin4)


def _out_proj_kernel(hf, hb, o_ref, gml_ref, yf, yb, rg_ref, gret_ref, ymla_ref, w_ref, x_ref, gffn_ref,
                     out_ref, hn_out):
    w_ml = ML_HEADS * ML_DIM
    w_ret = RET_HEADS * RET_V
    acc = x_ref[...] + jnp.dot(ymla_ref[...], w_ref[w_ml + w_ret:, :], preferred_element_type=F32)
    y_ml, y_ret = [], []
    for h in range(ML_HEADS):
        sl = slice(h * ML_DIM, (h + 1) * ML_DIM)
        y = _rmsnorm(hf[:, sl].astype(F32) + hb[:, sl].astype(F32), gml_ref[:, sl])
        y_ml.append((y / (1.0 + jnp.exp(-o_ref[:, sl].astype(F32)))).astype(BF16))
    for h in range(RET_HEADS):
        sl = slice(h * RET_V, (h + 1) * RET_V)
        y = _rmsnorm(yf[:, sl].astype(F32) + yb[:, sl].astype(F32), gret_ref[:, sl])
        gate = rg_ref[:, sl].astype(F32)
        y_ret.append((gate / (1.0 + jnp.exp(-gate)) * y).astype(BF16))
    acc = acc + jnp.dot(jnp.concatenate(y_ml, axis=1), w_ref[:w_ml, :], preferred_element_type=F32)
    acc = acc + jnp.dot(jnp.concatenate(y_ret, axis=1), w_ref[w_ml:w_ml + w_ret, :], preferred_element_type=F32)
    out_ref[...] = acc
    hn_out[...] = _rmsnorm(acc, gffn_ref[...]).astype(hn_out.dtype)


def _out_proj(h_f, h_b, y_f, y_b, proj, y_mla, g_ml, g_ret, g_ffn, layer, w, x, *, tm):
    S, N = x.shape
    W = ML_HEADS * ML_DIM
    blk = lambda c: pl.BlockSpec((tm, W), lambda i: (i, c))
    gain = pl.BlockSpec((None, 1, W), lambda i: (layer, 0, 0))
    return pl.pallas_call(
        _out_proj_kernel,
        name="out_proj",
        grid=(S // tm,),
        in_specs=[blk(0), blk(0), blk(COL_ML_O * LANE // W), gain, blk(0), blk(0), blk(COL_RG * LANE // W), gain,
                  pl.BlockSpec((tm, y_mla.shape[1]), lambda i: (i, 0)),
                  pl.BlockSpec(w.shape, lambda i: (0, 0)),
                  pl.BlockSpec((tm, N), lambda i: (i, 0)),
                  pl.BlockSpec((None, 1, N), lambda i: (layer, 0, 0))],
        out_specs=[pl.BlockSpec((tm, N), lambda i: (i, 0)), pl.BlockSpec((tm, N), lambda i: (i, 0))],
        out_shape=[jax.ShapeDtypeStruct((S, N), F32), jax.ShapeDtypeStruct((S, N), BF16)],
        compiler_params=_params(("parallel",)),
    )(h_f, h_b, proj, g_ml, y_f, y_b, proj, g_ret, y_mla, w, x, g_ffn)


def _ffn1_kernel(hn_ref, w_ref, o_ref):
    y = jnp.dot(hn_ref[...], w_ref[...], preferred_element_type=F32)
    o_ref[...] = jnp.square(jnp.maximum(y, 0.0)).astype(o_ref.dtype)


def _ffn1(hn, w, *, tm, tn):
    S, K = hn.shape
    N = w.shape[-1]
    return pl.pallas_call(
        _ffn1_kernel,
        name="ffn1",
        grid=(S // tm, N // tn),
        in_specs=[pl.BlockSpec((tm, K), lambda i, j: (i, 0)), pl.BlockSpec((K, tn), lambda i, j: (0, j))],
        out_specs=pl.BlockSpec((tm, tn), lambda i, j: (i, j)),
        out_shape=jax.ShapeDtypeStruct((S, N), BF16),
        compiler_params=_params(("parallel", "arbitrary")),
    )(hn, w)


def _ffn2_kernel(a_ref, w_ref, x_ref, gf_ref, o_ref, *, final):
    k = pl.program_id(2)

    @pl.when(k == 0)
    def _():
        o_ref[...] = x_ref[...]

    o_ref[...] += jnp.dot(a_ref[...], w_ref[...], preferred_element_type=F32)

    if final:
        @pl.when(k == pl.num_programs(2) - 1)
        def _():
            o_ref[...] = _rmsnorm(o_ref[...], gf_ref[...])


def _ffn2(a, w, x, g_final, *, tm, tn, tk, final):
    S, N = x.shape
    K = a.shape[1]
    assert not final or tn == N
    return pl.pallas_call(
        functools.partial(_ffn2_kernel, final=final),
        name="ffn2_final" if final else "ffn2",
        grid=(S // tm, N // tn, K // tk),
        in_specs=[
            pl.BlockSpec((tm, tk), lambda i, j, k: (i, k)),
            pl.BlockSpec((tk, tn), lambda i, j, k: (k, j)),
            pl.BlockSpec((tm, tn), lambda i, j, k: (i, j)),
            pl.BlockSpec((1, tn), lambda i, j, k: (0, j)),
        ],
        out_specs=pl.BlockSpec((tm, tn), lambda i, j, k: (i, j)),
        out_shape=jax.ShapeDtypeStruct((S, N), F32),
        compiler_params=_params(("parallel", "arbitrary", "arbitrary")),
    )(a, w, x, g_final)


def _log_sigmoid(t):
    return jnp.minimum(t, 0.0) - jnp.log1p(jnp.exp(-jnp.abs(t)))


def _split3(t):
    hi = t.astype(BF16)
    r1 = t - hi.astype(F32)
    mid = r1.astype(BF16)
    lo = (r1 - mid.astype(F32)).astype(BF16)
    return hi, mid, lo


GATE_ROWS = 16


def _mlstm_cumsum(g_t, tri):
    logf_t = _log_sigmoid(g_t)
    return logf_t, jnp.dot(jnp.concatenate(_split3(logf_t), axis=0), tri, preferred_element_type=F32)


def _mlstm_gates(g_t, logf_t, cs):
    b_t = cs[0:GATE_ROWS] + cs[GATE_ROWS:2 * GATE_ROWS] + cs[2 * GATE_ROWS:]
    tot_t = jnp.sum(logf_t, axis=1, keepdims=True)
    r_t = pltpu.roll(g_t, 4, 0) - b_t
    wend_t = tot_t + r_t
    mloc_t = jnp.max(wend_t, axis=1, keepdims=True)
    e_t = jnp.exp(wend_t - mloc_t)
    return b_t, r_t, tot_t, mloc_t, e_t


def _mixers_kernel(mqf, mktf, mvf, gf, rqf, rktf, rvf, mqb, mktb, mvb, gb, rqb, rktb, rvb,
                   bias_ref, tri_ref, sel_ref, decay_ref, inner_ref, zeta_ref,
                   cq_ref, ckv_ref, kr_ref, gq_ref, gkv_ref, wq_ref, wkv_ref, cos_ref, sin_ref,
                   hf_ref, hb_ref, yf_ref, yb_ref, q_out, k_out, v_out, c_ref, m_ref, r_ref, *, chunk_decay, cps):
    mla_refs = (cq_ref, ckv_ref, kr_ref, gq_ref, gkv_ref, wq_ref, wkv_ref, cos_ref, sin_ref, q_out, k_out, v_out)

    @pl.when(pl.program_id(0) == 0)
    def _():
        c_ref[...] = jnp.zeros(c_ref.shape, F32)
        m_ref[...] = jnp.full(m_ref.shape, NEG_INIT, F32)
        r_ref[...] = jnp.zeros(r_ref.shape, F32)

    L = CHUNK
    row = lax.broadcasted_iota(jnp.int32, (L, L), 0)
    col = lax.broadcasted_iota(jnp.int32, (L, L), 1)
    ones = jnp.ones((L, ML_DIM), BF16)
    bias = bias_ref[...]

    all_spans = [[slice(sub * L, (sub + 1) * L) for sub in (t, cps - 1 - t)] for t in range(cps)]

    items = [(d, h) for d in range(2) for h in range(ML_HEADS)]
    m_refs = ((mqf, mktf, mvf, hf_ref), (mqb, mktb, mvb, hb_ref))
    r_refs = ((rqf, rktf, rvf, yf_ref), (rqb, rktb, rvb, yb_ref))

    def m_operands(spans, d, h):
        q_ref, kt_ref, v_ref, _ = m_refs[d]
        sl = slice(h * ML_DIM, (h + 1) * ML_DIM)
        return q_ref[spans[d], sl], kt_ref[sl, spans[d]], v_ref[spans[d], sl]

    def r_operands(spans, d, h):
        q_ref, kt_ref, v_ref, _ = r_refs[d]
        return (q_ref[spans[d], h * RET_QK:(h + 1) * RET_QK], kt_ref[h * RET_QK:(h + 1) * RET_QK, spans[d]],
                v_ref[spans[d], h * RET_V:(h + 1) * RET_V])

    g_ts = [[g_ref[:, spans[d]] + bias for d, g_ref in enumerate((gf, gb))] for spans in all_spans]
    cums = [[_mlstm_cumsum(g_ts[t][d], tri_ref[1 - d]) for d in range(2)] for t in range(cps)]

    m_raw = [[jnp.dot(*m_operands(spans, d, h)[:2], preferred_element_type=F32) for d, h in items]
             for spans in all_spans]
    r_raw = [[jnp.dot(*r_operands(spans, d, h)[:2], preferred_element_type=F32) for d, h in items]
             for spans in all_spans]

    all_gates, all_bcast = [], []
    for t in range(cps):
        gates = [_mlstm_gates(g_ts[t][d], *cums[t][d]) for d in range(2)]
        x_rows = jnp.concatenate([part for g in gates for part in _split3(g[0])], axis=0)
        all_gates.append(gates)
        all_bcast.append(lax.dot_general(x_rows, sel_ref[...], _TN, preferred_element_type=F32))

    _mla_project(*mla_refs)

    m_cur = [m_ref[st:st + 1, :] for st in range(2 * ML_HEADS)]
    m_at, keep_at, gain_at = [], [], []
    for gates in all_gates:
        m_at.append(list(m_cur))
        keeps, gains = [], []
        for d, h in items:
            cf, st = 8 * d + 4 + h, ML_HEADS * d + h
            tot_t, mloc_t = gates[d][2], gates[d][3]
            blast = jnp.broadcast_to(tot_t[cf:cf + 1, :], (1, LANE))
            m_loc = jnp.broadcast_to(mloc_t[cf:cf + 1, :], (1, LANE))
            m_new = jnp.maximum(blast + m_cur[st], m_loc)
            keeps.append(jnp.exp(blast + m_cur[st] - m_new))
            gains.append(jnp.exp(m_loc - m_new))
            m_cur[st] = m_new
        keep_at.append(keeps)
        gain_at.append(gains)
    for st in range(2 * ML_HEADS):
        m_ref[st:st + 1, :] = m_cur[st]

    m_s, m_kw, m_winter, m_floor, r_s, r_kw = [], [], [], [], [], []
    for t, spans in enumerate(all_spans):
        ms_t, mkw_t, mwi_t, mfl_t, rs_t, rkw_t = [], [], [], [], [], []
        for j, (d, h) in enumerate(items):
            cf, st = 8 * d + 4 + h, ML_HEADS * d + h
            _, r_t, _, _, e_t = all_gates[t][d]
            mask = (row >= col) if d == 0 else (row <= col)
            bc = all_bcast[t][:, st * LANE:(st + 1) * LANE]
            dlog = jnp.where(mask, bc + r_t[cf:cf + 1, :], -jnp.inf)
            inter_log = bc + m_at[t][st]
            m_t = jnp.maximum(jnp.max(dlog, axis=1, keepdims=True), inter_log)
            ms_t.append((m_raw[t][j] * jnp.exp(dlog - m_t)).astype(BF16))
            mwi_t.append(jnp.exp(inter_log - m_t))
            mfl_t.append(jnp.exp(-m_t))
            kt = m_operands(spans, d, h)[1]
            mkw_t.append((kt.astype(F32) * e_t[cf:cf + 1, :]).astype(BF16))
            rs_t.append((r_raw[t][j] * decay_ref[st]).astype(BF16))
            rkt = r_operands(spans, d, h)[1]
            rkw_t.append((rkt.astype(F32) * zeta_ref[st:st + 1, :]).astype(BF16))
        m_s.append(ms_t), m_kw.append(mkw_t), m_winter.append(mwi_t), m_floor.append(mfl_t)
        r_s.append(rs_t), r_kw.append(rkw_t)

    c_cur = [c_ref[st] for st in range(2 * ML_HEADS)]
    r_cur = [r_ref[st] for st in range(2 * RET_HEADS)]
    for t, spans in enumerate(all_spans):
        v1s = [jnp.concatenate([m_operands(spans, d, h)[2], ones], axis=1) for d, h in items]
        intra = [jnp.dot(m_s[t][j], v1s[j], preferred_element_type=F32) for j in range(len(items))]
        inter = [jnp.dot(m_operands(spans, d, h)[0], c_cur[ML_HEADS * d + h].astype(BF16),
                         preferred_element_type=F32) for d, h in items]
        c_loc = [jnp.dot(m_kw[t][j], v1s[j], preferred_element_type=F32) for j in range(len(items))]
        ry_intra = [jnp.dot(r_s[t][j], r_operands(spans, d, h)[2], preferred_element_type=F32)
                    for j, (d, h) in enumerate(items)]
        ry_inter = [jnp.dot(r_operands(spans, d, h)[0], r_cur[RET_HEADS * d + h].astype(BF16),
                            preferred_element_type=F32) for d, h in items]
        r_loc = [jnp.dot(r_kw[t][j], r_operands(spans, d, h)[2], preferred_element_type=F32)
                 for j, (d, h) in enumerate(items)]
        for j, (d, h) in enumerate(items):
            st = ML_HEADS * d + h
            w_inter = m_winter[t][j]
            num = intra[j][:, :ML_DIM] + w_inter * inter[j][:, :ML_DIM]
            den = intra[j][:, ML_DIM:] + w_inter * inter[j][:, ML_DIM:]
            m_refs[d][3][spans[d], h * ML_DIM:(h + 1) * ML_DIM] = (
                num / jnp.maximum(jnp.abs(den), m_floor[t][j])).astype(BF16)
            keep, gain = keep_at[t][j], gain_at[t][j]
            c_cur[st] = (jnp.concatenate([keep, keep], axis=1) * c_cur[st]
                         + jnp.concatenate([gain, gain], axis=1) * c_loc[j])
            r_refs[d][3][spans[d], h * RET_V:(h + 1) * RET_V] = (
                ry_intra[j] + ry_inter[j] * inner_ref[st]).astype(BF16)
            r_cur[st] = chunk_decay[st] * r_cur[st] + r_loc[j]
    for st in range(2 * ML_HEADS):
        c_ref[st] = c_cur[st]
        r_ref[st] = r_cur[st]


def _gate_select_matrix():
    sel = np.zeros((2 * 3 * GATE_ROWS, 2 * ML_HEADS * LANE), np.float32)
    for d in range(2):
        for part in range(3):
            for h in range(ML_HEADS):
                st = ML_HEADS * d + h
                sel[(3 * d + part) * GATE_ROWS + 8 * d + 4 + h, st * LANE:(st + 1) * LANE] = 1.0
    return jnp.asarray(sel, BF16)


def _mixers(mq, mkt, rq, rkt, proj, gates_t, bias_t, layer, tri, sel, ret_consts, g_q, g_kv, w_q, w_kv, cos4, sin4,
            *, cps):
    S = proj.shape[0]
    q_rank, kv_rank = w_q.shape[1], w_kv.shape[1]
    rows = cps * CHUNK
    nb = S // rows
    MW = ML_HEADS * ML_DIM
    RQ = RET_HEADS * RET_QK
    RV = RET_HEADS * RET_V
    decay, inner, zeta, chunk_decay = ret_consts
    fwd = lambda n: n
    bwd = lambda n: nb - 1 - n

    def specs(idx):
        return [
            pl.BlockSpec((rows, MW), lambda n: (idx(n), 0)),
            pl.BlockSpec((MW, rows), lambda n: (0, idx(n))),
            pl.BlockSpec((rows, MW), lambda n: (idx(n), COL_ML_V * LANE // MW)),
            pl.BlockSpec((GATE_ROWS, rows), lambda n: (0, idx(n))),
            pl.BlockSpec((rows, RQ), lambda n: (idx(n), 0)),
            pl.BlockSpec((RQ, rows), lambda n: (0, idx(n))),
            pl.BlockSpec((rows, RV), lambda n: (idx(n), COL_RV * LANE // RV)),
        ]

    whole = lambda a: pl.BlockSpec(a.shape, lambda n: (0,) * a.ndim)
    out = lambda idx, w: pl.BlockSpec((rows, w), lambda n: (idx(n), 0))
    operands = (mq, mkt, proj, gates_t, rq, rkt, proj)
    return pl.pallas_call(
        functools.partial(_mixers_kernel, chunk_decay=chunk_decay, cps=cps),
        name="mixers",
        grid=(nb,),
        in_specs=specs(fwd) + specs(bwd) + [
            pl.BlockSpec((None, GATE_ROWS, CHUNK), lambda n: (layer, 0, 0)),
            whole(tri), whole(sel), whole(decay), whole(inner), whole(zeta),
            pl.BlockSpec((rows, q_rank), lambda n: (n, COL_CQ * LANE // q_rank)),
            pl.BlockSpec((rows, kv_rank), lambda n: (n, COL_CKV * LANE // kv_rank)),
            pl.BlockSpec((rows, LANE), lambda n: (n, COL_KROPE)),
            pl.BlockSpec((None, 1, q_rank), lambda n: (layer, 0, 0)),
            pl.BlockSpec((None, 1, kv_rank), lambda n: (layer, 0, 0)),
            pl.BlockSpec((None, q_rank, w_q.shape[2]), lambda n: (layer, 0, 0)),
            pl.BlockSpec((None, kv_rank, w_kv.shape[2]), lambda n: (layer, 0, 0)),
            pl.BlockSpec((rows, 2 * LANE), lambda n: (n, 0)),
            pl.BlockSpec((rows, 2 * LANE), lambda n: (n, 0)),
        ],
        out_specs=[out(fwd, MW), out(bwd, MW), out(fwd, RV), out(bwd, RV),
                   pl.BlockSpec((MLA_HEADS, rows, MLA_QK), lambda n: (0, n, 0)),
                   pl.BlockSpec((MLA_HEADS, rows, MLA_QK), lambda n: (0, n, 0)),
                   pl.BlockSpec((MLA_HEADS, rows, 2 * MLA_V), lambda n: (0, n, 0))],
        out_shape=[jax.ShapeDtypeStruct((S, MW), BF16), jax.ShapeDtypeStruct((S, MW), BF16),
                   jax.ShapeDtypeStruct((S, RV), BF16), jax.ShapeDtypeStruct((S, RV), BF16),
                   jax.ShapeDtypeStruct((MLA_HEADS, S, MLA_QK), BF16),
                   jax.ShapeDtypeStruct((MLA_HEADS, S, MLA_QK), BF16),
                   jax.ShapeDtypeStruct((MLA_HEADS, S, 2 * MLA_V), BF16)],
        scratch_shapes=[pltpu.VMEM((2 * ML_HEADS, ML_DIM, 2 * ML_DIM), F32), pltpu.VMEM((2 * ML_HEADS, LANE), F32),
                        pltpu.VMEM((2 * RET_HEADS, RET_QK, RET_V), F32)],
        compiler_params=_params(("arbitrary",)),
    )(*operands, *operands, bias_t, tri, sel, decay, inner, zeta, proj, proj, proj, g_q, g_kv, w_q, w_kv, cos4, sin4)


def _retention_consts():
    log_gamma = jnp.log1p(-jnp.exp2(-RET_DECAY_BASE - jnp.arange(RET_HEADS, dtype=F32)))
    idx = jnp.arange(CHUNK, dtype=F32)
    diff = idx[:, None] - idx[None, :]
    decay, inner, zeta = [], [], []
    for d in range(2):
        lg_d = log_gamma if d == 0 else log_gamma[::-1]
        dd = diff if d == 0 else -diff
        pos = idx if d == 0 else (CHUNK - 1.0 - idx)
        keep = dd >= 0
        for h in range(RET_HEADS):
            lg = lg_d[h]
            decay.append(jnp.where(keep, jnp.exp(jnp.where(keep, dd, 0.0) * lg), 0.0))
            inner.append(jnp.broadcast_to(jnp.exp((pos + 1.0) * lg)[:, None], (CHUNK, LANE)))
            zeta.append(jnp.exp((CHUNK - 1.0 - pos) * lg))
    lg_np = np.log1p(-np.exp2(-RET_DECAY_BASE - np.arange(RET_HEADS, dtype=np.float32))).astype(np.float32)
    chunk_decay = tuple(float(np.exp(np.float32(CHUNK) * lg)) for lg in list(lg_np) + list(lg_np[::-1]))
    return jnp.stack(decay), jnp.stack(inner), jnp.stack(zeta), chunk_decay


def _mla_project(cq_ref, ckv_ref, kr_ref, gq_ref, gkv_ref, wq_ref, wkv_ref, cos_ref, sin_ref,
                 q_out, k_out, v_out):
    tm = cq_ref.shape[0]
    cos4 = cos_ref[...]
    sin4 = sin_ref[...]
    qn = _rmsnorm(cq_ref[...].astype(F32), gq_ref[...]).astype(BF16)
    kvn = _rmsnorm(ckv_ref[...].astype(F32), gkv_ref[...]).astype(BF16)
    q_all = jnp.dot(qn, wq_ref[...], preferred_element_type=F32)
    kv_all = jnp.dot(kvn, wkv_ref[...], preferred_element_type=F32)
    q_scale = (MLA_QK ** -0.5) * math.log2(math.e)
    nope_w = MLA_HEADS * MLA_NOPE
    q_rope = _rope_grouped(q_all[:, nope_w:], jnp.concatenate([cos4, cos4], axis=1),
                           jnp.concatenate([sin4, sin4], axis=1)) * q_scale
    k_rope = _rope_grouped(kr_ref[...].astype(F32), cos4[:, :LANE], sin4[:, :LANE])[:, :ROPE_DIM].astype(BF16)
    ones = jnp.ones((tm, MLA_V), BF16)
    for h in range(MLA_HEADS):
        q_out[h, :, :MLA_NOPE] = (q_all[:, h * MLA_NOPE:(h + 1) * MLA_NOPE] * q_scale).astype(BF16)
        q_out[h, :, MLA_NOPE:] = q_rope[:, h * ROPE_DIM:(h + 1) * ROPE_DIM].astype(BF16)
        kv0 = h * (MLA_NOPE + MLA_V)
        k_out[h, :, :MLA_NOPE] = kv_all[:, kv0:kv0 + MLA_NOPE].astype(BF16)
        k_out[h, :, MLA_NOPE:] = k_rope
        v_out[h, :, :MLA_V] = kv_all[:, kv0 + MLA_NOPE:kv0 + MLA_NOPE + MLA_V].astype(BF16)
        v_out[h, :, MLA_V:] = ones


def _attn_kernel(q_ref, k_ref, v_ref, *rest, tk, nk, n_cast):
    w_refs, o_ref, wb_refs = rest[:n_cast], rest[n_cast], rest[n_cast + 1:]
    for w_ref, wb_ref in zip(w_refs, wb_refs):
        wb_ref[...] = w_ref[...].astype(wb_ref.dtype)

    q = q_ref[...]
    m = acc = None
    for c in range(nk):
        k = k_ref[c * tk:(c + 1) * tk, :]
        v = v_ref[c * tk:(c + 1) * tk, :]
        s = lax.dot_general(q, k, _NT, preferred_element_type=F32)
        m_c = jnp.max(s, axis=-1, keepdims=True)
        if c == 0:
            m = m_c
            acc = jnp.dot(jnp.exp2(s - m).astype(BF16), v, preferred_element_type=F32)
        else:
            m_new = jnp.maximum(m, m_c)
            acc = jnp.exp2(m - m_new) * acc + jnp.dot(jnp.exp2(s - m_new).astype(BF16), v, preferred_element_type=F32)
            m = m_new
    o_ref[...] = (acc[:, :MLA_V] / acc[:, MLA_V:]).astype(o_ref.dtype)


def _attention(q, k, v, cast_weights, layer, *, tq, tk):
    S = q.shape[1]
    nq = S // tq
    n_steps = MLA_HEADS * nq
    w_specs, wb_specs, wb_shapes = [], [], []
    for w in cast_weights:
        _, rows, cols = w.shape
        rb = rows // n_steps
        assert rb * n_steps == rows and rb % BF16_ROWS == 0
        w_specs.append(pl.BlockSpec((None, rb, cols), lambda h, i: (layer, h * nq + i, 0)))
        wb_specs.append(pl.BlockSpec((rb, cols), lambda h, i: (h * nq + i, 0)))
        wb_shapes.append(jax.ShapeDtypeStruct((rows, cols), BF16))
    outs = pl.pallas_call(
        functools.partial(_attn_kernel, tk=tk, nk=S // tk, n_cast=len(cast_weights)),
        name="attention",
        grid=(MLA_HEADS, nq),
        in_specs=[
            pl.BlockSpec((None, tq, MLA_QK), lambda h, i: (h, i, 0)),
            pl.BlockSpec((None, S, MLA_QK), lambda h, i: (h, 0, 0)),
            pl.BlockSpec((None, S, 2 * MLA_V), lambda h, i: (h, 0, 0)),
        ] + w_specs,
        out_specs=[pl.BlockSpec((tq, MLA_V), lambda h, i: (i, h))] + wb_specs,
        out_shape=[jax.ShapeDtypeStruct((S, MLA_HEADS * MLA_V), BF16)] + wb_shapes,
        compiler_params=_params(("parallel", "arbitrary")),
    )(q, k, v, *cast_weights)
    return outs[0], outs[1:]


def _pack_w_in_kernel(w_ref, o_ref):
    n_in, tc = w_ref.shape
    ml_end = 4 * ML_HEADS * ML_DIM
    gates = 4 * ML_HEADS
    rest = n_in - ml_end - gates
    gate_row = IN_WIDTH_PACKED - LANE
    assert ml_end + rest + (LANE - ROPE_DIM) == gate_row
    o_ref[:ml_end, :] = w_ref[:ml_end, :].astype(BF16)
    o_ref[ml_end:ml_end + rest, :] = w_ref[ml_end + gates:, :].astype(BF16)
    o_ref[ml_end + rest:gate_row, :] = jnp.zeros((LANE - ROPE_DIM, tc), BF16)
    o_ref[gate_row:gate_row + gates, :] = w_ref[ml_end:ml_end + gates, :].astype(BF16)
    o_ref[gate_row + gates:, :] = jnp.zeros((LANE - gates, tc), BF16)


def _pack_w_in(w_in_t, *, tc):
    L, n_in, K = w_in_t.shape
    return pl.pallas_call(
        _pack_w_in_kernel,
        name="pack_w_in",
        grid=(L, K // tc),
        in_specs=[pl.BlockSpec((None, n_in, tc), lambda l, i: (l, 0, i))],
        out_specs=pl.BlockSpec((None, IN_WIDTH_PACKED, tc), lambda l, i: (l, 0, i)),
        out_shape=jax.ShapeDtypeStruct((L, IN_WIDTH_PACKED, K), BF16),
        compiler_params=_params(("parallel", "parallel")),
    )(w_in_t)


def kernel(x, positions, g_mix, w_in, b_gates, w_conv, g_ml_out, g_ret_out, g_q_norm, w_q_up, g_kv_norm, w_kv_up,
           w_out, g_ffn, w_ff1, w_ff2, g_final):
    B, S, D = x.shape
    assert B == 1
    depth = w_in.shape[0]
    tm = min(512, S)
    tm_big = min(1024, S)

    half = ROPE_DIM // 2
    inv = ROPE_THETA ** (-jnp.arange(half, dtype=F32) / half)
    ang = positions[0].astype(F32)[:, None] * inv
    cos4 = jnp.tile(jnp.concatenate([jnp.cos(ang), jnp.cos(ang)], axis=-1), (1, RET_HEADS))
    sin4 = jnp.tile(jnp.concatenate([-jnp.sin(ang), jnp.sin(ang)], axis=-1), (1, RET_HEADS))

    w_in_p = _pack_w_in(jnp.swapaxes(w_in, 1, 2), tc=256)
    q_rank = w_q_up.shape[1]
    w_q4 = w_q_up.astype(BF16).reshape(depth, q_rank, MLA_HEADS, MLA_QK)
    w_q = jnp.concatenate([w_q4[..., :MLA_NOPE].reshape(depth, q_rank, MLA_HEADS * MLA_NOPE),
                           w_q4[..., MLA_NOPE:].reshape(depth, q_rank, MLA_HEADS * ROPE_DIM)], axis=-1)
    w_kv = w_kv_up.astype(BF16)
    bias_t = jnp.broadcast_to(b_gates[:, :, None], (depth, GATE_ROWS, CHUNK))
    idx = jnp.arange(CHUNK)
    tri = jnp.stack([idx[:, None] >= idx[None, :], idx[:, None] <= idx[None, :]]).astype(BF16)
    sel = _gate_select_matrix()
    ret_consts = _retention_consts()
    r3 = lambda g: g[:, None, :]

    xs = x[0]
    for l in range(depth):
        proj, gates_t, mq, mkt, rq, rkt = _in_proj(xs, r3(g_mix), w_in_p, w_conv, cos4, sin4, l, tm=tm_big)
        h_f, h_b, y_f, y_b, q, k, v = _mixers(mq, mkt, rq, rkt, proj, gates_t, bias_t, l, tri, sel, ret_consts,
                                              r3(g_q_norm), r3(g_kv_norm), w_q, w_kv, cos4, sin4, cps=4)
        y_mla, (w_out_b, w_ff1_b, w_ff2_b) = _attention(q, k, v, (w_out, w_ff1, w_ff2), l,
                                                        tq=min(2048, S), tk=min(256, S))
        xs, hn = _out_proj(h_f, h_b, y_f, y_b, proj, y_mla, r3(g_ml_out), r3(g_ret_out), r3(g_ffn), l, w_out_b, xs,
                           tm=tm)
        act = _ffn1(hn, w_ff1_b, tm=min(2048, S), tn=1024)
        if l < depth - 1:
            xs = _ffn2(act, w_ff2_b, xs, g_final[None, :], tm=tm_big, tn=1024, tk=4096, final=False)
        else:
            xs = _ffn2(act, w_ff2_b, xs, g_final[None, :], tm=tm_big, tn=D, tk=1024, final=True)
    return xs[None]
```

```python
import functools
import math

import numpy as np
import jax
import jax.numpy as jnp
from jax import lax
from jax.experimental import pallas as pl
from jax.experimental.pallas import tpu as pltpu

F32 = jnp.float32
BF16 = jnp.bfloat16

EPS = 1e-6
NEG_INIT = -1e30
ROPE_THETA = 10000.0
CHUNK = 128
LANE = 128
BF16_ROWS = 16
ML_HEADS = 4
ML_DIM = 128
RET_HEADS = 4
RET_QK = 64
RET_V = 128
RET_DECAY_BASE = 5.0
MLA_HEADS = 8
MLA_NOPE = 128
ROPE_DIM = 64
MLA_V = 128
MLA_QK = MLA_NOPE + ROPE_DIM
VMEM_LIMIT = 56 * 1024 * 1024

COL_ML_Q, COL_ML_K, COL_ML_V, COL_ML_O = 0, 4, 8, 12
COL_RQK, COL_RV, COL_RG = 16, 20, 24
COL_CQ, COL_CKV, COL_KROPE, COL_GATES = 28, 32, 34, 35
IN_WIDTH_PACKED = 36 * LANE

_NT = (((1,), (1,)), ((), ()))
_TN = (((0,), (0,)), ((), ()))


def _params(sem):
    return pltpu.CompilerParams(dimension_semantics=sem, vmem_limit_bytes=VMEM_LIMIT)


def _rmsnorm(x, g):
    return x * lax.rsqrt(jnp.mean(x * x, axis=-1, keepdims=True) + EPS) * g


def _rope_grouped(t, cosf, sinf):
    width = t.shape[1]
    lane = lax.broadcasted_iota(jnp.int32, t.shape, 1)
    first_half = (lane % ROPE_DIM) < (ROPE_DIM // 2)
    partner = jnp.where(first_half, pltpu.roll(t, width - ROPE_DIM // 2, 1), pltpu.roll(t, ROPE_DIM // 2, 1))
    return t * cosf + partner * sinf


IN_TN = 12 * LANE
MQK_W = 2 * ML_HEADS * ML_DIM
RQK_W = 2 * RET_HEADS * RET_QK
RQK_OFS = COL_RQK * LANE - IN_TN


def _in_proj_kernel(x_ref, xprev_ref, xnext_ref, g_ref, w_ref, wc_ref, cos_ref, sin_ref,
                    o_ref, gates_ref, mq_out, mkt_out, rq_out, rkt_out, hn_ref, *, nblk):
    i, j = pl.program_id(0), pl.program_id(1)
    tm = x_ref.shape[0]

    def project(hn):
        y = lax.dot_general(hn, w_ref[...], _NT, preferred_element_type=F32)
        o_ref[...] = y.astype(o_ref.dtype)
        return y

    @pl.when(j == 0)
    def _():
        g = g_ref[...]
        hn = _rmsnorm(x_ref[...], g).astype(hn_ref.dtype)
        hn_ref[...] = hn
        halo = jnp.concatenate([_rmsnorm(xprev_ref[...], g), _rmsnorm(xnext_ref[...], g)], axis=0).astype(BF16)
        y = project(hn)[:, :MQK_W]
        yh = lax.dot_general(halo, w_ref[:MQK_W, :], _NT, preferred_element_type=F32)
        prev_row = jnp.where(i > 0, yh[7:8, :], 0.0)
        next_row = jnp.where(i < nblk - 1, yh[8:9, :], 0.0)
        row = lax.broadcasted_iota(jnp.int32, y.shape, 0)
        y_m = jnp.where(row == 0, prev_row, pltpu.roll(y, 1, 0))
        y_p = jnp.where(row == tm - 1, next_row, pltpu.roll(y, tm - 1, 0))
        wc = wc_ref[...]
        c = y_m * wc[0:1, :] + y * wc[1:2, :] + y_p * wc[2:3, :]
        c = c / (1.0 + jnp.exp(-c))
        half = MQK_W // 2
        mq_out[...] = (c[:, :half] * (ML_DIM ** -0.5)).astype(mq_out.dtype)
        mkt_out[...] = c[:, half:].T.astype(mkt_out.dtype)

    @pl.when(j == 1)
    def _():
        y = project(hn_ref[...])
        cosf, sinf = cos_ref[...], sin_ref[...]
        half = RQK_W // 2
        rq_out[...] = _rope_grouped(y[:, RQK_OFS:RQK_OFS + half], cosf, sinf).astype(rq_out.dtype)
        rk = _rope_grouped(y[:, RQK_OFS + half:RQK_OFS + RQK_W], cosf, sinf) * (RET_QK ** -0.5)
        rkt_out[...] = rk.T.astype(rkt_out.dtype)

    @pl.when(j == 2)
    def _():
        y = project(hn_ref[...])
        gates_ref[...] = y[:, y.shape[1] - LANE:].T


def _in_proj(x, g, w, w_conv, cos4, sin4, layer, w_layer, *, tm):
    S, K = x.shape
    N = w.shape[1]
    assert N == 3 * IN_TN and MQK_W <= IN_TN and 0 <= RQK_OFS and RQK_OFS + RQK_W <= IN_TN
    nblk = S // tm
    hb = tm // 8
    row_blk = lambda width: pl.BlockSpec((tm, width), lambda i, j: (i, 0))
    col_blk = lambda height: pl.BlockSpec((height, tm), lambda i, j: (0, i))
    return pl.pallas_call(
        functools.partial(_in_proj_kernel, nblk=nblk),
        name="in_proj",
        grid=(nblk, N // IN_TN),
        in_specs=[
            row_blk(K),
            pl.BlockSpec((8, K), lambda i, j: (jnp.maximum(i * hb - 1, 0), 0)),
            pl.BlockSpec((8, K), lambda i, j: (jnp.minimum((i + 1) * hb, S // 8 - 1), 0)),
            pl.BlockSpec((None, 1, K), lambda i, j: (layer, 0, 0)),
            pl.BlockSpec((None, IN_TN, K), lambda i, j: (w_layer, j, 0)),
            pl.BlockSpec((None, 3, MQK_W), lambda i, j: (layer, 0, 0)),
            row_blk(RQK_W // 2), row_blk(RQK_W // 2),
        ],
        out_specs=[pl.BlockSpec((tm, IN_TN), lambda i, j: (i, j)), col_blk(LANE),
                   row_blk(MQK_W // 2), col_blk(MQK_W // 2), row_blk(RQK_W // 2), col_blk(RQK_W // 2)],
        out_shape=[jax.ShapeDtypeStruct((S, N), BF16), jax.ShapeDtypeStruct((LANE, S), F32),
                   jax.ShapeDtypeStruct((S, MQK_W // 2), BF16), jax.ShapeDtypeStruct((MQK_W // 2, S), BF16),
                   jax.ShapeDtypeStruct((S, RQK_W // 2), BF16), jax.ShapeDtypeStruct((RQK_W // 2, S), BF16)],
        scratch_shapes=[pltpu.VMEM((tm, K), BF16)],
        compiler_params=_params(("parallel", "arbitrary")),
    )(x, x, x, g, w, w_conv, cos4, sin4)


def _out_proj_kernel(hf, hb, o_ref, gml_ref, yf, yb, rg_ref, gret_ref, ymla_ref, w_ref, x_ref, gffn_ref,
                     out_ref, hn_out):
    w_ml = ML_HEADS * ML_DIM
    w_ret = RET_HEADS * RET_V
    acc = x_ref[...] + jnp.dot(ymla_ref[...], w_ref[w_ml + w_ret:, :], preferred_element_type=F32)
    y_ml, y_ret = [], []
    for h in range(ML_HEADS):
        sl = slice(h * ML_DIM, (h + 1) * ML_DIM)
        y = _rmsnorm(hf[:, sl].astype(F32) + hb[:, sl].astype(F32), gml_ref[:, sl])
        y_ml.append((y / (1.0 + jnp.exp(-o_ref[:, sl].astype(F32)))).astype(BF16))
    for h in range(RET_HEADS):
        sl = slice(h * RET_V, (h + 1) * RET_V)
        y = _rmsnorm(yf[:, sl].astype(F32) + yb[:, sl].astype(F32), gret_ref[:, sl])
        gate = rg_ref[:, sl].astype(F32)
        y_ret.append((gate / (1.0 + jnp.exp(-gate)) * y).astype(BF16))
    acc = acc + jnp.dot(jnp.concatenate(y_ml, axis=1), w_ref[:w_ml, :], preferred_element_type=F32)
    acc = acc + jnp.dot(jnp.concatenate(y_ret, axis=1), w_ref[w_ml:w_ml + w_ret, :], preferred_element_type=F32)
    out_ref[...] = acc
    hn_out[...] = _rmsnorm(acc, gffn_ref[...]).astype(hn_out.dtype)


def _out_proj(h_f, h_b, y_f, y_b, proj, y_mla, g_ml, g_ret, g_ffn, layer, w, x, *, tm):
    S, N = x.shape
    W = ML_HEADS * ML_DIM
    blk = lambda c: pl.BlockSpec((tm, W), lambda i: (i, c))
    gain = pl.BlockSpec((None, 1, W), lambda i: (layer, 0, 0))
    return pl.pallas_call(
        _out_proj_kernel,
        name="out_proj",
        grid=(S // tm,),
        in_specs=[blk(0), blk(0), blk(COL_ML_O * LANE // W), gain, blk(0), blk(0), blk(COL_RG * LANE // W), gain,
                  pl.BlockSpec((tm, y_mla.shape[1]), lambda i: (i, 0)),
                  pl.BlockSpec(w.shape, lambda i: (0, 0)),
                  pl.BlockSpec((tm, N), lambda i: (i, 0)),
                  pl.BlockSpec((None, 1, N), lambda i: (layer, 0, 0))],
        out_specs=[pl.BlockSpec((tm, N), lambda i: (i, 0)), pl.BlockSpec((tm, N), lambda i: (i, 0))],
        out_shape=[jax.ShapeDtypeStruct((S, N), F32), jax.ShapeDtypeStruct((S, N), BF16)],
        compiler_params=_params(("parallel",)),
    )(h_f, h_b, proj, g_ml, y_f, y_b, proj, g_ret, y_mla, w, x, g_ffn)


def _ffn1_kernel(hn_ref, w_ref, o_ref):
    y = jnp.dot(hn_ref[...], w_ref[...], preferred_element_type=F32)
    o_ref[...] = jnp.square(jnp.maximum(y, 0.0)).astype(o_ref.dtype)


def _ffn1(hn, w, *, tm, tn):
    S, K = hn.shape
    N = w.shape[-1]
    return pl.pallas_call(
        _ffn1_kernel,
        name="ffn1",
        grid=(S // tm, N // tn),
        in_specs=[pl.BlockSpec((tm, K), lambda i, j: (i, 0)), pl.BlockSpec((K, tn), lambda i, j: (0, j))],
        out_specs=pl.BlockSpec((tm, tn), lambda i, j: (i, j)),
        out_shape=jax.ShapeDtypeStruct((S, N), BF16),
        compiler_params=_params(("parallel", "arbitrary")),
    )(hn, w)


def _ffn2_kernel(a_ref, w_ref, x_ref, gf_ref, o_ref, *, final):
    k = pl.program_id(2)

    @pl.when(k == 0)
    def _():
        o_ref[...] = x_ref[...]

    o_ref[...] += jnp.dot(a_ref[...], w_ref[...], preferred_element_type=F32)

    if final:
        @pl.when(k == pl.num_programs(2) - 1)
        def _():
            o_ref[...] = _rmsnorm(o_ref[...], gf_ref[...])


def _ffn2(a, w, x, g_final, *, tm, tn, tk, final):
    S, N = x.shape
    K = a.shape[1]
    assert not final or tn == N
    return pl.pallas_call(
        functools.partial(_ffn2_kernel, final=final),
        name="ffn2_final" if final else "ffn2",
        grid=(S // tm, N // tn, K // tk),
        in_specs=[
            pl.BlockSpec((tm, tk), lambda i, j, k: (i, k)),
            pl.BlockSpec((tk, tn), lambda i, j, k: (k, j)),
            pl.BlockSpec((tm, tn), lambda i, j, k: (i, j)),
            pl.BlockSpec((1, tn), lambda i, j, k: (0, j)),
        ],
        out_specs=pl.BlockSpec((tm, tn), lambda i, j, k: (i, j)),
        out_shape=jax.ShapeDtypeStruct((S, N), F32),
        compiler_params=_params(("parallel", "arbitrary", "arbitrary")),
    )(a, w, x, g_final)


def _log_sigmoid(t):
    return jnp.minimum(t, 0.0) - jnp.log1p(jnp.exp(-jnp.abs(t)))


def _split3(t):
    hi = t.astype(BF16)
    r1 = t - hi.astype(F32)
    mid = r1.astype(BF16)
    lo = (r1 - mid.astype(F32)).astype(BF16)
    return hi, mid, lo


GATE_ROWS = 16


def _mlstm_cumsum(g_t, tri):
    logf_t = _log_sigmoid(g_t)
    return logf_t, jnp.dot(jnp.concatenate(_split3(logf_t), axis=0), tri, preferred_element_type=F32)


def _mlstm_gates(g_t, logf_t, cs):
    b_t = cs[0:GATE_ROWS] + cs[GATE_ROWS:2 * GATE_ROWS] + cs[2 * GATE_ROWS:]
    tot_t = jnp.sum(logf_t, axis=1, keepdims=True)
    r_t = pltpu.roll(g_t, 4, 0) - b_t
    wend_t = tot_t + r_t
    mloc_t = jnp.max(wend_t, axis=1, keepdims=True)
    e_t = jnp.exp(wend_t - mloc_t)
    return b_t, r_t, tot_t, mloc_t, e_t


def _mixers_kernel(mqf, mktf, mvf, gf, rqf, rktf, rvf, mqb, mktb, mvb, gb, rqb, rktb, rvb,
                   bias_ref, tri_ref, sel_ref, decay_ref, inner_ref, zeta_ref,
                   cq_ref, ckv_ref, kr_ref, gq_ref, gkv_ref, wq_ref, wkv_ref, cos_ref, sin_ref,
                   hf_ref, hb_ref, yf_ref, yb_ref, q_out, k_out, v_out, c_ref, m_ref, r_ref, *, chunk_decay, cps):
    mla_refs = (cq_ref, ckv_ref, kr_ref, gq_ref, gkv_ref, wq_ref, wkv_ref, cos_ref, sin_ref, q_out, k_out, v_out)

    @pl.when(pl.program_id(0) == 0)
    def _():
        c_ref[...] = jnp.zeros(c_ref.shape, F32)
        m_ref[...] = jnp.full(m_ref.shape, NEG_INIT, F32)
        r_ref[...] = jnp.zeros(r_ref.shape, F32)

    L = CHUNK
    row = lax.broadcasted_iota(jnp.int32, (L, L), 0)
    col = lax.broadcasted_iota(jnp.int32, (L, L), 1)
    ones = jnp.ones((L, ML_DIM), BF16)
    bias = bias_ref[...]

    all_spans = [[slice(sub * L, (sub + 1) * L) for sub in (t, cps - 1 - t)] for t in range(cps)]

    items = [(d, h) for d in range(2) for h in range(ML_HEADS)]
    m_refs = ((mqf, mktf, mvf, hf_ref), (mqb, mktb, mvb, hb_ref))
    r_refs = ((rqf, rktf, rvf, yf_ref), (rqb, rktb, rvb, yb_ref))

    def m_operands(spans, d, h):
        q_ref, kt_ref, v_ref, _ = m_refs[d]
        sl = slice(h * ML_DIM, (h + 1) * ML_DIM)
        return q_ref[spans[d], sl], kt_ref[sl, spans[d]], v_ref[spans[d], sl]

    def r_operands(spans, d, h):
        q_ref, kt_ref, v_ref, _ = r_refs[d]
        return (q_ref[spans[d], h * RET_QK:(h + 1) * RET_QK], kt_ref[h * RET_QK:(h + 1) * RET_QK, spans[d]],
                v_ref[spans[d], h * RET_V:(h + 1) * RET_V])

    g_ts = [[g_ref[:, spans[d]] + bias for d, g_ref in enumerate((gf, gb))] for spans in all_spans]
    cums = [[_mlstm_cumsum(g_ts[t][d], tri_ref[1 - d]) for d in range(2)] for t in range(cps)]

    m_raw = [[jnp.dot(*m_operands(spans, d, h)[:2], preferred_element_type=F32) for d, h in items]
             for spans in all_spans]
    r_raw = [[jnp.dot(*r_operands(spans, d, h)[:2], preferred_element_type=F32) for d, h in items]
             for spans in all_spans]

    all_gates, all_bcast = [], []
    for t in range(cps):
        gates = [_mlstm_gates(g_ts[t][d], *cums[t][d]) for d in range(2)]
        x_rows = jnp.concatenate([part for g in gates for part in _split3(g[0])], axis=0)
        all_gates.append(gates)
        all_bcast.append(lax.dot_general(x_rows, sel_ref[...], _TN, preferred_element_type=F32))

    _mla_project(*mla_refs)

    m_cur = [m_ref[st:st + 1, :] for st in range(2 * ML_HEADS)]
    m_at, keep_at, gain_at = [], [], []
    for gates in all_gates:
        m_at.append(list(m_cur))
        keeps, gains = [], []
        for d, h in items:
            cf, st = 8 * d + 4 + h, ML_HEADS * d + h
            tot_t, mloc_t = gates[d][2], gates[d][3]
            blast = jnp.broadcast_to(tot_t[cf:cf + 1, :], (1, LANE))
            m_loc = jnp.broadcast_to(mloc_t[cf:cf + 1, :], (1, LANE))
            m_new = jnp.maximum(blast + m_cur[st], m_loc)
            keeps.append(jnp.exp(blast + m_cur[st] - m_new))
            gains.append(jnp.exp(m_loc - m_new))
            m_cur[st] = m_new
        keep_at.append(keeps)
        gain_at.append(gains)
    for st in range(2 * ML_HEADS):
        m_ref[st:st + 1, :] = m_cur[st]

    m_s, m_kw, m_winter, m_floor, r_s, r_kw = [], [], [], [], [], []
    for t, spans in enumerate(all_spans):
        ms_t, mkw_t, mwi_t, mfl_t, rs_t, rkw_t = [], [], [], [], [], []
        for j, (d, h) in enumerate(items):
            cf, st = 8 * d + 4 + h, ML_HEADS * d + h
            _, r_t, _, _, e_t = all_gates[t][d]
            mask = (row >= col) if d == 0 else (row <= col)
            bc = all_bcast[t][:, st * LANE:(st + 1) * LANE]
            dlog = jnp.where(mask, bc + r_t[cf:cf + 1, :], -jnp.inf)
            inter_log = bc + m_at[t][st]
            m_t = jnp.maximum(jnp.max(dlog, axis=1, keepdims=True), inter_log)
            ms_t.append((m_raw[t][j] * jnp.exp(dlog - m_t)).astype(BF16))
            mwi_t.append(jnp.exp(inter_log - m_t))
            mfl_t.append(jnp.exp(-m_t))
            kt = m_operands(spans, d, h)[1]
            mkw_t.append((kt.astype(F32) * e_t[cf:cf + 1, :]).astype(BF16))
            rs_t.append((r_raw[t][j] * decay_ref[st]).astype(BF16))
            rkt = r_operands(spans, d, h)[1]
            rkw_t.append((rkt.astype(F32) * zeta_ref[st:st + 1, :]).astype(BF16))
        m_s.append(ms_t), m_kw.append(mkw_t), m_winter.append(mwi_t), m_floor.append(mfl_t)
        r_s.append(rs_t), r_kw.append(rkw_t)

    c_cur = [c_ref[st] for st in range(2 * ML_HEADS)]
    r_cur = [r_ref[st] for st in range(2 * RET_HEADS)]
    for t, spans in enumerate(all_spans):
        v1s = [jnp.concatenate([m_operands(spans, d, h)[2], ones], axis=1) for d, h in items]
        intra = [jnp.dot(m_s[t][j], v1s[j], preferred_element_type=F32) for j in range(len(items))]
        inter = [jnp.dot(m_operands(spans, d, h)[0], c_cur[ML_HEADS * d + h].astype(BF16),
                         preferred_element_type=F32) for d, h in items]
        c_loc = [jnp.dot(m_kw[t][j], v1s[j], preferred_element_type=F32) for j in range(len(items))]
        ry_intra = [jnp.dot(r_s[t][j], r_operands(spans, d, h)[2], preferred_element_type=F32)
                    for j, (d, h) in enumerate(items)]
        ry_inter = [jnp.dot(r_operands(spans, d, h)[0], r_cur[RET_HEADS * d + h].astype(BF16),
                            preferred_element_type=F32) for d, h in items]
        r_loc = [jnp.dot(r_kw[t][j], r_operands(spans, d, h)[2], preferred_element_type=F32)
                 for j, (d, h) in enumerate(items)]
        for j, (d, h) in enumerate(items):
            st = ML_HEADS * d + h
            w_inter = m_winter[t][j]
            num = intra[j][:, :ML_DIM] + w_inter * inter[j][:, :ML_DIM]
            den = intra[j][:, ML_DIM:] + w_inter * inter[j][:, ML_DIM:]
            m_refs[d][3][spans[d], h * ML_DIM:(h + 1) * ML_DIM] = (
                num / jnp.maximum(jnp.abs(den), m_floor[t][j])).astype(BF16)
            keep, gain = keep_at[t][j], gain_at[t][j]
            c_cur[st] = (jnp.concatenate([keep, keep], axis=1) * c_cur[st]
                         + jnp.concatenate([gain, gain], axis=1) * c_loc[j])
            r_refs[d][3][spans[d], h * RET_V:(h + 1) * RET_V] = (
                ry_intra[j] + ry_inter[j] * inner_ref[st]).astype(BF16)
            r_cur[st] = chunk_decay[st] * r_cur[st] + r_loc[j]
    for st in range(2 * ML_HEADS):
        c_ref[st] = c_cur[st]
        r_ref[st] = r_cur[st]


def _gate_select_matrix():
    sel = np.zeros((2 * 3 * GATE_ROWS, 2 * ML_HEADS * LANE), np.float32)
    for d in range(2):
        for part in range(3):
            for h in range(ML_HEADS):
                st = ML_HEADS * d + h
                sel[(3 * d + part) * GATE_ROWS + 8 * d + 4 + h, st * LANE:(st + 1) * LANE] = 1.0
    return jnp.asarray(sel, BF16)


def _mixers(mq, mkt, rq, rkt, proj, gates_t, bias_t, layer, tri, sel, ret_consts, g_q, g_kv, w_q, w_kv, cos4, sin4,
            *, cps):
    S = proj.shape[0]
    q_rank, kv_rank = w_q.shape[1], w_kv.shape[1]
    rows = cps * CHUNK
    nb = S // rows
    MW = ML_HEADS * ML_DIM
    RQ = RET_HEADS * RET_QK
    RV = RET_HEADS * RET_V
    decay, inner, zeta, chunk_decay = ret_consts
    fwd = lambda n: n
    bwd = lambda n: nb - 1 - n

    def specs(idx):
        return [
            pl.BlockSpec((rows, MW), lambda n: (idx(n), 0)),
            pl.BlockSpec((MW, rows), lambda n: (0, idx(n))),
            pl.BlockSpec((rows, MW), lambda n: (idx(n), COL_ML_V * LANE // MW)),
            pl.BlockSpec((GATE_ROWS, rows), lambda n: (0, idx(n))),
            pl.BlockSpec((rows, RQ), lambda n: (idx(n), 0)),
            pl.BlockSpec((RQ, rows), lambda n: (0, idx(n))),
            pl.BlockSpec((rows, RV), lambda n: (idx(n), COL_RV * LANE // RV)),
        ]

    whole = lambda a: pl.BlockSpec(a.shape, lambda n: (0,) * a.ndim)
    out = lambda idx, w: pl.BlockSpec((rows, w), lambda n: (idx(n), 0))
    operands = (mq, mkt, proj, gates_t, rq, rkt, proj)
    return pl.pallas_call(
        functools.partial(_mixers_kernel, chunk_decay=chunk_decay, cps=cps),
        name="mixers",
        grid=(nb,),
        in_specs=specs(fwd) + specs(bwd) + [
            pl.BlockSpec((None, GATE_ROWS, CHUNK), lambda n: (layer, 0, 0)),
            whole(tri), whole(sel), whole(decay), whole(inner), whole(zeta),
            pl.BlockSpec((rows, q_rank), lambda n: (n, COL_CQ * LANE // q_rank)),
            pl.BlockSpec((rows, kv_rank), lambda n: (n, COL_CKV * LANE // kv_rank)),
            pl.BlockSpec((rows, LANE), lambda n: (n, COL_KROPE)),
            pl.BlockSpec((None, 1, q_rank), lambda n: (layer, 0, 0)),
            pl.BlockSpec((None, 1, kv_rank), lambda n: (layer, 0, 0)),
            pl.BlockSpec((None, q_rank, w_q.shape[2]), lambda n: (layer, 0, 0)),
            pl.BlockSpec((None, kv_rank, w_kv.shape[2]), lambda n: (layer, 0, 0)),
            pl.BlockSpec((rows, 2 * LANE), lambda n: (n, 0)),
            pl.BlockSpec((rows, 2 * LANE), lambda n: (n, 0)),
        ],
        out_specs=[out(fwd, MW), out(bwd, MW), out(fwd, RV), out(bwd, RV),
                   pl.BlockSpec((MLA_HEADS, rows, MLA_QK), lambda n: (0, n, 0)),
                   pl.BlockSpec((MLA_HEADS, rows, MLA_QK), lambda n: (0, n, 0)),
                   pl.BlockSpec((MLA_HEADS, rows, 2 * MLA_V), lambda n: (0, n, 0))],
        out_shape=[jax.ShapeDtypeStruct((S, MW), BF16), jax.ShapeDtypeStruct((S, MW), BF16),
                   jax.ShapeDtypeStruct((S, RV), BF16), jax.ShapeDtypeStruct((S, RV), BF16),
                   jax.ShapeDtypeStruct((MLA_HEADS, S, MLA_QK), BF16),
                   jax.ShapeDtypeStruct((MLA_HEADS, S, MLA_QK), BF16),
                   jax.ShapeDtypeStruct((MLA_HEADS, S, 2 * MLA_V), BF16)],
        scratch_shapes=[pltpu.VMEM((2 * ML_HEADS, ML_DIM, 2 * ML_DIM), F32), pltpu.VMEM((2 * ML_HEADS, LANE), F32),
                        pltpu.VMEM((2 * RET_HEADS, RET_QK, RET_V), F32)],
        compiler_params=_params(("arbitrary",)),
    )(*operands, *operands, bias_t, tri, sel, decay, inner, zeta, proj, proj, proj, g_q, g_kv, w_q, w_kv, cos4, sin4)


def _retention_consts():
    log_gamma = jnp.log1p(-jnp.exp2(-RET_DECAY_BASE - jnp.arange(RET_HEADS, dtype=F32)))
    idx = jnp.arange(CHUNK, dtype=F32)
    diff = idx[:, None] - idx[None, :]
    decay, inner, zeta = [], [], []
    for d in range(2):
        lg_d = log_gamma if d == 0 else log_gamma[::-1]
        dd = diff if d == 0 else -diff
        pos = idx if d == 0 else (CHUNK - 1.0 - idx)
        keep = dd >= 0
        for h in range(RET_HEADS):
            lg = lg_d[h]
            decay.append(jnp.where(keep, jnp.exp(jnp.where(keep, dd, 0.0) * lg), 0.0))
            inner.append(jnp.broadcast_to(jnp.exp((pos + 1.0) * lg)[:, None], (CHUNK, LANE)))
            zeta.append(jnp.exp((CHUNK - 1.0 - pos) * lg))
    lg_np = np.log1p(-np.exp2(-RET_DECAY_BASE - np.arange(RET_HEADS, dtype=np.float32))).astype(np.float32)
    chunk_decay = tuple(float(np.exp(np.float32(CHUNK) * lg)) for lg in list(lg_np) + list(lg_np[::-1]))
    return jnp.stack(decay), jnp.stack(inner), jnp.stack(zeta), chunk_decay


def _mla_project(cq_ref, ckv_ref, kr_ref, gq_ref, gkv_ref, wq_ref, wkv_ref, cos_ref, sin_ref,
                 q_out, k_out, v_out):
    tm = cq_ref.shape[0]
    cos4 = cos_ref[...]
    sin4 = sin_ref[...]
    qn = _rmsnorm(cq_ref[...].astype(F32), gq_ref[...]).astype(BF16)
    kvn = _rmsnorm(ckv_ref[...].astype(F32), gkv_ref[...]).astype(BF16)
    q_all = jnp.dot(qn, wq_ref[...], preferred_element_type=F32)
    kv_all = jnp.dot(kvn, wkv_ref[...], preferred_element_type=F32)
    q_scale = (MLA_QK ** -0.5) * math.log2(math.e)
    nope_w = MLA_HEADS * MLA_NOPE
    q_rope = _rope_grouped(q_all[:, nope_w:], jnp.concatenate([cos4, cos4], axis=1),
                           jnp.concatenate([sin4, sin4], axis=1)) * q_scale
    k_rope = _rope_grouped(kr_ref[...].astype(F32), cos4[:, :LANE], sin4[:, :LANE])[:, :ROPE_DIM].astype(BF16)
    ones = jnp.ones((tm, MLA_V), BF16)
    for h in range(MLA_HEADS):
        q_out[h, :, :MLA_NOPE] = (q_all[:, h * MLA_NOPE:(h + 1) * MLA_NOPE] * q_scale).astype(BF16)
        q_out[h, :, MLA_NOPE:] = q_rope[:, h * ROPE_DIM:(h + 1) * ROPE_DIM].astype(BF16)
        kv0 = h * (MLA_NOPE + MLA_V)
        k_out[h, :, :MLA_NOPE] = kv_all[:, kv0:kv0 + MLA_NOPE].astype(BF16)
        k_out[h, :, MLA_NOPE:] = k_rope
        v_out[h, :, :MLA_V] = kv_all[:, kv0 + MLA_NOPE:kv0 + MLA_NOPE + MLA_V].astype(BF16)
        v_out[h, :, MLA_V:] = ones


def _pack_w_in_block(w_ref, o_ref):
    n_in, tc = w_ref.shape
    ml_end = 4 * ML_HEADS * ML_DIM
    gates = 4 * ML_HEADS
    rest = n_in - ml_end - gates
    gate_row = IN_WIDTH_PACKED - LANE
    assert ml_end + rest + (LANE - ROPE_DIM) == gate_row
    o_ref[:ml_end, :] = w_ref[:ml_end, :].astype(BF16)
    o_ref[ml_end:ml_end + rest, :] = w_ref[ml_end + gates:, :].astype(BF16)
    o_ref[ml_end + rest:gate_row, :] = jnp.zeros((LANE - ROPE_DIM, tc), BF16)
    o_ref[gate_row:gate_row + gates, :] = w_ref[ml_end:ml_end + gates, :].astype(BF16)
    o_ref[gate_row + gates:, :] = jnp.zeros((LANE - gates, tc), BF16)


def _pack_w_in(w_in_t, n_layers, *, tc):
    _, n_in, K = w_in_t.shape
    return pl.pallas_call(
        _pack_w_in_block,
        name="pack_w_in",
        grid=(n_layers, K // tc),
        in_specs=[pl.BlockSpec((None, n_in, tc), lambda l, i: (l, 0, i))],
        out_specs=pl.BlockSpec((None, IN_WIDTH_PACKED, tc), lambda l, i: (l, 0, i)),
        out_shape=jax.ShapeDtypeStruct((n_layers, IN_WIDTH_PACKED, K), BF16),
        compiler_params=_params(("parallel", "parallel")),
    )(w_in_t)


def _attn_kernel(q_ref, k_ref, v_ref, *rest, tk, nk, n_cast, pack):
    n_in = n_cast + pack
    w_refs, o_ref, wb_refs = rest[:n_in], rest[n_in], rest[n_in + 1:]
    for w_ref, wb_ref in zip(w_refs[:n_cast], wb_refs[:n_cast]):
        wb_ref[...] = w_ref[...].astype(wb_ref.dtype)
    if pack:
        _pack_w_in_block(w_refs[n_cast], wb_refs[n_cast])

    q = q_ref[...]
    m = acc = None
    for c in range(nk):
        k = k_ref[c * tk:(c + 1) * tk, :]
        v = v_ref[c * tk:(c + 1) * tk, :]
        s = lax.dot_general(q, k, _NT, preferred_element_type=F32)
        m_c = jnp.max(s, axis=-1, keepdims=True)
        if c == 0:
            m = m_c
            acc = jnp.dot(jnp.exp2(s - m).astype(BF16), v, preferred_element_type=F32)
        else:
            m_new = jnp.maximum(m, m_c)
            acc = jnp.exp2(m - m_new) * acc + jnp.dot(jnp.exp2(s - m_new).astype(BF16), v, preferred_element_type=F32)
            m = m_new
    o_ref[...] = (acc[:, :MLA_V] / acc[:, MLA_V:]).astype(o_ref.dtype)


def _attention_steps(S, tq):
    return MLA_HEADS * (S // tq)


def _attention(q, k, v, cast_weights, layer, pack_src, *, tq, tk, tc):
    S = q.shape[1]
    nq = S // tq
    n_steps = _attention_steps(S, tq)
    step = lambda h, i: h * nq + i
    w_specs, wb_specs, wb_shapes = [], [], []
    for w in cast_weights:
        _, rows, cols = w.shape
        rb = rows // n_steps
        assert rb * n_steps == rows and rb % BF16_ROWS == 0
        w_specs.append(pl.BlockSpec((None, rb, cols), lambda h, i: (layer, step(h, i), 0)))
        wb_specs.append(pl.BlockSpec((rb, cols), lambda h, i: (step(h, i), 0)))
        wb_shapes.append(jax.ShapeDtypeStruct((rows, cols), BF16))
    operands = list(cast_weights)
    if pack_src is not None:
        L, n_in, K = pack_src.shape
        kb = K // tc
        n_blocks = (L - 1) * kb
        assert n_blocks <= n_steps
        blk = lambda h, i: jnp.minimum(step(h, i), n_blocks - 1)
        w_specs.append(pl.BlockSpec((None, n_in, tc), lambda h, i: (1 + blk(h, i) // kb, 0, blk(h, i) % kb)))
        wb_specs.append(pl.BlockSpec((None, IN_WIDTH_PACKED, tc), lambda h, i: (blk(h, i) // kb, 0, blk(h, i) % kb)))
        wb_shapes.append(jax.ShapeDtypeStruct((L - 1, IN_WIDTH_PACKED, K), BF16))
        operands.append(pack_src)
    outs = pl.pallas_call(
        functools.partial(_attn_kernel, tk=tk, nk=S // tk, n_cast=len(cast_weights), pack=pack_src is not None),
        name="attention",
        grid=(MLA_HEADS, nq),
        in_specs=[
            pl.BlockSpec((None, tq, MLA_QK), lambda h, i: (h, i, 0)),
            pl.BlockSpec((None, S, MLA_QK), lambda h, i: (h, 0, 0)),
            pl.BlockSpec((None, S, 2 * MLA_V), lambda h, i: (h, 0, 0)),
        ] + w_specs,
        out_specs=[pl.BlockSpec((tq, MLA_V), lambda h, i: (i, h))] + wb_specs,
        out_shape=[jax.ShapeDtypeStruct((S, MLA_HEADS * MLA_V), BF16)] + wb_shapes,
        compiler_params=_params(("arbitrary", "arbitrary")),
    )(q, k, v, *operands)
    return outs[0], outs[1:]


def _tiles(S, D, F):
    cap = lambda n: min(n, S)
    return dict(
        row=cap(512),
        row_big=cap(1024),
        mixer_cps=4 if S % (4 * CHUNK) == 0 else 1,
        attn_tq=cap(2048), attn_tk=cap(256),
        ffn1_tm=cap(2048), ffn1_tn=min(1024, F),
        ffn2_tn=min(1024, D), ffn2_tk=min(4096, F),
        ffn2_final_tk=min(1024, F),
        pack_tc=min(256, D),
    )


def kernel(x, positions, g_mix, w_in, b_gates, w_conv, g_ml_out, g_ret_out, g_q_norm, w_q_up, g_kv_norm, w_kv_up,
           w_out, g_ffn, w_ff1, w_ff2, g_final):
    B, S, D = x.shape
    assert B == 1
    depth = w_in.shape[0]
    t = _tiles(S, D, w_ff1.shape[-1])

    half = ROPE_DIM // 2
    inv = ROPE_THETA ** (-jnp.arange(half, dtype=F32) / half)
    ang = positions[0].astype(F32)[:, None] * inv
    cos4 = jnp.tile(jnp.concatenate([jnp.cos(ang), jnp.cos(ang)], axis=-1), (1, RET_HEADS))
    sin4 = jnp.tile(jnp.concatenate([-jnp.sin(ang), jnp.sin(ang)], axis=-1), (1, RET_HEADS))

    w_in_t = jnp.swapaxes(w_in, 1, 2)
    ride_pack = depth > 1 and (depth - 1) * (D // t["pack_tc"]) <= _attention_steps(S, t["attn_tq"])
    w_in_first = _pack_w_in(w_in_t, 1 if ride_pack else depth, tc=t["pack_tc"])
    w_in_rest = None
    q_rank = w_q_up.shape[1]
    w_q4 = w_q_up.astype(BF16).reshape(depth, q_rank, MLA_HEADS, MLA_QK)
    w_q = jnp.concatenate([w_q4[..., :MLA_NOPE].reshape(depth, q_rank, MLA_HEADS * MLA_NOPE),
                           w_q4[..., MLA_NOPE:].reshape(depth, q_rank, MLA_HEADS * ROPE_DIM)], axis=-1)
    w_kv = w_kv_up.astype(BF16)
    bias_t = jnp.broadcast_to(b_gates[:, :, None], (depth, GATE_ROWS, CHUNK))
    idx = jnp.arange(CHUNK)
    tri = jnp.stack([idx[:, None] >= idx[None, :], idx[:, None] <= idx[None, :]]).astype(BF16)
    sel = _gate_select_matrix()
    ret_consts = _retention_consts()
    r3 = lambda g: g[:, None, :]

    xs = x[0]
    for l in range(depth):
        w_in_l, l_in = (w_in_first, l) if (l == 0 or not ride_pack) else (w_in_rest, l - 1)
        proj, gates_t, mq, mkt, rq, rkt = _in_proj(xs, r3(g_mix), w_in_l, w_conv, cos4, sin4, l, l_in,
                                                   tm=t["row_big"])
        h_f, h_b, y_f, y_b, q, k, v = _mixers(mq, mkt, rq, rkt, proj, gates_t, bias_t, l, tri, sel, ret_consts,
                                              r3(g_q_norm), r3(g_kv_norm), w_q, w_kv, cos4, sin4, cps=t["mixer_cps"])
        y_mla, prepared = _attention(q, k, v, (w_out, w_ff1, w_ff2), l, w_in_t if (ride_pack and l == 0) else None,
                                     tq=t["attn_tq"], tk=t["attn_tk"], tc=t["pack_tc"])
        w_out_b, w_ff1_b, w_ff2_b = prepared[:3]
        if ride_pack and l == 0:
            w_in_rest = prepared[3]
        xs, hn = _out_proj(h_f, h_b, y_f, y_b, proj, y_mla, r3(g_ml_out), r3(g_ret_out), r3(g_ffn), l, w_out_b, xs,
                           tm=t["row"])
        act = _ffn1(hn, w_ff1_b, tm=t["ffn1_tm"], tn=t["ffn1_tn"])
        if l < depth - 1:
            xs = _ffn2(act, w_ff2_b, xs, g_final[None, :], tm=t["row_big"], tn=t["ffn2_tn"], tk=t["ffn2_tk"],
                       final=False)
        else:
            xs = _ffn2(act, w_ff2_b, xs, g_final[None, :], tm=t["row_big"], tn=D, tk=t["ffn2_final_tk"], final=True)
    return xs[None]
```

```python
import functools
import math

import numpy as np
import jax
import jax.numpy as jnp
from jax import lax
from jax.experimental import pallas as pl
from jax.experimental.pallas import tpu as pltpu

F32 = jnp.float32
BF16 = jnp.bfloat16

EPS = 1e-6
NEG_INIT = -1e30
ROPE_THETA = 10000.0
CHUNK = 128
LANE = 128
BF16_ROWS = 16
ML_HEADS = 4
ML_DIM = 128
RET_HEADS = 4
RET_QK = 64
RET_V = 128
RET_DECAY_BASE = 5.0
MLA_HEADS = 8
MLA_NOPE = 128
ROPE_DIM = 64
MLA_V = 128
MLA_QK = MLA_NOPE + ROPE_DIM
VMEM_LIMIT = 60 * 1024 * 1024

COL_ML_Q, COL_ML_K, COL_ML_V, COL_ML_O = 0, 4, 8, 12
COL_RQK, COL_RV, COL_RG = 16, 20, 24
COL_CQ, COL_CKV, COL_KROPE, COL_GATES = 28, 32, 34, 35
IN_WIDTH_PACKED = 36 * LANE

_NT = (((1,), (1,)), ((), ()))
_TN = (((0,), (0,)), ((), ()))


def _params(sem):
    return pltpu.CompilerParams(dimension_semantics=sem, vmem_limit_bytes=VMEM_LIMIT)


def _rmsnorm(x, g):
    return x * lax.rsqrt(jnp.mean(x * x, axis=-1, keepdims=True) + EPS) * g


def _rope_grouped(t, cosf, sinf):
    width = t.shape[1]
    lane = lax.broadcasted_iota(jnp.int32, t.shape, 1)
    first_half = (lane % ROPE_DIM) < (ROPE_DIM // 2)
    partner = jnp.where(first_half, pltpu.roll(t, width - ROPE_DIM // 2, 1), pltpu.roll(t, ROPE_DIM // 2, 1))
    return t * cosf + partner * sinf


IN_TN = 12 * LANE
MQK_W = 2 * ML_HEADS * ML_DIM
RQK_W = 2 * RET_HEADS * RET_QK
RQK_OFS = COL_RQK * LANE - IN_TN


def _in_proj_kernel(x_ref, xprev_ref, xnext_ref, g_ref, w_ref, wc_ref, cos_ref, sin_ref,
                    o_ref, gates_ref, mq_out, mkt_out, rq_out, rkt_out, hn_ref, *, nblk):
    i, j = pl.program_id(0), pl.program_id(1)
    tm = x_ref.shape[0]

    def project(hn):
        y = lax.dot_general(hn, w_ref[...], _NT, preferred_element_type=F32)
        o_ref[...] = y.astype(o_ref.dtype)
        return y

    @pl.when(j == 0)
    def _():
        g = g_ref[...]
        hn = _rmsnorm(x_ref[...], g).astype(hn_ref.dtype)
        hn_ref[...] = hn
        halo = jnp.concatenate([_rmsnorm(xprev_ref[...], g), _rmsnorm(xnext_ref[...], g)], axis=0).astype(BF16)
        y = project(hn)[:, :MQK_W]
        yh = lax.dot_general(halo, w_ref[:MQK_W, :], _NT, preferred_element_type=F32)
        prev_row = jnp.where(i > 0, yh[7:8, :], 0.0)
        next_row = jnp.where(i < nblk - 1, yh[8:9, :], 0.0)
        wc = wc_ref[...]
        w0, w1, w2 = wc[0:1, :], wc[1:2, :], wc[2:3, :]
        c = pltpu.roll(y, 1, 0) * w0 + y * w1 + pltpu.roll(y, tm - 1, 0) * w2
        c_first = prev_row * w0 + y[0:1, :] * w1 + y[1:2, :] * w2
        c_last = y[tm - 2:tm - 1, :] * w0 + y[tm - 1:tm, :] * w1 + next_row * w2
        row8 = lax.broadcasted_iota(jnp.int32, (8, MQK_W), 0)
        c = jnp.concatenate([jnp.where(row8 == 0, c_first, c[0:8, :]), c[8:tm - 8, :],
                             jnp.where(row8 == 7, c_last, c[tm - 8:, :])], axis=0)
        c = c / (1.0 + jnp.exp(-c))
        half = MQK_W // 2
        mq_out[...] = (c[:, :half] * (ML_DIM ** -0.5)).astype(mq_out.dtype)
        mkt_out[...] = c[:, half:].T.astype(mkt_out.dtype)

    @pl.when(j == 1)
    def _():
        y = project(hn_ref[...])
        cosf, sinf = cos_ref[...], sin_ref[...]
        half = RQK_W // 2
        rq_out[...] = _rope_grouped(y[:, RQK_OFS:RQK_OFS + half], cosf, sinf).astype(rq_out.dtype)
        rk = _rope_grouped(y[:, RQK_OFS + half:RQK_OFS + RQK_W], cosf, sinf) * (RET_QK ** -0.5)
        rkt_out[...] = rk.T.astype(rkt_out.dtype)

    @pl.when(j == 2)
    def _():
        y = project(hn_ref[...])
        gates_ref[...] = y[:, y.shape[1] - LANE:].T


def _in_proj(x, g, w, w_conv, cos4, sin4, layer, w_layer, *, tm):
    S, K = x.shape
    N = w.shape[1]
    assert N == 3 * IN_TN and MQK_W <= IN_TN and 0 <= RQK_OFS and RQK_OFS + RQK_W <= IN_TN
    nblk = S // tm
    hb = tm // 8
    row_blk = lambda width: pl.BlockSpec((tm, width), lambda i, j: (i, 0))
    col_blk = lambda height: pl.BlockSpec((height, tm), lambda i, j: (0, i))
    return pl.pallas_call(
        functools.partial(_in_proj_kernel, nblk=nblk),
        name="in_proj",
        grid=(nblk, N // IN_TN),
        in_specs=[
            row_blk(K),
            pl.BlockSpec((8, K), lambda i, j: (jnp.maximum(i * hb - 1, 0), 0)),
            pl.BlockSpec((8, K), lambda i, j: (jnp.minimum((i + 1) * hb, S // 8 - 1), 0)),
            pl.BlockSpec((None, 1, K), lambda i, j: (layer, 0, 0)),
            pl.BlockSpec((None, IN_TN, K), lambda i, j: (w_layer, j, 0)),
            pl.BlockSpec((None, 3, MQK_W), lambda i, j: (layer, 0, 0)),
            row_blk(RQK_W // 2), row_blk(RQK_W // 2),
        ],
        out_specs=[pl.BlockSpec((tm, IN_TN), lambda i, j: (i, j)), col_blk(LANE),
                   row_blk(MQK_W // 2), col_blk(MQK_W // 2), row_blk(RQK_W // 2), col_blk(RQK_W // 2)],
        out_shape=[jax.ShapeDtypeStruct((S, N), BF16), jax.ShapeDtypeStruct((LANE, S), F32),
                   jax.ShapeDtypeStruct((S, MQK_W // 2), BF16), jax.ShapeDtypeStruct((MQK_W // 2, S), BF16),
                   jax.ShapeDtypeStruct((S, RQK_W // 2), BF16), jax.ShapeDtypeStruct((RQK_W // 2, S), BF16)],
        scratch_shapes=[pltpu.VMEM((tm, K), BF16)],
        compiler_params=_params(("parallel", "arbitrary")),
    )(x, x, x, g, w, w_conv, cos4, sin4)


def _out_proj_kernel(hf, hb, o_ref, gml_ref, yf, yb, rg_ref, gret_ref, ymla_ref, w_ref, x_ref, gffn_ref,
                     out_ref, hn_out):
    w_ml = ML_HEADS * ML_DIM
    w_ret = RET_HEADS * RET_V
    acc = x_ref[...] + jnp.dot(ymla_ref[...], w_ref[w_ml + w_ret:, :], preferred_element_type=F32)
    y_ml, y_ret = [], []
    for h in range(ML_HEADS):
        sl = slice(h * ML_DIM, (h + 1) * ML_DIM)
        y = _rmsnorm(hf[:, sl].astype(F32) + hb[:, sl].astype(F32), gml_ref[:, sl])
        y_ml.append((y / (1.0 + jnp.exp(-o_ref[:, sl].astype(F32)))).astype(BF16))
    for h in range(RET_HEADS):
        sl = slice(h * RET_V, (h + 1) * RET_V)
        y = _rmsnorm(yf[:, sl].astype(F32) + yb[:, sl].astype(F32), gret_ref[:, sl])
        gate = rg_ref[:, sl].astype(F32)
        y_ret.append((gate / (1.0 + jnp.exp(-gate)) * y).astype(BF16))
    acc = acc + jnp.dot(jnp.concatenate(y_ml, axis=1), w_ref[:w_ml, :], preferred_element_type=F32)
    acc = acc + jnp.dot(jnp.concatenate(y_ret, axis=1), w_ref[w_ml:w_ml + w_ret, :], preferred_element_type=F32)
    out_ref[...] = acc
    hn_out[...] = _rmsnorm(acc, gffn_ref[...]).astype(hn_out.dtype)


def _out_proj(h_f, h_b, y_f, y_b, proj, y_mla, g_ml, g_ret, g_ffn, layer, w, x, *, tm):
    S, N = x.shape
    W = ML_HEADS * ML_DIM
    blk = lambda c: pl.BlockSpec((tm, W), lambda i: (i, c))
    gain = pl.BlockSpec((None, 1, W), lambda i: (layer, 0, 0))
    return pl.pallas_call(
        _out_proj_kernel,
        name="out_proj",
        grid=(S // tm,),
        in_specs=[blk(0), blk(0), blk(COL_ML_O * LANE // W), gain, blk(0), blk(0), blk(COL_RG * LANE // W), gain,
                  pl.BlockSpec((tm, y_mla.shape[1]), lambda i: (i, 0)),
                  pl.BlockSpec(w.shape, lambda i: (0, 0)),
                  pl.BlockSpec((tm, N), lambda i: (i, 0)),
                  pl.BlockSpec((None, 1, N), lambda i: (layer, 0, 0))],
        out_specs=[pl.BlockSpec((tm, N), lambda i: (i, 0)), pl.BlockSpec((tm, N), lambda i: (i, 0))],
        out_shape=[jax.ShapeDtypeStruct((S, N), F32), jax.ShapeDtypeStruct((S, N), BF16)],
        compiler_params=_params(("parallel",)),
    )(h_f, h_b, proj, g_ml, y_f, y_b, proj, g_ret, y_mla, w, x, g_ffn)


def _ffn1_kernel(hn_ref, w_ref, o_ref):
    y = jnp.dot(hn_ref[...], w_ref[...], preferred_element_type=F32)
    o_ref[...] = jnp.square(jnp.maximum(y, 0.0)).astype(o_ref.dtype)


def _ffn1(hn, w, *, tm, tn):
    S, K = hn.shape
    N = w.shape[-1]
    return pl.pallas_call(
        _ffn1_kernel,
        name="ffn1",
        grid=(S // tm, N // tn),
        in_specs=[pl.BlockSpec((tm, K), lambda i, j: (i, 0)), pl.BlockSpec((K, tn), lambda i, j: (0, j))],
        out_specs=pl.BlockSpec((tm, tn), lambda i, j: (i, j)),
        out_shape=jax.ShapeDtypeStruct((S, N), BF16),
        compiler_params=_params(("parallel", "arbitrary")),
    )(hn, w)


def _ffn2_kernel(a_ref, w_ref, x_ref, gf_ref, o_ref, *, final):
    k = pl.program_id(2)

    @pl.when(k == 0)
    def _():
        o_ref[...] = x_ref[...]

    o_ref[...] += jnp.dot(a_ref[...], w_ref[...], preferred_element_type=F32)

    if final:
        @pl.when(k == pl.num_programs(2) - 1)
        def _():
            o_ref[...] = _rmsnorm(o_ref[...], gf_ref[...])


def _ffn2(a, w, x, g_final, *, tm, tn, tk, final):
    S, N = x.shape
    K = a.shape[1]
    assert not final or tn == N
    return pl.pallas_call(
        functools.partial(_ffn2_kernel, final=final),
        name="ffn2_final" if final else "ffn2",
        grid=(S // tm, N // tn, K // tk),
        in_specs=[
            pl.BlockSpec((tm, tk), lambda i, j, k: (i, k)),
            pl.BlockSpec((tk, tn), lambda i, j, k: (k, j)),
            pl.BlockSpec((tm, tn), lambda i, j, k: (i, j)),
            pl.BlockSpec((1, tn), lambda i, j, k: (0, j)),
        ],
        out_specs=pl.BlockSpec((tm, tn), lambda i, j, k: (i, j)),
        out_shape=jax.ShapeDtypeStruct((S, N), F32),
        compiler_params=_params(("parallel", "arbitrary", "arbitrary")),
    )(a, w, x, g_final)


def _log_sigmoid(t):
    return jnp.minimum(t, 0.0) - jnp.log1p(jnp.exp(-jnp.abs(t)))


def _split3(t):
    hi = t.astype(BF16)
    r1 = t - hi.astype(F32)
    mid = r1.astype(BF16)
    lo = (r1 - mid.astype(F32)).astype(BF16)
    return hi, mid, lo


GATE_ROWS = 16


def _mlstm_cumsum(g_t, tri):
    logf_t = _log_sigmoid(g_t)
    return logf_t, jnp.dot(jnp.concatenate(_split3(logf_t), axis=0), tri, preferred_element_type=F32)


def _mlstm_gates(g_t, logf_t, cs):
    b_t = cs[0:GATE_ROWS] + cs[GATE_ROWS:2 * GATE_ROWS] + cs[2 * GATE_ROWS:]
    tot_t = jnp.sum(logf_t, axis=1, keepdims=True)
    r_t = pltpu.roll(g_t, 4, 0) - b_t
    wend_t = tot_t + r_t
    mloc_t = jnp.max(wend_t, axis=1, keepdims=True)
    e_t = jnp.exp(wend_t - mloc_t)
    return b_t, r_t, tot_t, mloc_t, e_t


def _mixers_kernel(mqf, mktf, mvf, gf, rqf, rktf, rvf, mqb, mktb, mvb, gb, rqb, rktb, rvb,
                   bias_ref, tri_ref, sel_ref, decay_ref, inner_ref, zeta_ref,
                   cq_ref, ckv_ref, kr_ref, gq_ref, gkv_ref, wq_ref, wkv_ref, cos_ref, sin_ref,
                   hf_ref, hb_ref, yf_ref, yb_ref, q_out, k_out, v_out, c_ref, m_ref, r_ref, *, chunk_decay, cps):
    mla_refs = (cq_ref, ckv_ref, kr_ref, gq_ref, gkv_ref, wq_ref, wkv_ref, cos_ref, sin_ref, q_out, k_out, v_out)

    @pl.when(pl.program_id(0) == 0)
    def _():
        c_ref[...] = jnp.zeros(c_ref.shape, F32)
        m_ref[...] = jnp.full(m_ref.shape, NEG_INIT, F32)
        r_ref[...] = jnp.zeros(r_ref.shape, F32)

    L = CHUNK
    row = lax.broadcasted_iota(jnp.int32, (L, L), 0)
    col = lax.broadcasted_iota(jnp.int32, (L, L), 1)
    ones = jnp.ones((L, ML_DIM), BF16)
    bias = bias_ref[...]

    all_spans = [[slice(sub * L, (sub + 1) * L) for sub in (t, cps - 1 - t)] for t in range(cps)]

    items = [(d, h) for d in range(2) for h in range(ML_HEADS)]
    m_refs = ((mqf, mktf, mvf, hf_ref), (mqb, mktb, mvb, hb_ref))
    r_refs = ((rqf, rktf, rvf, yf_ref), (rqb, rktb, rvb, yb_ref))

    def m_operands(spans, d, h):
        q_ref, kt_ref, v_ref, _ = m_refs[d]
        sl = slice(h * ML_DIM, (h + 1) * ML_DIM)
        return q_ref[spans[d], sl], kt_ref[sl, spans[d]], v_ref[spans[d], sl]

    def r_operands(spans, d, h):
        q_ref, kt_ref, v_ref, _ = r_refs[d]
        return (q_ref[spans[d], h * RET_QK:(h + 1) * RET_QK], kt_ref[h * RET_QK:(h + 1) * RET_QK, spans[d]],
                v_ref[spans[d], h * RET_V:(h + 1) * RET_V])

    g_ts = [[g_ref[:, spans[d]] + bias for d, g_ref in enumerate((gf, gb))] for spans in all_spans]
    cums = [[_mlstm_cumsum(g_ts[t][d], tri_ref[1 - d]) for d in range(2)] for t in range(cps)]

    m_raw = [[jnp.dot(*m_operands(spans, d, h)[:2], preferred_element_type=F32) for d, h in items]
             for spans in all_spans]
    r_raw = [[jnp.dot(*r_operands(spans, d, h)[:2], preferred_element_type=F32) for d, h in items]
             for spans in all_spans]

    all_gates, all_bcast = [], []
    for t in range(cps):
        gates = [_mlstm_gates(g_ts[t][d], *cums[t][d]) for d in range(2)]
        x_rows = jnp.concatenate([part for g in gates for part in _split3(g[0])], axis=0)
        all_gates.append(gates)
        all_bcast.append(lax.dot_general(x_rows, sel_ref[...], _TN, preferred_element_type=F32))

    _mla_project(*mla_refs)

    m_cur = [m_ref[st:st + 1, :] for st in range(2 * ML_HEADS)]
    m_at, keep_at, gain_at = [], [], []
    for gates in all_gates:
        m_at.append(list(m_cur))
        keeps, gains = [], []
        for d, h in items:
            cf, st = 8 * d + 4 + h, ML_HEADS * d + h
            tot_t, mloc_t = gates[d][2], gates[d][3]
            blast = jnp.broadcast_to(tot_t[cf:cf + 1, :], (1, LANE))
            m_loc = jnp.broadcast_to(mloc_t[cf:cf + 1, :], (1, LANE))
            m_new = jnp.maximum(blast + m_cur[st], m_loc)
            keeps.append(jnp.exp(blast + m_cur[st] - m_new))
            gains.append(jnp.exp(m_loc - m_new))
            m_cur[st] = m_new
        keep_at.append(keeps)
        gain_at.append(gains)
    for st in range(2 * ML_HEADS):
        m_ref[st:st + 1, :] = m_cur[st]

    m_s, m_kw, m_winter, m_floor, r_s, r_kw = [], [], [], [], [], []
    for t, spans in enumerate(all_spans):
        ms_t, mkw_t, mwi_t, mfl_t, rs_t, rkw_t = [], [], [], [], [], []
        for j, (d, h) in enumerate(items):
            cf, st = 8 * d + 4 + h, ML_HEADS * d + h
            _, r_t, _, _, e_t = all_gates[t][d]
            mask = (row >= col) if d == 0 else (row <= col)
            bc = all_bcast[t][:, st * LANE:(st + 1) * LANE]
            dlog = jnp.where(mask, bc + r_t[cf:cf + 1, :], -jnp.inf)
            inter_log = bc + m_at[t][st]
            m_t = jnp.maximum(jnp.max(dlog, axis=1, keepdims=True), inter_log)
            ms_t.append((m_raw[t][j] * jnp.exp(dlog - m_t)).astype(BF16))
            mwi_t.append(jnp.exp(inter_log - m_t))
            mfl_t.append(jnp.exp(-m_t))
            kt = m_operands(spans, d, h)[1]
            mkw_t.append((kt.astype(F32) * e_t[cf:cf + 1, :]).astype(BF16))
            rs_t.append((r_raw[t][j] * decay_ref[st]).astype(BF16))
            rkt = r_operands(spans, d, h)[1]
            rkw_t.append((rkt.astype(F32) * zeta_ref[st:st + 1, :]).astype(BF16))
        m_s.append(ms_t), m_kw.append(mkw_t), m_winter.append(mwi_t), m_floor.append(mfl_t)
        r_s.append(rs_t), r_kw.append(rkw_t)

    c_cur = [c_ref[st] for st in range(2 * ML_HEADS)]
    r_cur = [r_ref[st] for st in range(2 * RET_HEADS)]
    for t, spans in enumerate(all_spans):
        v1s = [jnp.concatenate([m_operands(spans, d, h)[2], ones], axis=1) for d, h in items]
        intra = [jnp.dot(m_s[t][j], v1s[j], preferred_element_type=F32) for j in range(len(items))]
        inter = [jnp.dot(m_operands(spans, d, h)[0], c_cur[ML_HEADS * d + h].astype(BF16),
                         preferred_element_type=F32) for d, h in items]
        c_loc = [jnp.dot(m_kw[t][j], v1s[j], preferred_element_type=F32) for j in range(len(items))]
        ry_intra = [jnp.dot(r_s[t][j], r_operands(spans, d, h)[2], preferred_element_type=F32)
                    for j, (d, h) in enumerate(items)]
        ry_inter = [jnp.dot(r_operands(spans, d, h)[0], r_cur[RET_HEADS * d + h].astype(BF16),
                            preferred_element_type=F32) for d, h in items]
        r_loc = [jnp.dot(r_kw[t][j], r_operands(spans, d, h)[2], preferred_element_type=F32)
                 for j, (d, h) in enumerate(items)]
        for j, (d, h) in enumerate(items):
            st = ML_HEADS * d + h
            w_inter = m_winter[t][j]
            num = intra[j][:, :ML_DIM] + w_inter * inter[j][:, :ML_DIM]
            den = intra[j][:, ML_DIM:] + w_inter * inter[j][:, ML_DIM:]
            m_refs[d][3][spans[d], h * ML_DIM:(h + 1) * ML_DIM] = (
                num / jnp.maximum(jnp.abs(den), m_floor[t][j])).astype(BF16)
            keep, gain = keep_at[t][j], gain_at[t][j]
            c_cur[st] = (jnp.concatenate([keep, keep], axis=1) * c_cur[st]
                         + jnp.concatenate([gain, gain], axis=1) * c_loc[j])
            r_refs[d][3][spans[d], h * RET_V:(h + 1) * RET_V] = (
                ry_intra[j] + ry_inter[j] * inner_ref[st]).astype(BF16)
            r_cur[st] = chunk_decay[st] * r_cur[st] + r_loc[j]
    for st in range(2 * ML_HEADS):
        c_ref[st] = c_cur[st]
        r_ref[st] = r_cur[st]


def _gate_select_matrix():
    sel = np.zeros((2 * 3 * GATE_ROWS, 2 * ML_HEADS * LANE), np.float32)
    for d in range(2):
        for part in range(3):
            for h in range(ML_HEADS):
                st = ML_HEADS * d + h
                sel[(3 * d + part) * GATE_ROWS + 8 * d + 4 + h, st * LANE:(st + 1) * LANE] = 1.0
    return jnp.asarray(sel, BF16)


def _mixers(mq, mkt, rq, rkt, proj, gates_t, bias_t, layer, tri, sel, ret_consts, g_q, g_kv, w_q, w_kv, cos4, sin4,
            *, cps):
    S = proj.shape[0]
    q_rank, kv_rank = w_q.shape[1], w_kv.shape[1]
    rows = cps * CHUNK
    nb = S // rows
    MW = ML_HEADS * ML_DIM
    RQ = RET_HEADS * RET_QK
    RV = RET_HEADS * RET_V
    decay, inner, zeta, chunk_decay = ret_consts
    fwd = lambda n: n
    bwd = lambda n: nb - 1 - n

    def specs(idx):
        return [
            pl.BlockSpec((rows, MW), lambda n: (idx(n), 0)),
            pl.BlockSpec((MW, rows), lambda n: (0, idx(n))),
            pl.BlockSpec((rows, MW), lambda n: (idx(n), COL_ML_V * LANE // MW)),
            pl.BlockSpec((GATE_ROWS, rows), lambda n: (0, idx(n))),
            pl.BlockSpec((rows, RQ), lambda n: (idx(n), 0)),
            pl.BlockSpec((RQ, rows), lambda n: (0, idx(n))),
            pl.BlockSpec((rows, RV), lambda n: (idx(n), COL_RV * LANE // RV)),
        ]

    whole = lambda a: pl.BlockSpec(a.shape, lambda n: (0,) * a.ndim)
    out = lambda idx, w: pl.BlockSpec((rows, w), lambda n: (idx(n), 0))
    operands = (mq, mkt, proj, gates_t, rq, rkt, proj)
    return pl.pallas_call(
        functools.partial(_mixers_kernel, chunk_decay=chunk_decay, cps=cps),
        name="mixers",
        grid=(nb,),
        in_specs=specs(fwd) + specs(bwd) + [
            pl.BlockSpec((None, GATE_ROWS, CHUNK), lambda n: (layer, 0, 0)),
            whole(tri), whole(sel), whole(decay), whole(inner), whole(zeta),
            pl.BlockSpec((rows, q_rank), lambda n: (n, COL_CQ * LANE // q_rank)),
            pl.BlockSpec((rows, kv_rank), lambda n: (n, COL_CKV * LANE // kv_rank)),
            pl.BlockSpec((rows, LANE), lambda n: (n, COL_KROPE)),
            pl.BlockSpec((None, 1, q_rank), lambda n: (layer, 0, 0)),
            pl.BlockSpec((None, 1, kv_rank), lambda n: (layer, 0, 0)),
            pl.BlockSpec((None, q_rank, w_q.shape[2]), lambda n: (layer, 0, 0)),
            pl.BlockSpec((None, kv_rank, w_kv.shape[2]), lambda n: (layer, 0, 0)),
            pl.BlockSpec((rows, 2 * LANE), lambda n: (n, 0)),
            pl.BlockSpec((rows, 2 * LANE), lambda n: (n, 0)),
        ],
        out_specs=[out(fwd, MW), out(bwd, MW), out(fwd, RV), out(bwd, RV),
                   pl.BlockSpec((MLA_HEADS, rows, MLA_QK), lambda n: (0, n, 0)),
                   pl.BlockSpec((MLA_HEADS, rows, MLA_QK), lambda n: (0, n, 0)),
                   pl.BlockSpec((MLA_HEADS, rows, 2 * MLA_V), lambda n: (0, n, 0))],
        out_shape=[jax.ShapeDtypeStruct((S, MW), BF16), jax.ShapeDtypeStruct((S, MW), BF16),
                   jax.ShapeDtypeStruct((S, RV), BF16), jax.ShapeDtypeStruct((S, RV), BF16),
                   jax.ShapeDtypeStruct((MLA_HEADS, S, MLA_QK), BF16),
                   jax.ShapeDtypeStruct((MLA_HEADS, S, MLA_QK), BF16),
                   jax.ShapeDtypeStruct((MLA_HEADS, S, 2 * MLA_V), BF16)],
        scratch_shapes=[pltpu.VMEM((2 * ML_HEADS, ML_DIM, 2 * ML_DIM), F32), pltpu.VMEM((2 * ML_HEADS, LANE), F32),
                        pltpu.VMEM((2 * RET_HEADS, RET_QK, RET_V), F32)],
        compiler_params=_params(("arbitrary",)),
    )(*operands, *operands, bias_t, tri, sel, decay, inner, zeta, proj, proj, proj, g_q, g_kv, w_q, w_kv, cos4, sin4)


def _retention_consts():
    log_gamma = jnp.log1p(-jnp.exp2(-RET_DECAY_BASE - jnp.arange(RET_HEADS, dtype=F32)))
    idx = jnp.arange(CHUNK, dtype=F32)
    diff = idx[:, None] - idx[None, :]
    decay, inner, zeta = [], [], []
    for d in range(2):
        lg_d = log_gamma if d == 0 else log_gamma[::-1]
        dd = diff if d == 0 else -diff
        pos = idx if d == 0 else (CHUNK - 1.0 - idx)
        keep = dd >= 0
        for h in range(RET_HEADS):
            lg = lg_d[h]
            decay.append(jnp.where(keep, jnp.exp(jnp.where(keep, dd, 0.0) * lg), 0.0))
            inner.append(jnp.broadcast_to(jnp.exp((pos + 1.0) * lg)[:, None], (CHUNK, LANE)))
            zeta.append(jnp.exp((CHUNK - 1.0 - pos) * lg))
    lg_np = np.log1p(-np.exp2(-RET_DECAY_BASE - np.arange(RET_HEADS, dtype=np.float32))).astype(np.float32)
    chunk_decay = tuple(float(np.exp(np.float32(CHUNK) * lg)) for lg in list(lg_np) + list(lg_np[::-1]))
    return jnp.stack(decay), jnp.stack(inner), jnp.stack(zeta), chunk_decay


def _mla_project(cq_ref, ckv_ref, kr_ref, gq_ref, gkv_ref, wq_ref, wkv_ref, cos_ref, sin_ref,
                 q_out, k_out, v_out):
    tm = cq_ref.shape[0]
    cos4 = cos_ref[...]
    sin4 = sin_ref[...]
    qn = _rmsnorm(cq_ref[...].astype(F32), gq_ref[...]).astype(BF16)
    kvn = _rmsnorm(ckv_ref[...].astype(F32), gkv_ref[...]).astype(BF16)
    q_all = jnp.dot(qn, wq_ref[...], preferred_element_type=F32)
    kv_all = jnp.dot(kvn, wkv_ref[...], preferred_element_type=F32)
    q_scale = (MLA_QK ** -0.5) * math.log2(math.e)
    nope_w = MLA_HEADS * MLA_NOPE
    q_rope = _rope_grouped(q_all[:, nope_w:], jnp.concatenate([cos4, cos4], axis=1),
                           jnp.concatenate([sin4, sin4], axis=1)) * q_scale
    k_rope = _rope_grouped(kr_ref[...].astype(F32), cos4[:, :LANE], sin4[:, :LANE])[:, :ROPE_DIM].astype(BF16)
    ones = jnp.ones((tm, MLA_V), BF16)
    for h in range(MLA_HEADS):
        q_out[h, :, :MLA_NOPE] = (q_all[:, h * MLA_NOPE:(h + 1) * MLA_NOPE] * q_scale).astype(BF16)
        q_out[h, :, MLA_NOPE:] = q_rope[:, h * ROPE_DIM:(h + 1) * ROPE_DIM].astype(BF16)
        kv0 = h * (MLA_NOPE + MLA_V)
        k_out[h, :, :MLA_NOPE] = kv_all[:, kv0:kv0 + MLA_NOPE].astype(BF16)
        k_out[h, :, MLA_NOPE:] = k_rope
        v_out[h, :, :MLA_V] = kv_all[:, kv0 + MLA_NOPE:kv0 + MLA_NOPE + MLA_V].astype(BF16)
        v_out[h, :, MLA_V:] = ones


def _pack_w_in_block(w_ref, o_ref):
    n_in, tc = w_ref.shape
    ml_end = 4 * ML_HEADS * ML_DIM
    gates = 4 * ML_HEADS
    rest = n_in - ml_end - gates
    gate_row = IN_WIDTH_PACKED - LANE
    assert ml_end + rest + (LANE - ROPE_DIM) == gate_row
    o_ref[:ml_end, :] = w_ref[:ml_end, :].astype(BF16)
    o_ref[ml_end:ml_end + rest, :] = w_ref[ml_end + gates:, :].astype(BF16)
    o_ref[ml_end + rest:gate_row, :] = jnp.zeros((LANE - ROPE_DIM, tc), BF16)
    o_ref[gate_row:gate_row + gates, :] = w_ref[ml_end:ml_end + gates, :].astype(BF16)
    o_ref[gate_row + gates:, :] = jnp.zeros((LANE - gates, tc), BF16)


def _pack_w_in(w_in_t, n_layers, *, tc):
    _, n_in, K = w_in_t.shape
    return pl.pallas_call(
        _pack_w_in_block,
        name="pack_w_in",
        grid=(n_layers, K // tc),
        in_specs=[pl.BlockSpec((None, n_in, tc), lambda l, i: (l, 0, i))],
        out_specs=pl.BlockSpec((None, IN_WIDTH_PACKED, tc), lambda l, i: (l, 0, i)),
        out_shape=jax.ShapeDtypeStruct((n_layers, IN_WIDTH_PACKED, K), BF16),
        compiler_params=_params(("parallel", "parallel")),
    )(w_in_t)


def _attn_kernel(q_ref, k_ref, v_ref, *rest, tk, nk, n_cast, pack):
    n_in = n_cast + pack
    w_refs, o_ref, wb_refs = rest[:n_in], rest[n_in], rest[n_in + 1:]
    for w_ref, wb_ref in zip(w_refs[:n_cast], wb_refs[:n_cast]):
        wb_ref[...] = w_ref[...].astype(wb_ref.dtype)
    if pack:
        _pack_w_in_block(w_refs[n_cast], wb_refs[n_cast])

    q = q_ref[...]
    m = acc = None
    for c in range(nk):
        k = k_ref[c * tk:(c + 1) * tk, :]
        v = v_ref[c * tk:(c + 1) * tk, :]
        s = lax.dot_general(q, k, _NT, preferred_element_type=F32)
        m_c = jnp.max(s, axis=-1, keepdims=True)
        if c == 0:
            m = m_c
            acc = jnp.dot(jnp.exp2(s - m).astype(BF16), v, preferred_element_type=F32)
        else:
            m_new = jnp.maximum(m, m_c)
            acc = jnp.exp2(m - m_new) * acc + jnp.dot(jnp.exp2(s - m_new).astype(BF16), v, preferred_element_type=F32)
            m = m_new
    o_ref[...] = (acc[:, :MLA_V] / acc[:, MLA_V:]).astype(o_ref.dtype)


def _attention_steps(S, tq):
    return MLA_HEADS * (S // tq)


def _attention(q, k, v, cast_weights, layer, pack_src, *, tq, tk, tc):
    S = q.shape[1]
    nq = S // tq
    n_steps = _attention_steps(S, tq)
    step = lambda h, i: h * nq + i
    w_specs, wb_specs, wb_shapes = [], [], []
    for w in cast_weights:
        _, rows, cols = w.shape
        rb = rows // n_steps
        assert rb * n_steps == rows and rb % BF16_ROWS == 0
        w_specs.append(pl.BlockSpec((None, rb, cols), lambda h, i: (layer, step(h, i), 0)))
        wb_specs.append(pl.BlockSpec((rb, cols), lambda h, i: (step(h, i), 0)))
        wb_shapes.append(jax.ShapeDtypeStruct((rows, cols), BF16))
    operands = list(cast_weights)
    if pack_src is not None:
        L, n_in, K = pack_src.shape
        kb = K // tc
        n_blocks = (L - 1) * kb
        assert n_blocks <= n_steps
        blk = lambda h, i: jnp.minimum(step(h, i), n_blocks - 1)
        w_specs.append(pl.BlockSpec((None, n_in, tc), lambda h, i: (1 + blk(h, i) // kb, 0, blk(h, i) % kb)))
        wb_specs.append(pl.BlockSpec((None, IN_WIDTH_PACKED, tc), lambda h, i: (blk(h, i) // kb, 0, blk(h, i) % kb)))
        wb_shapes.append(jax.ShapeDtypeStruct((L - 1, IN_WIDTH_PACKED, K), BF16))
        operands.append(pack_src)
    outs = pl.pallas_call(
        functools.partial(_attn_kernel, tk=tk, nk=S // tk, n_cast=len(cast_weights), pack=pack_src is not None),
        name="attention",
        grid=(MLA_HEADS, nq),
        in_specs=[
            pl.BlockSpec((None, tq, MLA_QK), lambda h, i: (h, i, 0)),
            pl.BlockSpec((None, S, MLA_QK), lambda h, i: (h, 0, 0)),
            pl.BlockSpec((None, S, 2 * MLA_V), lambda h, i: (h, 0, 0)),
        ] + w_specs,
        out_specs=[pl.BlockSpec((tq, MLA_V), lambda h, i: (i, h))] + wb_specs,
        out_shape=[jax.ShapeDtypeStruct((S, MLA_HEADS * MLA_V), BF16)] + wb_shapes,
        compiler_params=_params(("arbitrary", "arbitrary")),
    )(q, k, v, *operands)
    return outs[0], outs[1:]


def _tiles(S, D, F):
    cap = lambda n: min(n, S)
    return dict(
        row=cap(512),
        row_big=cap(1024),
        mixer_cps=4 if S % (4 * CHUNK) == 0 else 1,
        attn_tq=cap(2048), attn_tk=cap(256),
        ffn1_tm=cap(2048), ffn1_tn=min(1024, F),
        ffn2_tn=min(1024, D), ffn2_tk=min(4096, F),
        ffn2_final_tk=min(1024, F),
        pack_tc=min(256, D),
    )


def kernel(x, positions, g_mix, w_in, b_gates, w_conv, g_ml_out, g_ret_out, g_q_norm, w_q_up, g_kv_norm, w_kv_up,
           w_out, g_ffn, w_ff1, w_ff2, g_final):
    B, S, D = x.shape
    assert B == 1
    depth = w_in.shape[0]
    t = _tiles(S, D, w_ff1.shape[-1])

    half = ROPE_DIM // 2
    lane = jnp.arange(RET_HEADS * ROPE_DIM)
    inv = ROPE_THETA ** (-(lane % half).astype(F32) / half)
    ang = positions[0].astype(F32)[:, None] * inv
    cos4 = jnp.cos(ang)
    sin4 = jnp.where((lane % ROPE_DIM) < half, -jnp.sin(ang), jnp.sin(ang))

    w_in_t = jnp.swapaxes(w_in, 1, 2)
    ride_pack = depth > 1 and (depth - 1) * (D // t["pack_tc"]) <= _attention_steps(S, t["attn_tq"])
    w_in_first = _pack_w_in(w_in_t, 1 if ride_pack else depth, tc=t["pack_tc"])
    w_in_rest = None
    q_rank = w_q_up.shape[1]
    w_q4 = w_q_up.astype(BF16).reshape(depth, q_rank, MLA_HEADS, MLA_QK)
    w_q = jnp.concatenate([w_q4[..., :MLA_NOPE].reshape(depth, q_rank, MLA_HEADS * MLA_NOPE),
                           w_q4[..., MLA_NOPE:].reshape(depth, q_rank, MLA_HEADS * ROPE_DIM)], axis=-1)
    w_kv = w_kv_up.astype(BF16)
    bias_t = jnp.broadcast_to(b_gates[:, :, None], (depth, GATE_ROWS, CHUNK))
    idx = jnp.arange(CHUNK)
    tri = jnp.stack([idx[:, None] >= idx[None, :], idx[:, None] <= idx[None, :]]).astype(BF16)
    sel = _gate_select_matrix()
    ret_consts = _retention_consts()
    r3 = lambda g: g[:, None, :]

    xs = x[0]
    for l in range(depth):
        w_in_l, l_in = (w_in_first, l) if (l == 0 or not ride_pack) else (w_in_rest, l - 1)
        proj, gates_t, mq, mkt, rq, rkt = _in_proj(xs, r3(g_mix), w_in_l, w_conv, cos4, sin4, l, l_in,
                                                   tm=t["row_big"])
        h_f, h_b, y_f, y_b, q, k, v = _mixers(mq, mkt, rq, rkt, proj, gates_t, bias_t, l, tri, sel, ret_consts,
                                              r3(g_q_norm), r3(g_kv_norm), w_q, w_kv, cos4, sin4, cps=t["mixer_cps"])
        y_mla, prepared = _attention(q, k, v, (w_out, w_ff1, w_ff2), l, w_in_t if (ride_pack and l == 0) else None,
                                     tq=t["attn_tq"], tk=t["attn_tk"], tc=t["pack_tc"])
        w_out_b, w_ff1_b, w_ff2_b = prepared[:3]
        if ride_pack and l == 0:
            w_in_rest = prepared[3]
        xs, hn = _out_proj(h_f, h_b, y_f, y_b, proj, y_mla, r3(g_ml_out), r3(g_ret_out), r3(g_ffn), l, w_out_b, xs,
                           tm=t["row"])
        act = _ffn1(hn, w_ff1_b, tm=t["ffn1_tm"], tn=t["ffn1_tn"])
        if l < depth - 1:
            xs = _ffn2(act, w_ff2_b, xs, g_final[None, :], tm=t["row_big"], tn=t["ffn2_tn"], tk=t["ffn2_tk"],
                       final=False)
        else:
            xs = _ffn2(act, w_ff2_b, xs, g_final[None, :], tm=t["row_big"], tn=D, tk=t["ffn2_final_tk"], final=True)
    return xs[None]
```

```python
import functools
import math

import numpy as np
import jax
import jax.numpy as jnp
from jax import lax
from jax.experimental import pallas as pl
from jax.experimental.pallas import tpu as pltpu

F32 = jnp.float32
BF16 = jnp.bfloat16

EPS = 1e-6
NEG_INIT = -1e30
ROPE_THETA = 10000.0
CHUNK = 128
LANE = 128
BF16_ROWS = 16
ML_HEADS = 4
ML_DIM = 128
RET_HEADS = 4
RET_QK = 64
RET_V = 128
RET_DECAY_BASE = 5.0
MLA_HEADS = 8
MLA_NOPE = 128
ROPE_DIM = 64
MLA_V = 128
MLA_QK = MLA_NOPE + ROPE_DIM
VMEM_LIMIT = 60 * 1024 * 1024

COL_ML_Q, COL_ML_K, COL_ML_V, COL_ML_O = 0, 4, 8, 12
COL_RQK, COL_RV, COL_RG = 16, 20, 24
COL_CQ, COL_CKV, COL_KROPE, COL_GATES = 28, 32, 34, 35
IN_WIDTH_PACKED = 36 * LANE

_NT = (((1,), (1,)), ((), ()))
_TN = (((0,), (0,)), ((), ()))


def _params(sem):
    return pltpu.CompilerParams(dimension_semantics=sem, vmem_limit_bytes=VMEM_LIMIT)


def _rmsnorm(x, g):
    return x * lax.rsqrt(jnp.mean(x * x, axis=-1, keepdims=True) + EPS) * g


def _rope_grouped(t, cosf, sinf):
    width = t.shape[1]
    lane = lax.broadcasted_iota(jnp.int32, t.shape, 1)
    first_half = (lane % ROPE_DIM) < (ROPE_DIM // 2)
    partner = jnp.where(first_half, pltpu.roll(t, width - ROPE_DIM // 2, 1), pltpu.roll(t, ROPE_DIM // 2, 1))
    return t * cosf + partner * sinf


IN_TN = 12 * LANE
MQK_W = 2 * ML_HEADS * ML_DIM
RQK_W = 2 * RET_HEADS * RET_QK
RQK_OFS = COL_RQK * LANE - IN_TN


def _in_proj_kernel(x_ref, xprev_ref, xnext_ref, g_ref, w_ref, wc_ref, cos_ref, sin_ref,
                    o_ref, gates_ref, mq_out, mkt_out, rq_out, rkt_out, hn_ref, *, nblk):
    i, j = pl.program_id(0), pl.program_id(1)
    tm = x_ref.shape[0]

    def project(hn):
        y = lax.dot_general(hn, w_ref[...], _NT, preferred_element_type=F32)
        o_ref[...] = y.astype(o_ref.dtype)
        return y

    @pl.when(j == 0)
    def _():
        g = g_ref[...]
        hn = _rmsnorm(x_ref[...], g).astype(hn_ref.dtype)
        hn_ref[...] = hn
        halo = jnp.concatenate([_rmsnorm(xprev_ref[...], g), _rmsnorm(xnext_ref[...], g)], axis=0).astype(BF16)
        y = project(hn)[:, :MQK_W]
        yh = lax.dot_general(halo, w_ref[:MQK_W, :], _NT, preferred_element_type=F32)
        prev_row = jnp.where(i > 0, yh[7:8, :], 0.0)
        next_row = jnp.where(i < nblk - 1, yh[8:9, :], 0.0)
        wc = wc_ref[...]
        w0, w1, w2 = wc[0:1, :], wc[1:2, :], wc[2:3, :]
        c = pltpu.roll(y, 1, 0) * w0 + y * w1 + pltpu.roll(y, tm - 1, 0) * w2
        c_first = prev_row * w0 + y[0:1, :] * w1 + y[1:2, :] * w2
        c_last = y[tm - 2:tm - 1, :] * w0 + y[tm - 1:tm, :] * w1 + next_row * w2
        row8 = lax.broadcasted_iota(jnp.int32, (8, MQK_W), 0)
        c = jnp.concatenate([jnp.where(row8 == 0, c_first, c[0:8, :]), c[8:tm - 8, :],
                             jnp.where(row8 == 7, c_last, c[tm - 8:, :])], axis=0)
        c = c / (1.0 + jnp.exp(-c))
        half = MQK_W // 2
        mq_out[...] = (c[:, :half] * (ML_DIM ** -0.5)).astype(mq_out.dtype)
        mkt_out[...] = c[:, half:].T.astype(mkt_out.dtype)

    @pl.when(j == 1)
    def _():
        y = project(hn_ref[...])
        cosf, sinf = cos_ref[...], sin_ref[...]
        half = RQK_W // 2
        rq_out[...] = _rope_grouped(y[:, RQK_OFS:RQK_OFS + half], cosf, sinf).astype(rq_out.dtype)
        rk = _rope_grouped(y[:, RQK_OFS + half:RQK_OFS + RQK_W], cosf, sinf) * (RET_QK ** -0.5)
        rkt_out[...] = rk.T.astype(rkt_out.dtype)

    @pl.when(j == 2)
    def _():
        y = project(hn_ref[...])
        gates_ref[...] = y[:, y.shape[1] - LANE:].T


def _in_proj(x, g, w, w_conv, cos4, sin4, layer, w_layer, *, tm):
    S, K = x.shape
    N = w.shape[1]
    assert N == 3 * IN_TN and MQK_W <= IN_TN and 0 <= RQK_OFS and RQK_OFS + RQK_W <= IN_TN
    nblk = S // tm
    hb = tm // 8
    row_blk = lambda width: pl.BlockSpec((tm, width), lambda i, j: (i, 0))
    col_blk = lambda height: pl.BlockSpec((height, tm), lambda i, j: (0, i))
    return pl.pallas_call(
        functools.partial(_in_proj_kernel, nblk=nblk),
        name="in_proj",
        grid=(nblk, N // IN_TN),
        in_specs=[
            row_blk(K),
            pl.BlockSpec((8, K), lambda i, j: (jnp.maximum(i * hb - 1, 0), 0)),
            pl.BlockSpec((8, K), lambda i, j: (jnp.minimum((i + 1) * hb, S // 8 - 1), 0)),
            pl.BlockSpec((None, 1, K), lambda i, j: (layer, 0, 0)),
            pl.BlockSpec((None, IN_TN, K), lambda i, j: (w_layer, j, 0)),
            pl.BlockSpec((None, 3, MQK_W), lambda i, j: (layer, 0, 0)),
            row_blk(RQK_W // 2), row_blk(RQK_W // 2),
        ],
        out_specs=[pl.BlockSpec((tm, IN_TN), lambda i, j: (i, j)), col_blk(LANE),
                   row_blk(MQK_W // 2), col_blk(MQK_W // 2), row_blk(RQK_W // 2), col_blk(RQK_W // 2)],
        out_shape=[jax.ShapeDtypeStruct((S, N), BF16), jax.ShapeDtypeStruct((LANE, S), F32),
                   jax.ShapeDtypeStruct((S, MQK_W // 2), BF16), jax.ShapeDtypeStruct((MQK_W // 2, S), BF16),
                   jax.ShapeDtypeStruct((S, RQK_W // 2), BF16), jax.ShapeDtypeStruct((RQK_W // 2, S), BF16)],
        scratch_shapes=[pltpu.VMEM((tm, K), BF16)],
        compiler_params=_params(("parallel", "arbitrary")),
    )(x, x, x, g, w, w_conv, cos4, sin4)


def _out_proj_kernel(hf, hb, o_ref, gml_ref, yf, yb, rg_ref, gret_ref, ymla_ref, w_ref, x_ref, gffn_ref,
                     out_ref, hn_out):
    w_ml = ML_HEADS * ML_DIM
    w_ret = RET_HEADS * RET_V
    acc = x_ref[...] + jnp.dot(ymla_ref[...], w_ref[w_ml + w_ret:, :], preferred_element_type=F32)
    y_ml, y_ret = [], []
    for h in range(ML_HEADS):
        sl = slice(h * ML_DIM, (h + 1) * ML_DIM)
        y = _rmsnorm(hf[:, sl].astype(F32) + hb[:, sl].astype(F32), gml_ref[:, sl])
        y_ml.append((y / (1.0 + jnp.exp(-o_ref[:, sl].astype(F32)))).astype(BF16))
    for h in range(RET_HEADS):
        sl = slice(h * RET_V, (h + 1) * RET_V)
        y = _rmsnorm(yf[:, sl].astype(F32) + yb[:, sl].astype(F32), gret_ref[:, sl])
        gate = rg_ref[:, sl].astype(F32)
        y_ret.append((gate / (1.0 + jnp.exp(-gate)) * y).astype(BF16))
    acc = acc + jnp.dot(jnp.concatenate(y_ml, axis=1), w_ref[:w_ml, :], preferred_element_type=F32)
    acc = acc + jnp.dot(jnp.concatenate(y_ret, axis=1), w_ref[w_ml:w_ml + w_ret, :], preferred_element_type=F32)
    out_ref[...] = acc
    hn_out[...] = _rmsnorm(acc, gffn_ref[...]).astype(hn_out.dtype)


def _out_proj(h_f, h_b, y_f, y_b, proj, y_mla, g_ml, g_ret, g_ffn, layer, w, x, *, tm):
    S, N = x.shape
    W = ML_HEADS * ML_DIM
    blk = lambda c: pl.BlockSpec((tm, W), lambda i: (i, c))
    gain = pl.BlockSpec((None, 1, W), lambda i: (layer, 0, 0))
    return pl.pallas_call(
        _out_proj_kernel,
        name="out_proj",
        grid=(S // tm,),
        in_specs=[blk(0), blk(0), blk(COL_ML_O * LANE // W), gain, blk(0), blk(0), blk(COL_RG * LANE // W), gain,
                  pl.BlockSpec((tm, y_mla.shape[1]), lambda i: (i, 0)),
                  pl.BlockSpec(w.shape, lambda i: (0, 0)),
                  pl.BlockSpec((tm, N), lambda i: (i, 0)),
                  pl.BlockSpec((None, 1, N), lambda i: (layer, 0, 0))],
        out_specs=[pl.BlockSpec((tm, N), lambda i: (i, 0)), pl.BlockSpec((tm, N), lambda i: (i, 0))],
        out_shape=[jax.ShapeDtypeStruct((S, N), F32), jax.ShapeDtypeStruct((S, N), BF16)],
        compiler_params=_params(("parallel",)),
    )(h_f, h_b, proj, g_ml, y_f, y_b, proj, g_ret, y_mla, w, x, g_ffn)


def _ffn1_kernel(hn_ref, w_ref, o_ref):
    y = jnp.dot(hn_ref[...], w_ref[...], preferred_element_type=F32)
    o_ref[...] = jnp.square(jnp.maximum(y, 0.0)).astype(o_ref.dtype)


def _ffn1(hn, w, *, tm, tn):
    S, K = hn.shape
    N = w.shape[-1]
    return pl.pallas_call(
        _ffn1_kernel,
        name="ffn1",
        grid=(S // tm, N // tn),
        in_specs=[pl.BlockSpec((tm, K), lambda i, j: (i, 0)), pl.BlockSpec((K, tn), lambda i, j: (0, j))],
        out_specs=pl.BlockSpec((tm, tn), lambda i, j: (i, j)),
        out_shape=jax.ShapeDtypeStruct((S, N), BF16),
        compiler_params=_params(("parallel", "arbitrary")),
    )(hn, w)


def _ffn2_kernel(a_ref, w_ref, x_ref, gf_ref, o_ref, *, final):
    k = pl.program_id(2)

    @pl.when(k == 0)
    def _():
        o_ref[...] = x_ref[...]

    o_ref[...] += jnp.dot(a_ref[...], w_ref[...], preferred_element_type=F32)

    if final:
        @pl.when(k == pl.num_programs(2) - 1)
        def _():
            o_ref[...] = _rmsnorm(o_ref[...], gf_ref[...])


def _ffn2(a, w, x, g_final, *, tm, tn, tk, final):
    S, N = x.shape
    K = a.shape[1]
    assert not final or tn == N
    return pl.pallas_call(
        functools.partial(_ffn2_kernel, final=final),
        name="ffn2_final" if final else "ffn2",
        grid=(S // tm, N // tn, K // tk),
        in_specs=[
            pl.BlockSpec((tm, tk), lambda i, j, k: (i, k)),
            pl.BlockSpec((tk, tn), lambda i, j, k: (k, j)),
            pl.BlockSpec((tm, tn), lambda i, j, k: (i, j)),
            pl.BlockSpec((1, tn), lambda i, j, k: (0, j)),
        ],
        out_specs=pl.BlockSpec((tm, tn), lambda i, j, k: (i, j)),
        out_shape=jax.ShapeDtypeStruct((S, N), F32),
        compiler_params=_params(("parallel", "arbitrary", "arbitrary")),
    )(a, w, x, g_final)


def _log_sigmoid(t):
    return jnp.minimum(t, 0.0) - jnp.log1p(jnp.exp(-jnp.abs(t)))


def _split3(t):
    hi = t.astype(BF16)
    r1 = t - hi.astype(F32)
    mid = r1.astype(BF16)
    lo = (r1 - mid.astype(F32)).astype(BF16)
    return hi, mid, lo


GATE_ROWS = 16


def _mlstm_cumsum(g_t, tri):
    logf_t = _log_sigmoid(g_t)
    return logf_t, jnp.dot(jnp.concatenate(_split3(logf_t), axis=0), tri, preferred_element_type=F32)


def _mlstm_gates(g_t, logf_t, cs):
    b_t = cs[0:GATE_ROWS] + cs[GATE_ROWS:2 * GATE_ROWS] + cs[2 * GATE_ROWS:]
    tot_t = jnp.sum(logf_t, axis=1, keepdims=True)
    r_t = pltpu.roll(g_t, 4, 0) - b_t
    wend_t = tot_t + r_t
    mloc_t = jnp.max(wend_t, axis=1, keepdims=True)
    e_t = jnp.exp(wend_t - mloc_t)
    return b_t, r_t, tot_t, mloc_t, e_t


def _mixers_kernel(mqf, mktf, mvf, gf, rqf, rktf, rvf, mqb, mktb, mvb, gb, rqb, rktb, rvb,
                   bias_ref, tri_ref, sel_ref, decay_ref, inner_ref, zeta_ref,
                   cq_ref, ckv_ref, kr_ref, gq_ref, gkv_ref, wq_ref, wkv_ref, cos_ref, sin_ref,
                   hf_ref, hb_ref, yf_ref, yb_ref, q_out, k_out, v_out, c_ref, m_ref, r_ref, *, chunk_decay, cps):
    mla_refs = (cq_ref, ckv_ref, kr_ref, gq_ref, gkv_ref, wq_ref, wkv_ref, cos_ref, sin_ref, q_out, k_out, v_out)

    @pl.when(pl.program_id(0) == 0)
    def _():
        c_ref[...] = jnp.zeros(c_ref.shape, F32)
        m_ref[...] = jnp.full(m_ref.shape, NEG_INIT, F32)
        r_ref[...] = jnp.zeros(r_ref.shape, F32)

    L = CHUNK
    row = lax.broadcasted_iota(jnp.int32, (L, L), 0)
    col = lax.broadcasted_iota(jnp.int32, (L, L), 1)
    ones = jnp.ones((L, ML_DIM), BF16)
    bias = bias_ref[...]

    all_spans = [[slice(sub * L, (sub + 1) * L) for sub in (t, cps - 1 - t)] for t in range(cps)]

    items = [(d, h) for d in range(2) for h in range(ML_HEADS)]
    m_refs = ((mqf, mktf, mvf, hf_ref), (mqb, mktb, mvb, hb_ref))
    r_refs = ((rqf, rktf, rvf, yf_ref), (rqb, rktb, rvb, yb_ref))

    def m_operands(spans, d, h):
        q_ref, kt_ref, v_ref, _ = m_refs[d]
        sl = slice(h * ML_DIM, (h + 1) * ML_DIM)
        return q_ref[spans[d], sl], kt_ref[sl, spans[d]], v_ref[spans[d], sl]

    def r_operands(spans, d, h):
        q_ref, kt_ref, v_ref, _ = r_refs[d]
        return (q_ref[spans[d], h * RET_QK:(h + 1) * RET_QK], kt_ref[h * RET_QK:(h + 1) * RET_QK, spans[d]],
                v_ref[spans[d], h * RET_V:(h + 1) * RET_V])

    g_ts = [[g_ref[:, spans[d]] + bias for d, g_ref in enumerate((gf, gb))] for spans in all_spans]
    cums = [[_mlstm_cumsum(g_ts[t][d], tri_ref[1 - d]) for d in range(2)] for t in range(cps)]

    m_raw = [[jnp.dot(*m_operands(spans, d, h)[:2], preferred_element_type=F32) for d, h in items]
             for spans in all_spans]
    r_raw = [[jnp.dot(*r_operands(spans, d, h)[:2], preferred_element_type=F32) for d, h in items]
             for spans in all_spans]

    all_gates, all_bcast = [], []
    for t in range(cps):
        gates = [_mlstm_gates(g_ts[t][d], *cums[t][d]) for d in range(2)]
        x_rows = jnp.concatenate([part for g in gates for part in _split3(g[0])], axis=0)
        all_gates.append(gates)
        all_bcast.append(lax.dot_general(x_rows, sel_ref[...], _TN, preferred_element_type=F32))

    _mla_project(*mla_refs)

    m_cur = [m_ref[st:st + 1, :] for st in range(2 * ML_HEADS)]
    m_at, keep_at, gain_at = [], [], []
    for gates in all_gates:
        m_at.append(list(m_cur))
        keeps, gains = [], []
        for d, h in items:
            cf, st = 8 * d + 4 + h, ML_HEADS * d + h
            tot_t, mloc_t = gates[d][2], gates[d][3]
            blast = jnp.broadcast_to(tot_t[cf:cf + 1, :], (1, LANE))
            m_loc = jnp.broadcast_to(mloc_t[cf:cf + 1, :], (1, LANE))
            m_new = jnp.maximum(blast + m_cur[st], m_loc)
            keeps.append(jnp.exp(blast + m_cur[st] - m_new))
            gains.append(jnp.exp(m_loc - m_new))
            m_cur[st] = m_new
        keep_at.append(keeps)
        gain_at.append(gains)
    for st in range(2 * ML_HEADS):
        m_ref[st:st + 1, :] = m_cur[st]

    m_s, m_kw, m_winter, m_floor, r_s, r_kw = [], [], [], [], [], []
    for t, spans in enumerate(all_spans):
        ms_t, mkw_t, mwi_t, mfl_t, rs_t, rkw_t = [], [], [], [], [], []
        for j, (d, h) in enumerate(items):
            cf, st = 8 * d + 4 + h, ML_HEADS * d + h
            _, r_t, _, _, e_t = all_gates[t][d]
            mask = (row >= col) if d == 0 else (row <= col)
            bc = all_bcast[t][:, st * LANE:(st + 1) * LANE]
            dlog = jnp.where(mask, bc + r_t[cf:cf + 1, :], -jnp.inf)
            inter_log = bc + m_at[t][st]
            m_t = jnp.maximum(jnp.max(dlog, axis=1, keepdims=True), inter_log)
            ms_t.append((m_raw[t][j] * jnp.exp(dlog - m_t)).astype(BF16))
            mwi_t.append(jnp.exp(inter_log - m_t))
            mfl_t.append(jnp.exp(-m_t))
            kt = m_operands(spans, d, h)[1]
            mkw_t.append((kt.astype(F32) * e_t[cf:cf + 1, :]).astype(BF16))
            rs_t.append((r_raw[t][j] * decay_ref[st]).astype(BF16))
            rkt = r_operands(spans, d, h)[1]
            rkw_t.append((rkt.astype(F32) * zeta_ref[st:st + 1, :]).astype(BF16))
        m_s.append(ms_t), m_kw.append(mkw_t), m_winter.append(mwi_t), m_floor.append(mfl_t)
        r_s.append(rs_t), r_kw.append(rkw_t)

    c_cur = [c_ref[st] for st in range(2 * ML_HEADS)]
    r_cur = [r_ref[st] for st in range(2 * RET_HEADS)]
    for t, spans in enumerate(all_spans):
        v1s = [jnp.concatenate([m_operands(spans, d, h)[2], ones], axis=1) for d, h in items]
        intra = [jnp.dot(m_s[t][j], v1s[j], preferred_element_type=F32) for j in range(len(items))]
        inter = [jnp.dot(m_operands(spans, d, h)[0], c_cur[ML_HEADS * d + h].astype(BF16),
                         preferred_element_type=F32) for d, h in items]
        c_loc = [jnp.dot(m_kw[t][j], v1s[j], preferred_element_type=F32) for j in range(len(items))]
        ry_intra = [jnp.dot(r_s[t][j], r_operands(spans, d, h)[2], preferred_element_type=F32)
                    for j, (d, h) in enumerate(items)]
        ry_inter = [jnp.dot(r_operands(spans, d, h)[0], r_cur[RET_HEADS * d + h].astype(BF16),
                            preferred_element_type=F32) for d, h in items]
        r_loc = [jnp.dot(r_kw[t][j], r_operands(spans, d, h)[2], preferred_element_type=F32)
                 for j, (d, h) in enumerate(items)]
        for j, (d, h) in enumerate(items):
            st = ML_HEADS * d + h
            w_inter = m_winter[t][j]
            num = intra[j][:, :ML_DIM] + w_inter * inter[j][:, :ML_DIM]
            den = intra[j][:, ML_DIM:] + w_inter * inter[j][:, ML_DIM:]
            m_refs[d][3][spans[d], h * ML_DIM:(h + 1) * ML_DIM] = (
                num / jnp.maximum(jnp.abs(den), m_floor[t][j])).astype(BF16)
            keep, gain = keep_at[t][j], gain_at[t][j]
            c_cur[st] = (jnp.concatenate([keep, keep], axis=1) * c_cur[st]
                         + jnp.concatenate([gain, gain], axis=1) * c_loc[j])
            r_refs[d][3][spans[d], h * RET_V:(h + 1) * RET_V] = (
                ry_intra[j] + ry_inter[j] * inner_ref[st]).astype(BF16)
            r_cur[st] = chunk_decay[st] * r_cur[st] + r_loc[j]
    for st in range(2 * ML_HEADS):
        c_ref[st] = c_cur[st]
        r_ref[st] = r_cur[st]


def _gate_select_matrix():
    sel = np.zeros((2 * 3 * GATE_ROWS, 2 * ML_HEADS * LANE), np.float32)
    for d in range(2):
        for part in range(3):
            for h in range(ML_HEADS):
                st = ML_HEADS * d + h
                sel[(3 * d + part) * GATE_ROWS + 8 * d + 4 + h, st * LANE:(st + 1) * LANE] = 1.0
    return jnp.asarray(sel, BF16)


def _mixers(mq, mkt, rq, rkt, proj, gates_t, bias_t, layer, tri, sel, ret_consts, g_q, g_kv, w_q, w_kv, cos4, sin4,
            *, cps):
    S = proj.shape[0]
    q_rank, kv_rank = w_q.shape[1], w_kv.shape[1]
    rows = cps * CHUNK
    nb = S // rows
    MW = ML_HEADS * ML_DIM
    RQ = RET_HEADS * RET_QK
    RV = RET_HEADS * RET_V
    decay, inner, zeta, chunk_decay = ret_consts
    fwd = lambda n: n
    bwd = lambda n: nb - 1 - n

    def specs(idx):
        return [
            pl.BlockSpec((rows, MW), lambda n: (idx(n), 0)),
            pl.BlockSpec((MW, rows), lambda n: (0, idx(n))),
            pl.BlockSpec((rows, MW), lambda n: (idx(n), COL_ML_V * LANE // MW)),
            pl.BlockSpec((GATE_ROWS, rows), lambda n: (0, idx(n))),
            pl.BlockSpec((rows, RQ), lambda n: (idx(n), 0)),
            pl.BlockSpec((RQ, rows), lambda n: (0, idx(n))),
            pl.BlockSpec((rows, RV), lambda n: (idx(n), COL_RV * LANE // RV)),
        ]

    whole = lambda a: pl.BlockSpec(a.shape, lambda n: (0,) * a.ndim)
    out = lambda idx, w: pl.BlockSpec((rows, w), lambda n: (idx(n), 0))
    operands = (mq, mkt, proj, gates_t, rq, rkt, proj)
    return pl.pallas_call(
        functools.partial(_mixers_kernel, chunk_decay=chunk_decay, cps=cps),
        name="mixers",
        grid=(nb,),
        in_specs=specs(fwd) + specs(bwd) + [
            pl.BlockSpec((None, GATE_ROWS, CHUNK), lambda n: (layer, 0, 0)),
            whole(tri), whole(sel), whole(decay), whole(inner), whole(zeta),
            pl.BlockSpec((rows, q_rank), lambda n: (n, COL_CQ * LANE // q_rank)),
            pl.BlockSpec((rows, kv_rank), lambda n: (n, COL_CKV * LANE // kv_rank)),
            pl.BlockSpec((rows, LANE), lambda n: (n, COL_KROPE)),
            pl.BlockSpec((None, 1, q_rank), lambda n: (layer, 0, 0)),
            pl.BlockSpec((None, 1, kv_rank), lambda n: (layer, 0, 0)),
            pl.BlockSpec((None, q_rank, w_q.shape[2]), lambda n: (layer, 0, 0)),
            pl.BlockSpec((None, kv_rank, w_kv.shape[2]), lambda n: (layer, 0, 0)),
            pl.BlockSpec((rows, 2 * LANE), lambda n: (n, 0)),
            pl.BlockSpec((rows, 2 * LANE), lambda n: (n, 0)),
        ],
        out_specs=[out(fwd, MW), out(bwd, MW), out(fwd, RV), out(bwd, RV),
                   pl.BlockSpec((MLA_HEADS, rows, MLA_QK), lambda n: (0, n, 0)),
                   pl.BlockSpec((MLA_HEADS, rows, MLA_QK), lambda n: (0, n, 0)),
                   pl.BlockSpec((MLA_HEADS, rows, 2 * MLA_V), lambda n: (0, n, 0))],
        out_shape=[jax.ShapeDtypeStruct((S, MW), BF16), jax.ShapeDtypeStruct((S, MW), BF16),
                   jax.ShapeDtypeStruct((S, RV), BF16), jax.ShapeDtypeStruct((S, RV), BF16),
                   jax.ShapeDtypeStruct((MLA_HEADS, S, MLA_QK), BF16),
                   jax.ShapeDtypeStruct((MLA_HEADS, S, MLA_QK), BF16),
                   jax.ShapeDtypeStruct((MLA_HEADS, S, 2 * MLA_V), BF16)],
        scratch_shapes=[pltpu.VMEM((2 * ML_HEADS, ML_DIM, 2 * ML_DIM), F32), pltpu.VMEM((2 * ML_HEADS, LANE), F32),
                        pltpu.VMEM((2 * RET_HEADS, RET_QK, RET_V), F32)],
        compiler_params=_params(("arbitrary",)),
    )(*operands, *operands, bias_t, tri, sel, decay, inner, zeta, proj, proj, proj, g_q, g_kv, w_q, w_kv, cos4, sin4)


def _retention_consts():
    log_gamma = jnp.log1p(-jnp.exp2(-RET_DECAY_BASE - jnp.arange(RET_HEADS, dtype=F32)))
    idx = jnp.arange(CHUNK, dtype=F32)
    diff = idx[:, None] - idx[None, :]
    decay, inner, zeta = [], [], []
    for d in range(2):
        lg_d = log_gamma if d == 0 else log_gamma[::-1]
        dd = diff if d == 0 else -diff
        pos = idx if d == 0 else (CHUNK - 1.0 - idx)
        keep = dd >= 0
        for h in range(RET_HEADS):
            lg = lg_d[h]
            decay.append(jnp.where(keep, jnp.exp(jnp.where(keep, dd, 0.0) * lg), 0.0))
            inner.append(jnp.broadcast_to(jnp.exp((pos + 1.0) * lg)[:, None], (CHUNK, LANE)))
            zeta.append(jnp.exp((CHUNK - 1.0 - pos) * lg))
    lg_np = np.log1p(-np.exp2(-RET_DECAY_BASE - np.arange(RET_HEADS, dtype=np.float32))).astype(np.float32)
    chunk_decay = tuple(float(np.exp(np.float32(CHUNK) * lg)) for lg in list(lg_np) + list(lg_np[::-1]))
    return jnp.stack(decay), jnp.stack(inner), jnp.stack(zeta), chunk_decay


def _mla_project(cq_ref, ckv_ref, kr_ref, gq_ref, gkv_ref, wq_ref, wkv_ref, cos_ref, sin_ref,
                 q_out, k_out, v_out):
    tm = cq_ref.shape[0]
    cos4 = cos_ref[...]
    sin4 = sin_ref[...]
    qn = _rmsnorm(cq_ref[...].astype(F32), gq_ref[...]).astype(BF16)
    kvn = _rmsnorm(ckv_ref[...].astype(F32), gkv_ref[...]).astype(BF16)
    q_all = jnp.dot(qn, wq_ref[...], preferred_element_type=F32)
    kv_all = jnp.dot(kvn, wkv_ref[...], preferred_element_type=F32)
    q_scale = (MLA_QK ** -0.5) * math.log2(math.e)
    nope_w = MLA_HEADS * MLA_NOPE
    q_rope = _rope_grouped(q_all[:, nope_w:], jnp.concatenate([cos4, cos4], axis=1),
                           jnp.concatenate([sin4, sin4], axis=1)) * q_scale
    k_rope = _rope_grouped(kr_ref[...].astype(F32), cos4[:, :LANE], sin4[:, :LANE])[:, :ROPE_DIM].astype(BF16)
    ones = jnp.ones((tm, MLA_V), BF16)
    for h in range(MLA_HEADS):
        q_out[h, :, :MLA_NOPE] = (q_all[:, h * MLA_NOPE:(h + 1) * MLA_NOPE] * q_scale).astype(BF16)
        q_out[h, :, MLA_NOPE:] = q_rope[:, h * ROPE_DIM:(h + 1) * ROPE_DIM].astype(BF16)
        kv0 = h * (MLA_NOPE + MLA_V)
        k_out[h, :, :MLA_NOPE] = kv_all[:, kv0:kv0 + MLA_NOPE].astype(BF16)
        k_out[h, :, MLA_NOPE:] = k_rope
        v_out[h, :, :MLA_V] = kv_all[:, kv0 + MLA_NOPE:kv0 + MLA_NOPE + MLA_V].astype(BF16)
        v_out[h, :, MLA_V:] = ones


def _pack_w_in_block(w_ref, o_ref):
    n_in, tc = w_ref.shape
    ml_end = 4 * ML_HEADS * ML_DIM
    gates = 4 * ML_HEADS
    rest = n_in - ml_end - gates
    gate_row = IN_WIDTH_PACKED - LANE
    assert ml_end + rest + (LANE - ROPE_DIM) == gate_row
    o_ref[:ml_end, :] = w_ref[:ml_end, :].astype(BF16)
    o_ref[ml_end:ml_end + rest, :] = w_ref[ml_end + gates:, :].astype(BF16)
    o_ref[ml_end + rest:gate_row, :] = jnp.zeros((LANE - ROPE_DIM, tc), BF16)
    o_ref[gate_row:gate_row + gates, :] = w_ref[ml_end:ml_end + gates, :].astype(BF16)
    o_ref[gate_row + gates:, :] = jnp.zeros((LANE - gates, tc), BF16)


def _pack_w_in(w_in_t, n_layers, *, tc):
    _, n_in, K = w_in_t.shape
    return pl.pallas_call(
        _pack_w_in_block,
        name="pack_w_in",
        grid=(n_layers, K // tc),
        in_specs=[pl.BlockSpec((None, n_in, tc), lambda l, i: (l, 0, i))],
        out_specs=pl.BlockSpec((None, IN_WIDTH_PACKED, tc), lambda l, i: (l, 0, i)),
        out_shape=jax.ShapeDtypeStruct((n_layers, IN_WIDTH_PACKED, K), BF16),
        compiler_params=_params(("parallel", "parallel")),
    )(w_in_t)


def _attn_kernel(q_ref, k_ref, v_ref, *rest, tk, nk, n_cast, pack):
    n_in = n_cast + pack
    w_refs, o_ref, wb_refs = rest[:n_in], rest[n_in], rest[n_in + 1:]
    for w_ref, wb_ref in zip(w_refs[:n_cast], wb_refs[:n_cast]):
        wb_ref[...] = w_ref[...].astype(wb_ref.dtype)
    if pack:
        _pack_w_in_block(w_refs[n_cast], wb_refs[n_cast])

    q = q_ref[...]
    m = acc = None
    for c in range(nk):
        k = k_ref[c * tk:(c + 1) * tk, :]
        v = v_ref[c * tk:(c + 1) * tk, :]
        s = lax.dot_general(q, k, _NT, preferred_element_type=F32)
        m_c = jnp.max(s, axis=-1, keepdims=True)
        if c == 0:
            m = m_c
            acc = jnp.dot(jnp.exp2(s - m).astype(BF16), v, preferred_element_type=F32)
        else:
            m_new = jnp.maximum(m, m_c)
            acc = jnp.exp2(m - m_new) * acc + jnp.dot(jnp.exp2(s - m_new).astype(BF16), v, preferred_element_type=F32)
            m = m_new
    o_ref[...] = (acc[:, :MLA_V] / acc[:, MLA_V:]).astype(o_ref.dtype)


def _attention_steps(S, tq):
    return MLA_HEADS * (S // tq)


def _attention(q, k, v, cast_weights, layer, pack_src, *, tq, tk, tc):
    S = q.shape[1]
    nq = S // tq
    n_steps = _attention_steps(S, tq)
    step = lambda h, i: h * nq + i
    w_specs, wb_specs, wb_shapes = [], [], []
    for w in cast_weights:
        _, rows, cols = w.shape
        rb = rows // n_steps
        assert rb * n_steps == rows and rb % BF16_ROWS == 0
        w_specs.append(pl.BlockSpec((None, rb, cols), lambda h, i: (layer, step(h, i), 0)))
        wb_specs.append(pl.BlockSpec((rb, cols), lambda h, i: (step(h, i), 0)))
        wb_shapes.append(jax.ShapeDtypeStruct((rows, cols), BF16))
    operands = list(cast_weights)
    if pack_src is not None:
        L, n_in, K = pack_src.shape
        kb = K // tc
        n_blocks = (L - 1) * kb
        assert n_blocks <= n_steps
        blk = lambda h, i: jnp.minimum(step(h, i), n_blocks - 1)
        w_specs.append(pl.BlockSpec((None, n_in, tc), lambda h, i: (1 + blk(h, i) // kb, 0, blk(h, i) % kb)))
        wb_specs.append(pl.BlockSpec((None, IN_WIDTH_PACKED, tc), lambda h, i: (blk(h, i) // kb, 0, blk(h, i) % kb)))
        wb_shapes.append(jax.ShapeDtypeStruct((L - 1, IN_WIDTH_PACKED, K), BF16))
        operands.append(pack_src)
    outs = pl.pallas_call(
        functools.partial(_attn_kernel, tk=tk, nk=S // tk, n_cast=len(cast_weights), pack=pack_src is not None),
        name="attention",
        grid=(MLA_HEADS, nq),
        in_specs=[
            pl.BlockSpec((None, tq, MLA_QK), lambda h, i: (h, i, 0)),
            pl.BlockSpec((None, S, MLA_QK), lambda h, i: (h, 0, 0)),
            pl.BlockSpec((None, S, 2 * MLA_V), lambda h, i: (h, 0, 0)),
        ] + w_specs,
        out_specs=[pl.BlockSpec((tq, MLA_V), lambda h, i: (i, h))] + wb_specs,
        out_shape=[jax.ShapeDtypeStruct((S, MLA_HEADS * MLA_V), BF16)] + wb_shapes,
        compiler_params=_params(("arbitrary", "arbitrary")),
    )(q, k, v, *operands)
    return outs[0], outs[1:]


def _tiles(S, D, F):
    cap = lambda n: min(n, S)
    return dict(
        row=cap(512),
        row_big=cap(1024),
        mixer_cps=4 if S % (4 * CHUNK) == 0 else 1,
        attn_tq=cap(2048), attn_tk=cap(256),
        ffn1_tm=cap(2048), ffn1_tn=min(1024, F),
        ffn2_tn=min(1024, D), ffn2_tk=min(4096, F),
        ffn2_final_tk=min(2048, F),
        pack_tc=min(256, D),
    )


def kernel(x, positions, g_mix, w_in, b_gates, w_conv, g_ml_out, g_ret_out, g_q_norm, w_q_up, g_kv_norm, w_kv_up,
           w_out, g_ffn, w_ff1, w_ff2, g_final):
    B, S, D = x.shape
    assert B == 1
    depth = w_in.shape[0]
    t = _tiles(S, D, w_ff1.shape[-1])

    half = ROPE_DIM // 2
    lane = jnp.arange(RET_HEADS * ROPE_DIM)
    inv = ROPE_THETA ** (-(lane % half).astype(F32) / half)
    ang = positions[0].astype(F32)[:, None] * inv
    cos4 = jnp.cos(ang)
    sin4 = jnp.where((lane % ROPE_DIM) < half, -jnp.sin(ang), jnp.sin(ang))

    w_in_t = jnp.swapaxes(w_in, 1, 2)
    ride_pack = depth > 1 and (depth - 1) * (D // t["pack_tc"]) <= _attention_steps(S, t["attn_tq"])
    w_in_first = _pack_w_in(w_in_t, 1 if ride_pack else depth, tc=t["pack_tc"])
    w_in_rest = None
    q_rank = w_q_up.shape[1]
    w_q4 = w_q_up.astype(BF16).reshape(depth, q_rank, MLA_HEADS, MLA_QK)
    w_q = jnp.concatenate([w_q4[..., :MLA_NOPE].reshape(depth, q_rank, MLA_HEADS * MLA_NOPE),
                           w_q4[..., MLA_NOPE:].reshape(depth, q_rank, MLA_HEADS * ROPE_DIM)], axis=-1)
    w_kv = w_kv_up.astype(BF16)
    bias_t = jnp.broadcast_to(b_gates[:, :, None], (depth, GATE_ROWS, CHUNK))
    idx = jnp.arange(CHUNK)
    tri = jnp.stack([idx[:, None] >= idx[None, :], idx[:, None] <= idx[None, :]]).astype(BF16)
    sel = _gate_select_matrix()
    ret_consts = _retention_consts()
    r3 = lambda g: g[:, None, :]

    xs = x[0]
    for l in range(depth):
        w_in_l, l_in = (w_in_first, l) if (l == 0 or not ride_pack) else (w_in_rest, l - 1)
        proj, gates_t, mq, mkt, rq, rkt = _in_proj(xs, r3(g_mix), w_in_l, w_conv, cos4, sin4, l, l_in,
                                                   tm=t["row_big"])
        h_f, h_b, y_f, y_b, q, k, v = _mixers(mq, mkt, rq, rkt, proj, gates_t, bias_t, l, tri, sel, ret_consts,
                                              r3(g_q_norm), r3(g_kv_norm), w_q, w_kv, cos4, sin4, cps=t["mixer_cps"])
        y_mla, prepared = _attention(q, k, v, (w_out, w_ff1, w_ff2), l, w_in_t if (ride_pack and l == 0) else None,
                                     tq=t["attn_tq"], tk=t["attn_tk"], tc=t["pack_tc"])
        w_out_b, w_ff1_b, w_ff2_b = prepared[:3]
        if ride_pack and l == 0:
            w_in_rest = prepared[3]
        xs, hn = _out_proj(h_f, h_b, y_f, y_b, proj, y_mla, r3(g_ml_out), r3(g_ret_out), r3(g_ffn), l, w_out_b, xs,
                           tm=t["row"])
        act = _ffn1(hn, w_ff1_b, tm=t["ffn1_tm"], tn=t["ffn1_tn"])
        if l < depth - 1:
            xs = _ffn2(act, w_ff2_b, xs, g_final[None, :], tm=t["row_big"], tn=t["ffn2_tn"], tk=t["ffn2_tk"],
                       final=False)
        else:
            xs = _ffn2(act, w_ff2_b, xs, g_final[None, :], tm=t["row_big"], tn=D, tk=t["ffn2_final_tk"], final=True)
    return xs[None]
```

```python
import functools
import math

import numpy as np
import jax
import jax.numpy as jnp
from jax import lax
from jax.experimental import pallas as pl
from jax.experimental.pallas import tpu as pltpu

F32 = jnp.float32
BF16 = jnp.bfloat16

EPS = 1e-6
NEG_INIT = -1e30
ROPE_THETA = 10000.0
CHUNK = 128
LANE = 128
BF16_ROWS = 16
ML_HEADS = 4
ML_DIM = 128
RET_HEADS = 4
RET_QK = 64
RET_V = 128
RET_DECAY_BASE = 5.0
MLA_HEADS = 8
MLA_NOPE = 128
ROPE_DIM = 64
MLA_V = 128
MLA_QK = MLA_NOPE + ROPE_DIM
VMEM_LIMIT = 60 * 1024 * 1024

COL_ML_Q, COL_ML_K, COL_ML_V, COL_ML_O = 0, 4, 8, 12
COL_RQK, COL_RV, COL_RG = 16, 20, 24
COL_CQ, COL_CKV, COL_KROPE, COL_GATES = 28, 32, 34, 35
IN_WIDTH_PACKED = 36 * LANE

_NT = (((1,), (1,)), ((), ()))
_TN = (((0,), (0,)), ((), ()))


def _params(sem):
    return pltpu.CompilerParams(dimension_semantics=sem, vmem_limit_bytes=VMEM_LIMIT)


def _rmsnorm(x, g):
    return x * lax.rsqrt(jnp.mean(x * x, axis=-1, keepdims=True) + EPS) * g


def _rope_grouped(t, cosf, sinf):
    width = t.shape[1]
    lane = lax.broadcasted_iota(jnp.int32, t.shape, 1)
    first_half = (lane % ROPE_DIM) < (ROPE_DIM // 2)
    partner = jnp.where(first_half, pltpu.roll(t, width - ROPE_DIM // 2, 1), pltpu.roll(t, ROPE_DIM // 2, 1))
    return t * cosf + partner * sinf


IN_TN = 12 * LANE
MQK_W = 2 * ML_HEADS * ML_DIM
RQK_W = 2 * RET_HEADS * RET_QK
RQK_OFS = COL_RQK * LANE - IN_TN


def _in_proj_kernel(x_ref, xprev_ref, xnext_ref, g_ref, w_ref, wc_ref, cos_ref, sin_ref,
                    o_ref, gates_ref, mq_out, mkt_out, rq_out, rkt_out, hn_ref, *, nblk):
    i, j = pl.program_id(0), pl.program_id(1)
    tm = x_ref.shape[0]

    def project(hn):
        y = lax.dot_general(hn, w_ref[...], _NT, preferred_element_type=F32)
        o_ref[...] = y.astype(o_ref.dtype)
        return y

    @pl.when(j == 0)
    def _():
        g = g_ref[...]
        hn = _rmsnorm(x_ref[...], g).astype(hn_ref.dtype)
        hn_ref[...] = hn
        halo = jnp.concatenate([_rmsnorm(xprev_ref[...], g), _rmsnorm(xnext_ref[...], g)], axis=0).astype(BF16)
        y = project(hn)[:, :MQK_W]
        yh = lax.dot_general(halo, w_ref[:MQK_W, :], _NT, preferred_element_type=F32)
        prev_row = jnp.where(i > 0, yh[7:8, :], 0.0)
        next_row = jnp.where(i < nblk - 1, yh[8:9, :], 0.0)
        wc = wc_ref[...]
        w0, w1, w2 = wc[0:1, :], wc[1:2, :], wc[2:3, :]
        c = pltpu.roll(y, 1, 0) * w0 + y * w1 + pltpu.roll(y, tm - 1, 0) * w2
        c_first = prev_row * w0 + y[0:1, :] * w1 + y[1:2, :] * w2
        c_last = y[tm - 2:tm - 1, :] * w0 + y[tm - 1:tm, :] * w1 + next_row * w2
        row8 = lax.broadcasted_iota(jnp.int32, (8, MQK_W), 0)
        c = jnp.concatenate([jnp.where(row8 == 0, c_first, c[0:8, :]), c[8:tm - 8, :],
                             jnp.where(row8 == 7, c_last, c[tm - 8:, :])], axis=0)
        c = c / (1.0 + jnp.exp(-c))
        half = MQK_W // 2
        mq_out[...] = (c[:, :half] * (ML_DIM ** -0.5)).astype(mq_out.dtype)
        mkt_out[...] = c[:, half:].T.astype(mkt_out.dtype)

    @pl.when(j == 1)
    def _():
        y = project(hn_ref[...])
        cosf, sinf = cos_ref[...], sin_ref[...]
        half = RQK_W // 2
        rq_out[...] = _rope_grouped(y[:, RQK_OFS:RQK_OFS + half], cosf, sinf).astype(rq_out.dtype)
        rk = _rope_grouped(y[:, RQK_OFS + half:RQK_OFS + RQK_W], cosf, sinf) * (RET_QK ** -0.5)
        rkt_out[...] = rk.T.astype(rkt_out.dtype)

    @pl.when(j == 2)
    def _():
        y = project(hn_ref[...])
        gates_ref[...] = y[:, y.shape[1] - LANE:].T


def _in_proj(x, g, w, w_conv, cos4, sin4, layer, w_layer, *, tm):
    S, K = x.shape
    N = w.shape[1]
    assert N == 3 * IN_TN and MQK_W <= IN_TN and 0 <= RQK_OFS and RQK_OFS + RQK_W <= IN_TN
    nblk = S // tm
    hb = tm // 8
    row_blk = lambda width: pl.BlockSpec((tm, width), lambda i, j: (i, 0))
    col_blk = lambda height: pl.BlockSpec((height, tm), lambda i, j: (0, i))
    return pl.pallas_call(
        functools.partial(_in_proj_kernel, nblk=nblk),
        name="in_proj",
        grid=(nblk, N // IN_TN),
        in_specs=[
            row_blk(K),
            pl.BlockSpec((8, K), lambda i, j: (jnp.maximum(i * hb - 1, 0), 0)),
            pl.BlockSpec((8, K), lambda i, j: (jnp.minimum((i + 1) * hb, S // 8 - 1), 0)),
            pl.BlockSpec((None, 1, K), lambda i, j: (layer, 0, 0)),
            pl.BlockSpec((None, IN_TN, K), lambda i, j: (w_layer, j, 0)),
            pl.BlockSpec((None, 3, MQK_W), lambda i, j: (layer, 0, 0)),
            row_blk(RQK_W // 2), row_blk(RQK_W // 2),
        ],
        out_specs=[pl.BlockSpec((tm, IN_TN), lambda i, j: (i, j)), col_blk(LANE),
                   row_blk(MQK_W // 2), col_blk(MQK_W // 2), row_blk(RQK_W // 2), col_blk(RQK_W // 2)],
        out_shape=[jax.ShapeDtypeStruct((S, N), BF16), jax.ShapeDtypeStruct((LANE, S), F32),
                   jax.ShapeDtypeStruct((S, MQK_W // 2), BF16), jax.ShapeDtypeStruct((MQK_W // 2, S), BF16),
                   jax.ShapeDtypeStruct((S, RQK_W // 2), BF16), jax.ShapeDtypeStruct((RQK_W // 2, S), BF16)],
        scratch_shapes=[pltpu.VMEM((tm, K), BF16)],
        compiler_params=_params(("parallel", "arbitrary")),
    )(x, x, x, g, w, w_conv, cos4, sin4)


def _out_proj_kernel(hf, hb, o_ref, gml_ref, yf, yb, rg_ref, gret_ref, ymla_ref, w_ref, x_ref, gffn_ref,
                     out_ref, hn_out):
    w_ml = ML_HEADS * ML_DIM
    w_ret = RET_HEADS * RET_V
    acc = x_ref[...] + jnp.dot(ymla_ref[...], w_ref[w_ml + w_ret:, :], preferred_element_type=F32)
    y_ml, y_ret = [], []
    for h in range(ML_HEADS):
        sl = slice(h * ML_DIM, (h + 1) * ML_DIM)
        y = _rmsnorm(hf[:, sl].astype(F32) + hb[:, sl].astype(F32), gml_ref[:, sl])
        y_ml.append((y / (1.0 + jnp.exp(-o_ref[:, sl].astype(F32)))).astype(BF16))
    for h in range(RET_HEADS):
        sl = slice(h * RET_V, (h + 1) * RET_V)
        y = _rmsnorm(yf[:, sl].astype(F32) + yb[:, sl].astype(F32), gret_ref[:, sl])
        gate = rg_ref[:, sl].astype(F32)
        y_ret.append((gate / (1.0 + jnp.exp(-gate)) * y).astype(BF16))
    acc = acc + jnp.dot(jnp.concatenate(y_ml, axis=1), w_ref[:w_ml, :], preferred_element_type=F32)
    acc = acc + jnp.dot(jnp.concatenate(y_ret, axis=1), w_ref[w_ml:w_ml + w_ret, :], preferred_element_type=F32)
    out_ref[...] = acc
    hn_out[...] = _rmsnorm(acc, gffn_ref[...]).astype(hn_out.dtype)


def _out_proj(h_f, h_b, y_f, y_b, proj, y_mla, g_ml, g_ret, g_ffn, layer, w, x, *, tm):
    S, N = x.shape
    W = ML_HEADS * ML_DIM
    blk = lambda c: pl.BlockSpec((tm, W), lambda i: (i, c))
    gain = pl.BlockSpec((None, 1, W), lambda i: (layer, 0, 0))
    return pl.pallas_call(
        _out_proj_kernel,
        name="out_proj",
        grid=(S // tm,),
        in_specs=[blk(0), blk(0), blk(COL_ML_O * LANE // W), gain, blk(0), blk(0), blk(COL_RG * LANE // W), gain,
                  pl.BlockSpec((tm, y_mla.shape[1]), lambda i: (i, 0)),
                  pl.BlockSpec(w.shape, lambda i: (0, 0)),
                  pl.BlockSpec((tm, N), lambda i: (i, 0)),
                  pl.BlockSpec((None, 1, N), lambda i: (layer, 0, 0))],
        out_specs=[pl.BlockSpec((tm, N), lambda i: (i, 0)), pl.BlockSpec((tm, N), lambda i: (i, 0))],
        out_shape=[jax.ShapeDtypeStruct((S, N), F32), jax.ShapeDtypeStruct((S, N), BF16)],
        compiler_params=_params(("parallel",)),
    )(h_f, h_b, proj, g_ml, y_f, y_b, proj, g_ret, y_mla, w, x, g_ffn)


def _ffn1_kernel(hn_ref, w_ref, o_ref):
    y = jnp.dot(hn_ref[...], w_ref[...], preferred_element_type=F32)
    o_ref[...] = jnp.square(jnp.maximum(y, 0.0)).astype(o_ref.dtype)


def _ffn1(hn, w, *, tm, tn):
    S, K = hn.shape
    N = w.shape[-1]
    return pl.pallas_call(
        _ffn1_kernel,
        name="ffn1",
        grid=(S // tm, N // tn),
        in_specs=[pl.BlockSpec((tm, K), lambda i, j: (i, 0)), pl.BlockSpec((K, tn), lambda i, j: (0, j))],
        out_specs=pl.BlockSpec((tm, tn), lambda i, j: (i, j)),
        out_shape=jax.ShapeDtypeStruct((S, N), BF16),
        compiler_params=_params(("parallel", "arbitrary")),
    )(hn, w)


def _ffn2_kernel(a_ref, w_ref, x_ref, gf_ref, o_ref, *, final):
    k = pl.program_id(2)

    @pl.when(k == 0)
    def _():
        o_ref[...] = x_ref[...]

    o_ref[...] += jnp.dot(a_ref[...], w_ref[...], preferred_element_type=F32)

    if final:
        @pl.when(k == pl.num_programs(2) - 1)
        def _():
            o_ref[...] = _rmsnorm(o_ref[...], gf_ref[...])


def _ffn2(a, w, x, g_final, *, tm, tn, tk, final):
    S, N = x.shape
    K = a.shape[1]
    assert not final or tn == N
    return pl.pallas_call(
        functools.partial(_ffn2_kernel, final=final),
        name="ffn2_final" if final else "ffn2",
        grid=(S // tm, N // tn, K // tk),
        in_specs=[
            pl.BlockSpec((tm, tk), lambda i, j, k: (i, k)),
            pl.BlockSpec((tk, tn), lambda i, j, k: (k, j)),
            pl.BlockSpec((tm, tn), lambda i, j, k: (i, j)),
            pl.BlockSpec((1, tn), lambda i, j, k: (0, j)),
        ],
        out_specs=pl.BlockSpec((tm, tn), lambda i, j, k: (i, j)),
        out_shape=jax.ShapeDtypeStruct((S, N), F32),
        compiler_params=_params(("parallel", "arbitrary", "arbitrary")),
    )(a, w, x, g_final)


def _log_sigmoid(t):
    return jnp.minimum(t, 0.0) - jnp.log1p(jnp.exp(-jnp.abs(t)))


def _split3(t):
    hi = t.astype(BF16)
    r1 = t - hi.astype(F32)
    mid = r1.astype(BF16)
    lo = (r1 - mid.astype(F32)).astype(BF16)
    return hi, mid, lo


GATE_ROWS = 16


def _mlstm_cumsum(g_t, tri):
    logf_t = _log_sigmoid(g_t)
    return logf_t, jnp.dot(jnp.concatenate(_split3(logf_t), axis=0), tri, preferred_element_type=F32)


def _mlstm_gates(g_t, logf_t, cs):
    b_t = cs[0:GATE_ROWS] + cs[GATE_ROWS:2 * GATE_ROWS] + cs[2 * GATE_ROWS:]
    tot_t = jnp.sum(logf_t, axis=1, keepdims=True)
    r_t = pltpu.roll(g_t, 4, 0) - b_t
    wend_t = tot_t + r_t
    mloc_t = jnp.max(wend_t, axis=1, keepdims=True)
    e_t = jnp.exp(wend_t - mloc_t)
    return b_t, r_t, tot_t, mloc_t, e_t


def _mixers_kernel(mqf, mktf, mvf, gf, rqf, rktf, rvf, mqb, mktb, mvb, gb, rqb, rktb, rvb,
                   bias_ref, tri_ref, sel_ref, decay_ref, inner_ref, zeta_ref,
                   cq_ref, ckv_ref, kr_ref, gq_ref, gkv_ref, wq_ref, wkv_ref, cos_ref, sin_ref,
                   hf_ref, hb_ref, yf_ref, yb_ref, q_out, k_out, v_out, c_ref, m_ref, r_ref, *, chunk_decay, cps):
    mla_refs = (cq_ref, ckv_ref, kr_ref, gq_ref, gkv_ref, wq_ref, wkv_ref, cos_ref, sin_ref, q_out, k_out, v_out)

    @pl.when(pl.program_id(0) == 0)
    def _():
        c_ref[...] = jnp.zeros(c_ref.shape, F32)
        m_ref[...] = jnp.full(m_ref.shape, NEG_INIT, F32)
        r_ref[...] = jnp.zeros(r_ref.shape, F32)

    L = CHUNK
    row = lax.broadcasted_iota(jnp.int32, (L, L), 0)
    col = lax.broadcasted_iota(jnp.int32, (L, L), 1)
    ones = jnp.ones((L, ML_DIM), BF16)
    bias = bias_ref[...]

    all_spans = [[slice(sub * L, (sub + 1) * L) for sub in (t, cps - 1 - t)] for t in range(cps)]

    items = [(d, h) for d in range(2) for h in range(ML_HEADS)]
    m_refs = ((mqf, mktf, mvf, hf_ref), (mqb, mktb, mvb, hb_ref))
    r_refs = ((rqf, rktf, rvf, yf_ref), (rqb, rktb, rvb, yb_ref))

    def m_operands(spans, d, h):
        q_ref, kt_ref, v_ref, _ = m_refs[d]
        sl = slice(h * ML_DIM, (h + 1) * ML_DIM)
        return q_ref[spans[d], sl], kt_ref[sl, spans[d]], v_ref[spans[d], sl]

    def r_operands(spans, d, h):
        q_ref, kt_ref, v_ref, _ = r_refs[d]
        return (q_ref[spans[d], h * RET_QK:(h + 1) * RET_QK], kt_ref[h * RET_QK:(h + 1) * RET_QK, spans[d]],
                v_ref[spans[d], h * RET_V:(h + 1) * RET_V])

    g_ts = [[g_ref[:, spans[d]] + bias for d, g_ref in enumerate((gf, gb))] for spans in all_spans]
    cums = [[_mlstm_cumsum(g_ts[t][d], tri_ref[1 - d]) for d in range(2)] for t in range(cps)]

    m_raw = [[jnp.dot(*m_operands(spans, d, h)[:2], preferred_element_type=F32) for d, h in items]
             for spans in all_spans]
    r_raw = [[jnp.dot(*r_operands(spans, d, h)[:2], preferred_element_type=F32) for d, h in items]
             for spans in all_spans]

    all_gates, all_bcast = [], []
    for t in range(cps):
        gates = [_mlstm_gates(g_ts[t][d], *cums[t][d]) for d in range(2)]
        x_rows = jnp.concatenate([part for g in gates for part in _split3(g[0])], axis=0)
        all_gates.append(gates)
        all_bcast.append(lax.dot_general(x_rows, sel_ref[...], _TN, preferred_element_type=F32))

    _mla_project(*mla_refs)

    m_cur = [m_ref[st:st + 1, :] for st in range(2 * ML_HEADS)]
    m_at, keep_at, gain_at = [], [], []
    for gates in all_gates:
        m_at.append(list(m_cur))
        keeps, gains = [], []
        for d, h in items:
            cf, st = 8 * d + 4 + h, ML_HEADS * d + h
            tot_t, mloc_t = gates[d][2], gates[d][3]
            blast = jnp.broadcast_to(tot_t[cf:cf + 1, :], (1, LANE))
            m_loc = jnp.broadcast_to(mloc_t[cf:cf + 1, :], (1, LANE))
            m_new = jnp.maximum(blast + m_cur[st], m_loc)
            keeps.append(jnp.exp(blast + m_cur[st] - m_new))
            gains.append(jnp.exp(m_loc - m_new))
            m_cur[st] = m_new
        keep_at.append(keeps)
        gain_at.append(gains)
    for st in range(2 * ML_HEADS):
        m_ref[st:st + 1, :] = m_cur[st]

    m_s, m_kw, m_winter, m_floor, r_s, r_kw = [], [], [], [], [], []
    for t, spans in enumerate(all_spans):
        ms_t, mkw_t, mwi_t, mfl_t, rs_t, rkw_t = [], [], [], [], [], []
        for j, (d, h) in enumerate(items):
            cf, st = 8 * d + 4 + h, ML_HEADS * d + h
            _, r_t, _, _, e_t = all_gates[t][d]
            mask = (row >= col) if d == 0 else (row <= col)
            bc = all_bcast[t][:, st * LANE:(st + 1) * LANE]
            dlog = jnp.where(mask, bc + r_t[cf:cf + 1, :], -jnp.inf)
            inter_log = bc + m_at[t][st]
            m_t = jnp.maximum(jnp.max(dlog, axis=1, keepdims=True), inter_log)
            ms_t.append((m_raw[t][j] * jnp.exp(dlog - m_t)).astype(BF16))
            mwi_t.append(jnp.exp(inter_log - m_t))
            mfl_t.append(jnp.exp(-m_t))
            kt = m_operands(spans, d, h)[1]
            mkw_t.append((kt.astype(F32) * e_t[cf:cf + 1, :]).astype(BF16))
            rs_t.append((r_raw[t][j] * decay_ref[st]).astype(BF16))
            rkt = r_operands(spans, d, h)[1]
            rkw_t.append((rkt.astype(F32) * zeta_ref[st:st + 1, :]).astype(BF16))
        m_s.append(ms_t), m_kw.append(mkw_t), m_winter.append(mwi_t), m_floor.append(mfl_t)
        r_s.append(rs_t), r_kw.append(rkw_t)

    c_cur = [c_ref[st] for st in range(2 * ML_HEADS)]
    r_cur = [r_ref[st] for st in range(2 * RET_HEADS)]
    for t, spans in enumerate(all_spans):
        v1s = [jnp.concatenate([m_operands(spans, d, h)[2], ones], axis=1) for d, h in items]
        intra = [jnp.dot(m_s[t][j], v1s[j], preferred_element_type=F32) for j in range(len(items))]
        inter = [jnp.dot(m_operands(spans, d, h)[0], c_cur[ML_HEADS * d + h].astype(BF16),
                         preferred_element_type=F32) for d, h in items]
        c_loc = [jnp.dot(m_kw[t][j], v1s[j], preferred_element_type=F32) for j in range(len(items))]
        ry_intra = [jnp.dot(r_s[t][j], r_operands(spans, d, h)[2], preferred_element_type=F32)
                    for j, (d, h) in enumerate(items)]
        ry_inter = [jnp.dot(r_operands(spans, d, h)[0], r_cur[RET_HEADS * d + h].astype(BF16),
                            preferred_element_type=F32) for d, h in items]
        r_loc = [jnp.dot(r_kw[t][j], r_operands(spans, d, h)[2], preferred_element_type=F32)
                 for j, (d, h) in enumerate(items)]
        for j, (d, h) in enumerate(items):
            st = ML_HEADS * d + h
            w_inter = m_winter[t][j]
            num = intra[j][:, :ML_DIM] + w_inter * inter[j][:, :ML_DIM]
            den = intra[j][:, ML_DIM:] + w_inter * inter[j][:, ML_DIM:]
            m_refs[d][3][spans[d], h * ML_DIM:(h + 1) * ML_DIM] = (
                num / jnp.maximum(jnp.abs(den), m_floor[t][j])).astype(BF16)
            keep, gain = keep_at[t][j], gain_at[t][j]
            c_cur[st] = (jnp.concatenate([keep, keep], axis=1) * c_cur[st]
                         + jnp.concatenate([gain, gain], axis=1) * c_loc[j])
            r_refs[d][3][spans[d], h * RET_V:(h + 1) * RET_V] = (
                ry_intra[j] + ry_inter[j] * inner_ref[st]).astype(BF16)
            r_cur[st] = chunk_decay[st] * r_cur[st] + r_loc[j]
    for st in range(2 * ML_HEADS):
        c_ref[st] = c_cur[st]
        r_ref[st] = r_cur[st]


def _gate_select_matrix():
    sel = np.zeros((2 * 3 * GATE_ROWS, 2 * ML_HEADS * LANE), np.float32)
    for d in range(2):
        for part in range(3):
            for h in range(ML_HEADS):
                st = ML_HEADS * d + h
                sel[(3 * d + part) * GATE_ROWS + 8 * d + 4 + h, st * LANE:(st + 1) * LANE] = 1.0
    return jnp.asarray(sel, BF16)


def _mixers(mq, mkt, rq, rkt, proj, gates_t, bias_t, layer, tri, sel, ret_consts, g_q, g_kv, w_q, w_kv, cos4, sin4,
            *, cps):
    S = proj.shape[0]
    q_rank, kv_rank = w_q.shape[1], w_kv.shape[1]
    rows = cps * CHUNK
    nb = S // rows
    MW = ML_HEADS * ML_DIM
    RQ = RET_HEADS * RET_QK
    RV = RET_HEADS * RET_V
    decay, inner, zeta, chunk_decay = ret_consts
    fwd = lambda n: n
    bwd = lambda n: nb - 1 - n

    def specs(idx):
        return [
            pl.BlockSpec((rows, MW), lambda n: (idx(n), 0)),
            pl.BlockSpec((MW, rows), lambda n: (0, idx(n))),
            pl.BlockSpec((rows, MW), lambda n: (idx(n), COL_ML_V * LANE // MW)),
            pl.BlockSpec((GATE_ROWS, rows), lambda n: (0, idx(n))),
            pl.BlockSpec((rows, RQ), lambda n: (idx(n), 0)),
            pl.BlockSpec((RQ, rows), lambda n: (0, idx(n))),
            pl.BlockSpec((rows, RV), lambda n: (idx(n), COL_RV * LANE // RV)),
        ]

    whole = lambda a: pl.BlockSpec(a.shape, lambda n: (0,) * a.ndim)
    out = lambda idx, w: pl.BlockSpec((rows, w), lambda n: (idx(n), 0))
    operands = (mq, mkt, proj, gates_t, rq, rkt, proj)
    return pl.pallas_call(
        functools.partial(_mixers_kernel, chunk_decay=chunk_decay, cps=cps),
        name="mixers",
        grid=(nb,),
        in_specs=specs(fwd) + specs(bwd) + [
            pl.BlockSpec((None, GATE_ROWS, CHUNK), lambda n: (layer, 0, 0)),
            whole(tri), whole(sel), whole(decay), whole(inner), whole(zeta),
            pl.BlockSpec((rows, q_rank), lambda n: (n, COL_CQ * LANE // q_rank)),
            pl.BlockSpec((rows, kv_rank), lambda n: (n, COL_CKV * LANE // kv_rank)),
            pl.BlockSpec((rows, LANE), lambda n: (n, COL_KROPE)),
            pl.BlockSpec((None, 1, q_rank), lambda n: (layer, 0, 0)),
            pl.BlockSpec((None, 1, kv_rank), lambda n: (layer, 0, 0)),
            pl.BlockSpec((None, q_rank, w_q.shape[2]), lambda n: (layer, 0, 0)),
            pl.BlockSpec((None, kv_rank, w_kv.shape[2]), lambda n: (layer, 0, 0)),
            pl.BlockSpec((rows, 2 * LANE), lambda n: (n, 0)),
            pl.BlockSpec((rows, 2 * LANE), lambda n: (n, 0)),
        ],
        out_specs=[out(fwd, MW), out(bwd, MW), out(fwd, RV), out(bwd, RV),
                   pl.BlockSpec((MLA_HEADS, rows, MLA_QK), lambda n: (0, n, 0)),
                   pl.BlockSpec((MLA_HEADS, rows, MLA_QK), lambda n: (0, n, 0)),
                   pl.BlockSpec((MLA_HEADS, rows, 2 * MLA_V), lambda n: (0, n, 0))],
        out_shape=[jax.ShapeDtypeStruct((S, MW), BF16), jax.ShapeDtypeStruct((S, MW), BF16),
                   jax.ShapeDtypeStruct((S, RV), BF16), jax.ShapeDtypeStruct((S, RV), BF16),
                   jax.ShapeDtypeStruct((MLA_HEADS, S, MLA_QK), BF16),
                   jax.ShapeDtypeStruct((MLA_HEADS, S, MLA_QK), BF16),
                   jax.ShapeDtypeStruct((MLA_HEADS, S, 2 * MLA_V), BF16)],
        scratch_shapes=[pltpu.VMEM((2 * ML_HEADS, ML_DIM, 2 * ML_DIM), F32), pltpu.VMEM((2 * ML_HEADS, LANE), F32),
                        pltpu.VMEM((2 * RET_HEADS, RET_QK, RET_V), F32)],
        compiler_params=_params(("arbitrary",)),
    )(*operands, *operands, bias_t, tri, sel, decay, inner, zeta, proj, proj, proj, g_q, g_kv, w_q, w_kv, cos4, sin4)


def _retention_consts():
    log_gamma = jnp.log1p(-jnp.exp2(-RET_DECAY_BASE - jnp.arange(RET_HEADS, dtype=F32)))
    idx = jnp.arange(CHUNK, dtype=F32)
    diff = idx[:, None] - idx[None, :]
    decay, inner, zeta = [], [], []
    for d in range(2):
        lg_d = log_gamma if d == 0 else log_gamma[::-1]
        dd = diff if d == 0 else -diff
        pos = idx if d == 0 else (CHUNK - 1.0 - idx)
        keep = dd >= 0
        for h in range(RET_HEADS):
            lg = lg_d[h]
            decay.append(jnp.where(keep, jnp.exp(jnp.where(keep, dd, 0.0) * lg), 0.0))
            inner.append(jnp.broadcast_to(jnp.exp((pos + 1.0) * lg)[:, None], (CHUNK, LANE)))
            zeta.append(jnp.exp((CHUNK - 1.0 - pos) * lg))
    lg_np = np.log1p(-np.exp2(-RET_DECAY_BASE - np.arange(RET_HEADS, dtype=np.float32))).astype(np.float32)
    chunk_decay = tuple(float(np.exp(np.float32(CHUNK) * lg)) for lg in list(lg_np) + list(lg_np[::-1]))
    return jnp.stack(decay), jnp.stack(inner), jnp.stack(zeta), chunk_decay


def _mla_project(cq_ref, ckv_ref, kr_ref, gq_ref, gkv_ref, wq_ref, wkv_ref, cos_ref, sin_ref,
                 q_out, k_out, v_out):
    tm = cq_ref.shape[0]
    cos4 = cos_ref[...]
    sin4 = sin_ref[...]
    qn = _rmsnorm(cq_ref[...].astype(F32), gq_ref[...]).astype(BF16)
    kvn = _rmsnorm(ckv_ref[...].astype(F32), gkv_ref[...]).astype(BF16)
    q_all = jnp.dot(qn, wq_ref[...], preferred_element_type=F32)
    kv_all = jnp.dot(kvn, wkv_ref[...], preferred_element_type=F32)
    q_scale = (MLA_QK ** -0.5) * math.log2(math.e)
    nope_w = MLA_HEADS * MLA_NOPE
    q_rope = _rope_grouped(q_all[:, nope_w:], jnp.concatenate([cos4, cos4], axis=1),
                           jnp.concatenate([sin4, sin4], axis=1)) * q_scale
    k_rope = _rope_grouped(kr_ref[...].astype(F32), cos4[:, :LANE], sin4[:, :LANE])[:, :ROPE_DIM].astype(BF16)
    ones = jnp.ones((tm, MLA_V), BF16)
    for h in range(MLA_HEADS):
        q_out[h, :, :MLA_NOPE] = (q_all[:, h * MLA_NOPE:(h + 1) * MLA_NOPE] * q_scale).astype(BF16)
        q_out[h, :, MLA_NOPE:] = q_rope[:, h * ROPE_DIM:(h + 1) * ROPE_DIM].astype(BF16)
        kv0 = h * (MLA_NOPE + MLA_V)
        k_out[h, :, :MLA_NOPE] = kv_all[:, kv0:kv0 + MLA_NOPE].astype(BF16)
        k_out[h, :, MLA_NOPE:] = k_rope
        v_out[h, :, :MLA_V] = kv_all[:, kv0 + MLA_NOPE:kv0 + MLA_NOPE + MLA_V].astype(BF16)
        v_out[h, :, MLA_V:] = ones


def _pack_w_in_block(w_ref, o_ref):
    n_in, tc = w_ref.shape
    ml_end = 4 * ML_HEADS * ML_DIM
    gates = 4 * ML_HEADS
    rest = n_in - ml_end - gates
    gate_row = IN_WIDTH_PACKED - LANE
    assert ml_end + rest + (LANE - ROPE_DIM) == gate_row
    o_ref[:ml_end, :] = w_ref[:ml_end, :].astype(BF16)
    o_ref[ml_end:ml_end + rest, :] = w_ref[ml_end + gates:, :].astype(BF16)
    o_ref[ml_end + rest:gate_row, :] = jnp.zeros((LANE - ROPE_DIM, tc), BF16)
    o_ref[gate_row:gate_row + gates, :] = w_ref[ml_end:ml_end + gates, :].astype(BF16)
    o_ref[gate_row + gates:, :] = jnp.zeros((LANE - gates, tc), BF16)


def _pack_w_in(w_in_t, n_layers, *, tc):
    _, n_in, K = w_in_t.shape
    return pl.pallas_call(
        _pack_w_in_block,
        name="pack_w_in",
        grid=(n_layers, K // tc),
        in_specs=[pl.BlockSpec((None, n_in, tc), lambda l, i: (l, 0, i))],
        out_specs=pl.BlockSpec((None, IN_WIDTH_PACKED, tc), lambda l, i: (l, 0, i)),
        out_shape=jax.ShapeDtypeStruct((n_layers, IN_WIDTH_PACKED, K), BF16),
        compiler_params=_params(("parallel", "parallel")),
    )(w_in_t)


def _attn_kernel(q_ref, k_ref, v_ref, *rest, tk, nk, n_cast, pack):
    n_in = n_cast + pack
    w_refs, o_ref, wb_refs = rest[:n_in], rest[n_in], rest[n_in + 1:]
    for w_ref, wb_ref in zip(w_refs[:n_cast], wb_refs[:n_cast]):
        wb_ref[...] = w_ref[...].astype(wb_ref.dtype)
    if pack:
        _pack_w_in_block(w_refs[n_cast], wb_refs[n_cast])

    q = q_ref[...]
    m = acc = None
    for c in range(nk):
        k = k_ref[c * tk:(c + 1) * tk, :]
        v = v_ref[c * tk:(c + 1) * tk, :]
        s = lax.dot_general(q, k, _NT, preferred_element_type=F32)
        m_c = jnp.max(s, axis=-1, keepdims=True)
        if c == 0:
            m = m_c
            acc = jnp.dot(jnp.exp2(s - m).astype(BF16), v, preferred_element_type=F32)
        else:
            m_new = jnp.maximum(m, m_c)
            acc = jnp.exp2(m - m_new) * acc + jnp.dot(jnp.exp2(s - m_new).astype(BF16), v, preferred_element_type=F32)
            m = m_new
    o_ref[...] = (acc[:, :MLA_V] / acc[:, MLA_V:]).astype(o_ref.dtype)


def _attention_steps(S, tq):
    return MLA_HEADS * (S // tq)


def _attention(q, k, v, cast_weights, layer, pack_src, *, tq, tk, tc):
    S = q.shape[1]
    nq = S // tq
    n_steps = _attention_steps(S, tq)
    step = lambda h, i: h * nq + i
    w_specs, wb_specs, wb_shapes = [], [], []
    for w in cast_weights:
        _, rows, cols = w.shape
        rb = rows // n_steps
        assert rb * n_steps == rows and rb % BF16_ROWS == 0
        w_specs.append(pl.BlockSpec((None, rb, cols), lambda h, i: (layer, step(h, i), 0)))
        wb_specs.append(pl.BlockSpec((rb, cols), lambda h, i: (step(h, i), 0)))
        wb_shapes.append(jax.ShapeDtypeStruct((rows, cols), BF16))
    operands = list(cast_weights)
    if pack_src is not None:
        L, n_in, K = pack_src.shape
        kb = K // tc
        n_blocks = (L - 1) * kb
        assert n_blocks <= n_steps
        blk = lambda h, i: jnp.minimum(step(h, i), n_blocks - 1)
        w_specs.append(pl.BlockSpec((None, n_in, tc), lambda h, i: (1 + blk(h, i) // kb, 0, blk(h, i) % kb)))
        wb_specs.append(pl.BlockSpec((None, IN_WIDTH_PACKED, tc), lambda h, i: (blk(h, i) // kb, 0, blk(h, i) % kb)))
        wb_shapes.append(jax.ShapeDtypeStruct((L - 1, IN_WIDTH_PACKED, K), BF16))
        operands.append(pack_src)
    outs = pl.pallas_call(
        functools.partial(_attn_kernel, tk=tk, nk=S // tk, n_cast=len(cast_weights), pack=pack_src is not None),
        name="attention",
        grid=(MLA_HEADS, nq),
        in_specs=[
            pl.BlockSpec((None, tq, MLA_QK), lambda h, i: (h, i, 0)),
            pl.BlockSpec((None, S, MLA_QK), lambda h, i: (h, 0, 0)),
            pl.BlockSpec((None, S, 2 * MLA_V), lambda h, i: (h, 0, 0)),
        ] + w_specs,
        out_specs=[pl.BlockSpec((tq, MLA_V), lambda h, i: (i, h))] + wb_specs,
        out_shape=[jax.ShapeDtypeStruct((S, MLA_HEADS * MLA_V), BF16)] + wb_shapes,
        compiler_params=_params(("arbitrary", "arbitrary")),
    )(q, k, v, *operands)
    return outs[0], outs[1:]


def _tiles(S, D, F):
    cap = lambda n: min(n, S)
    return dict(
        row=cap(512),
        row_big=cap(1024),
        mixer_cps=2 if S % (2 * CHUNK) == 0 else 1,
        attn_tq=cap(2048), attn_tk=cap(256),
        ffn1_tm=cap(2048), ffn1_tn=min(1024, F),
        ffn2_tn=min(1024, D), ffn2_tk=min(4096, F),
        ffn2_final_tk=min(2048, F),
        pack_tc=min(256, D),
    )


def kernel(x, positions, g_mix, w_in, b_gates, w_conv, g_ml_out, g_ret_out, g_q_norm, w_q_up, g_kv_norm, w_kv_up,
           w_out, g_ffn, w_ff1, w_ff2, g_final):
    B, S, D = x.shape
    assert B == 1
    depth = w_in.shape[0]
    t = _tiles(S, D, w_ff1.shape[-1])

    half = ROPE_DIM // 2
    lane = jnp.arange(RET_HEADS * ROPE_DIM)
    inv = ROPE_THETA ** (-(lane % half).astype(F32) / half)
    ang = positions[0].astype(F32)[:, None] * inv
    cos4 = jnp.cos(ang)
    sin4 = jnp.where((lane % ROPE_DIM) < half, -jnp.sin(ang), jnp.sin(ang))

    w_in_t = jnp.swapaxes(w_in, 1, 2)
    ride_pack = depth > 1 and (depth - 1) * (D // t["pack_tc"]) <= _attention_steps(S, t["attn_tq"])
    w_in_first = _pack_w_in(w_in_t, 1 if ride_pack else depth, tc=t["pack_tc"])
    w_in_rest = None
    q_rank = w_q_up.shape[1]
    w_q4 = w_q_up.astype(BF16).reshape(depth, q_rank, MLA_HEADS, MLA_QK)
    w_q = jnp.concatenate([w_q4[..., :MLA_NOPE].reshape(depth, q_rank, MLA_HEADS * MLA_NOPE),
                           w_q4[..., MLA_NOPE:].reshape(depth, q_rank, MLA_HEADS * ROPE_DIM)], axis=-1)
    w_kv = w_kv_up.astype(BF16)
    bias_t = jnp.broadcast_to(b_gates[:, :, None], (depth, GATE_ROWS, CHUNK))
    idx = jnp.arange(CHUNK)
    tri = jnp.stack([idx[:, None] >= idx[None, :], idx[:, None] <= idx[None, :]]).astype(BF16)
    sel = _gate_select_matrix()
    ret_consts = _retention_consts()
    r3 = lambda g: g[:, None, :]

    xs = x[0]
    for l in range(depth):
        w_in_l, l_in = (w_in_first, l) if (l == 0 or not ride_pack) else (w_in_rest, l - 1)
        proj, gates_t, mq, mkt, rq, rkt = _in_proj(xs, r3(g_mix), w_in_l, w_conv, cos4, sin4, l, l_in,
                                                   tm=t["row_big"])
        h_f, h_b, y_f, y_b, q, k, v = _mixers(mq, mkt, rq, rkt, proj, gates_t, bias_t, l, tri, sel, ret_consts,
                                              r3(g_q_norm), r3(g_kv_norm), w_q, w_kv, cos4, sin4, cps=t["mixer_cps"])
        y_mla, prepared = _attention(q, k, v, (w_out, w_ff1, w_ff2), l, w_in_t if (ride_pack and l == 0) else None,
                                     tq=t["attn_tq"], tk=t["attn_tk"], tc=t["pack_tc"])
        w_out_b, w_ff1_b, w_ff2_b = prepared[:3]
        if ride_pack and l == 0:
            w_in_rest = prepared[3]
        xs, hn = _out_proj(h_f, h_b, y_f, y_b, proj, y_mla, r3(g_ml_out), r3(g_ret_out), r3(g_ffn), l, w_out_b, xs,
                           tm=t["row"])
        act = _ffn1(hn, w_ff1_b, tm=t["ffn1_tm"], tn=t["ffn1_tn"])
        if l < depth - 1:
            xs = _ffn2(act, w_ff2_b, xs, g_final[None, :], tm=t["row_big"], tn=t["ffn2_tn"], tk=t["ffn2_tk"],
                       final=False)
        else:
            xs = _ffn2(act, w_ff2_b, xs, g_final[None, :], tm=t["row_big"], tn=D, tk=t["ffn2_final_tk"], final=True)
    return xs[None]
```

```python
import functools
import math

import numpy as np
import jax
import jax.numpy as jnp
from jax import lax
from jax.experimental import pallas as pl
from jax.experimental.pallas import tpu as pltpu

F32 = jnp.float32
BF16 = jnp.bfloat16

EPS = 1e-6
NEG_INIT = -1e30
ROPE_THETA = 10000.0
CHUNK = 128
LANE = 128
BF16_ROWS = 16
ML_HEADS = 4
ML_DIM = 128
RET_HEADS = 4
RET_QK = 64
RET_V = 128
RET_DECAY_BASE = 5.0
MLA_HEADS = 8
MLA_NOPE = 128
ROPE_DIM = 64
MLA_V = 128
MLA_QK = MLA_NOPE + ROPE_DIM
VMEM_LIMIT = 60 * 1024 * 1024

COL_ML_Q, COL_ML_K, COL_ML_V, COL_ML_O = 0, 4, 8, 12
COL_RQK, COL_RV, COL_RG = 16, 20, 24
COL_CQ, COL_CKV, COL_KROPE, COL_GATES = 28, 32, 34, 35
IN_WIDTH_PACKED = 36 * LANE

_NT = (((1,), (1,)), ((), ()))
_TN = (((0,), (0,)), ((), ()))


def _params(sem):
    return pltpu.CompilerParams(dimension_semantics=sem, vmem_limit_bytes=VMEM_LIMIT)


def _rmsnorm(x, g):
    return x * lax.rsqrt(jnp.mean(x * x, axis=-1, keepdims=True) + EPS) * g


def _rope_grouped(t, cosf, sinf):
    width = t.shape[1]
    lane = lax.broadcasted_iota(jnp.int32, t.shape, 1)
    first_half = (lane % ROPE_DIM) < (ROPE_DIM // 2)
    partner = jnp.where(first_half, pltpu.roll(t, width - ROPE_DIM // 2, 1), pltpu.roll(t, ROPE_DIM // 2, 1))
    return t * cosf + partner * sinf


IN_TN = 12 * LANE
MQK_W = 2 * ML_HEADS * ML_DIM
RQK_W = 2 * RET_HEADS * RET_QK
RQK_OFS = COL_RQK * LANE - IN_TN


def _in_proj_kernel(x_ref, xprev_ref, xnext_ref, g_ref, w_ref, wc_ref, cos_ref, sin_ref,
                    o_ref, gates_ref, mq_out, mkt_out, rq_out, rkt_out, hn_ref, *, nblk):
    i, j = pl.program_id(0), pl.program_id(1)
    tm = x_ref.shape[0]

    def project(hn):
        y = lax.dot_general(hn, w_ref[...], _NT, preferred_element_type=F32)
        o_ref[...] = y.astype(o_ref.dtype)
        return y

    @pl.when(j == 0)
    def _():
        g = g_ref[...]
        hn = _rmsnorm(x_ref[...], g).astype(hn_ref.dtype)
        hn_ref[...] = hn
        halo = jnp.concatenate([_rmsnorm(xprev_ref[...], g), _rmsnorm(xnext_ref[...], g)], axis=0).astype(BF16)
        y = project(hn)[:, :MQK_W]
        yh = lax.dot_general(halo, w_ref[:MQK_W, :], _NT, preferred_element_type=F32)
        prev_row = jnp.where(i > 0, yh[7:8, :], 0.0)
        next_row = jnp.where(i < nblk - 1, yh[8:9, :], 0.0)
        wc = wc_ref[...]
        w0, w1, w2 = wc[0:1, :], wc[1:2, :], wc[2:3, :]
        c = pltpu.roll(y, 1, 0) * w0 + y * w1 + pltpu.roll(y, tm - 1, 0) * w2
        c_first = prev_row * w0 + y[0:1, :] * w1 + y[1:2, :] * w2
        c_last = y[tm - 2:tm - 1, :] * w0 + y[tm - 1:tm, :] * w1 + next_row * w2
        row8 = lax.broadcasted_iota(jnp.int32, (8, MQK_W), 0)
        c = jnp.concatenate([jnp.where(row8 == 0, c_first, c[0:8, :]), c[8:tm - 8, :],
                             jnp.where(row8 == 7, c_last, c[tm - 8:, :])], axis=0)
        c = c / (1.0 + jnp.exp(-c))
        half = MQK_W // 2
        mq_out[...] = (c[:, :half] * (ML_DIM ** -0.5)).astype(mq_out.dtype)
        mkt_out[...] = c[:, half:].T.astype(mkt_out.dtype)

    @pl.when(j == 1)
    def _():
        y = project(hn_ref[...])
        cosf, sinf = cos_ref[...], sin_ref[...]
        half = RQK_W // 2
        rq_out[...] = _rope_grouped(y[:, RQK_OFS:RQK_OFS + half], cosf, sinf).astype(rq_out.dtype)
        rk = _rope_grouped(y[:, RQK_OFS + half:RQK_OFS + RQK_W], cosf, sinf) * (RET_QK ** -0.5)
        rkt_out[...] = rk.T.astype(rkt_out.dtype)

    @pl.when(j == 2)
    def _():
        y = project(hn_ref[...])
        gates_ref[...] = y[:, y.shape[1] - LANE:].T


def _in_proj(x, g, w, w_conv, cos4, sin4, layer, w_layer, *, tm):
    S, K = x.shape
    N = w.shape[1]
    assert N == 3 * IN_TN and MQK_W <= IN_TN and 0 <= RQK_OFS and RQK_OFS + RQK_W <= IN_TN
    nblk = S // tm
    hb = tm // 8
    row_blk = lambda width: pl.BlockSpec((tm, width), lambda i, j: (i, 0))
    col_blk = lambda height: pl.BlockSpec((height, tm), lambda i, j: (0, i))
    return pl.pallas_call(
        functools.partial(_in_proj_kernel, nblk=nblk),
        name="in_proj",
        grid=(nblk, N // IN_TN),
        in_specs=[
            row_blk(K),
            pl.BlockSpec((8, K), lambda i, j: (jnp.maximum(i * hb - 1, 0), 0)),
            pl.BlockSpec((8, K), lambda i, j: (jnp.minimum((i + 1) * hb, S // 8 - 1), 0)),
            pl.BlockSpec((None, 1, K), lambda i, j: (layer, 0, 0)),
            pl.BlockSpec((None, IN_TN, K), lambda i, j: (w_layer, j, 0)),
            pl.BlockSpec((None, 3, MQK_W), lambda i, j: (layer, 0, 0)),
            row_blk(RQK_W // 2), row_blk(RQK_W // 2),
        ],
        out_specs=[pl.BlockSpec((tm, IN_TN), lambda i, j: (i, j)), col_blk(LANE),
                   row_blk(MQK_W // 2), col_blk(MQK_W // 2), row_blk(RQK_W // 2), col_blk(RQK_W // 2)],
        out_shape=[jax.ShapeDtypeStruct((S, N), BF16), jax.ShapeDtypeStruct((LANE, S), F32),
                   jax.ShapeDtypeStruct((S, MQK_W // 2), BF16), jax.ShapeDtypeStruct((MQK_W // 2, S), BF16),
                   jax.ShapeDtypeStruct((S, RQK_W // 2), BF16), jax.ShapeDtypeStruct((RQK_W // 2, S), BF16)],
        scratch_shapes=[pltpu.VMEM((tm, K), BF16)],
        compiler_params=_params(("parallel", "arbitrary")),
    )(x, x, x, g, w, w_conv, cos4, sin4)


def _out_proj_kernel(hf, hb, o_ref, gml_ref, yf, yb, rg_ref, gret_ref, ymla_ref, w_ref, x_ref, gffn_ref,
                     out_ref, hn_out):
    w_ml = ML_HEADS * ML_DIM
    w_ret = RET_HEADS * RET_V
    acc = x_ref[...] + jnp.dot(ymla_ref[...], w_ref[w_ml + w_ret:, :], preferred_element_type=F32)
    y_ml, y_ret = [], []
    for h in range(ML_HEADS):
        sl = slice(h * ML_DIM, (h + 1) * ML_DIM)
        y = _rmsnorm(hf[:, sl].astype(F32) + hb[:, sl].astype(F32), gml_ref[:, sl])
        y_ml.append((y / (1.0 + jnp.exp(-o_ref[:, sl].astype(F32)))).astype(BF16))
    for h in range(RET_HEADS):
        sl = slice(h * RET_V, (h + 1) * RET_V)
        y = _rmsnorm(yf[:, sl].astype(F32) + yb[:, sl].astype(F32), gret_ref[:, sl])
        gate = rg_ref[:, sl].astype(F32)
        y_ret.append((gate / (1.0 + jnp.exp(-gate)) * y).astype(BF16))
    acc = acc + jnp.dot(jnp.concatenate(y_ml, axis=1), w_ref[:w_ml, :], preferred_element_type=F32)
    acc = acc + jnp.dot(jnp.concatenate(y_ret, axis=1), w_ref[w_ml:w_ml + w_ret, :], preferred_element_type=F32)
    out_ref[...] = acc
    hn_out[...] = _rmsnorm(acc, gffn_ref[...]).astype(hn_out.dtype)


def _out_proj(h_f, h_b, y_f, y_b, proj, y_mla, g_ml, g_ret, g_ffn, layer, w, x, *, tm):
    S, N = x.shape
    W = ML_HEADS * ML_DIM
    blk = lambda c: pl.BlockSpec((tm, W), lambda i: (i, c))
    gain = pl.BlockSpec((None, 1, W), lambda i: (layer, 0, 0))
    return pl.pallas_call(
        _out_proj_kernel,
        name="out_proj",
        grid=(S // tm,),
        in_specs=[blk(0), blk(0), blk(COL_ML_O * LANE // W), gain, blk(0), blk(0), blk(COL_RG * LANE // W), gain,
                  pl.BlockSpec((tm, y_mla.shape[1]), lambda i: (i, 0)),
                  pl.BlockSpec(w.shape, lambda i: (0, 0)),
                  pl.BlockSpec((tm, N), lambda i: (i, 0)),
                  pl.BlockSpec((None, 1, N), lambda i: (layer, 0, 0))],
        out_specs=[pl.BlockSpec((tm, N), lambda i: (i, 0)), pl.BlockSpec((tm, N), lambda i: (i, 0))],
        out_shape=[jax.ShapeDtypeStruct((S, N), F32), jax.ShapeDtypeStruct((S, N), BF16)],
        compiler_params=_params(("parallel",)),
    )(h_f, h_b, proj, g_ml, y_f, y_b, proj, g_ret, y_mla, w, x, g_ffn)


def _ffn1_kernel(hn_ref, w_ref, o_ref):
    y = jnp.dot(hn_ref[...], w_ref[...], preferred_element_type=F32)
    o_ref[...] = jnp.square(jnp.maximum(y, 0.0)).astype(o_ref.dtype)


def _ffn1(hn, w, *, tm, tn):
    S, K = hn.shape
    N = w.shape[-1]
    return pl.pallas_call(
        _ffn1_kernel,
        name="ffn1",
        grid=(S // tm, N // tn),
        in_specs=[pl.BlockSpec((tm, K), lambda i, j: (i, 0)), pl.BlockSpec((K, tn), lambda i, j: (0, j))],
        out_specs=pl.BlockSpec((tm, tn), lambda i, j: (i, j)),
        out_shape=jax.ShapeDtypeStruct((S, N), BF16),
        compiler_params=_params(("parallel", "arbitrary")),
    )(hn, w)


def _ffn2_kernel(a_ref, w_ref, x_ref, gf_ref, o_ref, *, final):
    k = pl.program_id(2)

    @pl.when(k == 0)
    def _():
        o_ref[...] = x_ref[...]

    o_ref[...] += jnp.dot(a_ref[...], w_ref[...], preferred_element_type=F32)

    if final:
        @pl.when(k == pl.num_programs(2) - 1)
        def _():
            o_ref[...] = _rmsnorm(o_ref[...], gf_ref[...])


def _ffn2(a, w, x, g_final, *, tm, tn, tk, final):
    S, N = x.shape
    K = a.shape[1]
    assert not final or tn == N
    return pl.pallas_call(
        functools.partial(_ffn2_kernel, final=final),
        name="ffn2_final" if final else "ffn2",
        grid=(S // tm, N // tn, K // tk),
        in_specs=[
            pl.BlockSpec((tm, tk), lambda i, j, k: (i, k)),
            pl.BlockSpec((tk, tn), lambda i, j, k: (k, j)),
            pl.BlockSpec((tm, tn), lambda i, j, k: (i, j)),
            pl.BlockSpec((1, tn), lambda i, j, k: (0, j)),
        ],
        out_specs=pl.BlockSpec((tm, tn), lambda i, j, k: (i, j)),
        out_shape=jax.ShapeDtypeStruct((S, N), F32),
        compiler_params=_params(("parallel", "arbitrary", "arbitrary")),
    )(a, w, x, g_final)


def _log_sigmoid(t):
    return jnp.minimum(t, 0.0) - jnp.log1p(jnp.exp(-jnp.abs(t)))


def _split3(t):
    hi = t.astype(BF16)
    r1 = t - hi.astype(F32)
    mid = r1.astype(BF16)
    lo = (r1 - mid.astype(F32)).astype(BF16)
    return hi, mid, lo


GATE_ROWS = 16


def _mlstm_cumsum(g_t, tri):
    logf_t = _log_sigmoid(g_t)
    return logf_t, jnp.dot(jnp.concatenate(_split3(logf_t), axis=0), tri, preferred_element_type=F32)


def _mlstm_gates(g_t, logf_t, cs):
    b_t = cs[0:GATE_ROWS] + cs[GATE_ROWS:2 * GATE_ROWS] + cs[2 * GATE_ROWS:]
    tot_t = jnp.sum(logf_t, axis=1, keepdims=True)
    r_t = pltpu.roll(g_t, 4, 0) - b_t
    wend_t = tot_t + r_t
    mloc_t = jnp.max(wend_t, axis=1, keepdims=True)
    e_t = jnp.exp(wend_t - mloc_t)
    return b_t, r_t, tot_t, mloc_t, e_t


def _mixers_kernel(mqf, mktf, mvf, gf, rqf, rktf, rvf, mqb, mktb, mvb, gb, rqb, rktb, rvb,
                   bias_ref, tri_ref, sel_ref, decay_ref, inner_ref, zeta_ref,
                   cq_ref, ckv_ref, kr_ref, gq_ref, gkv_ref, wq_ref, wkv_ref, cos_ref, sin_ref,
                   hf_ref, hb_ref, yf_ref, yb_ref, q_out, k_out, v_out, c_ref, m_ref, r_ref, *, chunk_decay, cps):
    mla_refs = (cq_ref, ckv_ref, kr_ref, gq_ref, gkv_ref, wq_ref, wkv_ref, cos_ref, sin_ref, q_out, k_out, v_out)

    @pl.when(pl.program_id(0) == 0)
    def _():
        c_ref[...] = jnp.zeros(c_ref.shape, F32)
        m_ref[...] = jnp.full(m_ref.shape, NEG_INIT, F32)
        r_ref[...] = jnp.zeros(r_ref.shape, F32)

    L = CHUNK
    row = lax.broadcasted_iota(jnp.int32, (L, L), 0)
    col = lax.broadcasted_iota(jnp.int32, (L, L), 1)
    ones = jnp.ones((L, ML_DIM), BF16)
    bias = bias_ref[...]

    all_spans = [[slice(sub * L, (sub + 1) * L) for sub in (t, cps - 1 - t)] for t in range(cps)]

    items = [(d, h) for d in range(2) for h in range(ML_HEADS)]
    m_refs = ((mqf, mktf, mvf, hf_ref), (mqb, mktb, mvb, hb_ref))
    r_refs = ((rqf, rktf, rvf, yf_ref), (rqb, rktb, rvb, yb_ref))

    def m_operands(spans, d, h):
        q_ref, kt_ref, v_ref, _ = m_refs[d]
        sl = slice(h * ML_DIM, (h + 1) * ML_DIM)
        return q_ref[spans[d], sl], kt_ref[sl, spans[d]], v_ref[spans[d], sl]

    def r_operands(spans, d, h):
        q_ref, kt_ref, v_ref, _ = r_refs[d]
        return (q_ref[spans[d], h * RET_QK:(h + 1) * RET_QK], kt_ref[h * RET_QK:(h + 1) * RET_QK, spans[d]],
                v_ref[spans[d], h * RET_V:(h + 1) * RET_V])

    g_ts = [[g_ref[:, spans[d]] + bias for d, g_ref in enumerate((gf, gb))] for spans in all_spans]
    cums = [[_mlstm_cumsum(g_ts[t][d], tri_ref[1 - d]) for d in range(2)] for t in range(cps)]

    m_raw = [[jnp.dot(*m_operands(spans, d, h)[:2], preferred_element_type=F32) for d, h in items]
             for spans in all_spans]
    r_raw = [[jnp.dot(*r_operands(spans, d, h)[:2], preferred_element_type=F32) for d, h in items]
             for spans in all_spans]

    all_gates, all_bcast = [], []
    for t in range(cps):
        gates = [_mlstm_gates(g_ts[t][d], *cums[t][d]) for d in range(2)]
        x_rows = jnp.concatenate([part for g in gates for part in _split3(g[0])], axis=0)
        all_gates.append(gates)
        all_bcast.append(lax.dot_general(x_rows, sel_ref[...], _TN, preferred_element_type=F32))

    _mla_project(*mla_refs)

    m_cur = [m_ref[st:st + 1, :] for st in range(2 * ML_HEADS)]
    m_at, keep_at, gain_at = [], [], []
    for gates in all_gates:
        m_at.append(list(m_cur))
        keeps, gains = [], []
        for d, h in items:
            cf, st = 8 * d + 4 + h, ML_HEADS * d + h
            tot_t, mloc_t = gates[d][2], gates[d][3]
            blast = jnp.broadcast_to(tot_t[cf:cf + 1, :], (1, LANE))
            m_loc = jnp.broadcast_to(mloc_t[cf:cf + 1, :], (1, LANE))
            m_new = jnp.maximum(blast + m_cur[st], m_loc)
            keeps.append(jnp.exp(blast + m_cur[st] - m_new))
            gains.append(jnp.exp(m_loc - m_new))
            m_cur[st] = m_new
        keep_at.append(keeps)
        gain_at.append(gains)
    for st in range(2 * ML_HEADS):
        m_ref[st:st + 1, :] = m_cur[st]

    m_s, m_kw, m_winter, m_floor, r_s, r_kw = [], [], [], [], [], []
    for t, spans in enumerate(all_spans):
        ms_t, mkw_t, mwi_t, mfl_t, rs_t, rkw_t = [], [], [], [], [], []
        for j, (d, h) in enumerate(items):
            cf, st = 8 * d + 4 + h, ML_HEADS * d + h
            _, r_t, _, _, e_t = all_gates[t][d]
            mask = (row >= col) if d == 0 else (row <= col)
            bc = all_bcast[t][:, st * LANE:(st + 1) * LANE]
            dlog = jnp.where(mask, bc + r_t[cf:cf + 1, :], -jnp.inf)
            inter_log = bc + m_at[t][st]
            m_t = jnp.maximum(jnp.max(dlog, axis=1, keepdims=True), inter_log)
            ms_t.append((m_raw[t][j] * jnp.exp(dlog - m_t)).astype(BF16))
            mwi_t.append(jnp.exp(inter_log - m_t))
            mfl_t.append(jnp.exp(-m_t))
            kt = m_operands(spans, d, h)[1]
            mkw_t.append((kt.astype(F32) * e_t[cf:cf + 1, :]).astype(BF16))
            rs_t.append((r_raw[t][j] * decay_ref[st]).astype(BF16))
            rkt = r_operands(spans, d, h)[1]
            rkw_t.append((rkt.astype(F32) * zeta_ref[st:st + 1, :]).astype(BF16))
        m_s.append(ms_t), m_kw.append(mkw_t), m_winter.append(mwi_t), m_floor.append(mfl_t)
        r_s.append(rs_t), r_kw.append(rkw_t)

    c_cur = [c_ref[st] for st in range(2 * ML_HEADS)]
    r_cur = [r_ref[st] for st in range(2 * RET_HEADS)]
    for t, spans in enumerate(all_spans):
        v1s = [jnp.concatenate([m_operands(spans, d, h)[2], ones], axis=1) for d, h in items]
        intra = [jnp.dot(m_s[t][j], v1s[j], preferred_element_type=F32) for j in range(len(items))]
        inter = [jnp.dot(m_operands(spans, d, h)[0], c_cur[ML_HEADS * d + h].astype(BF16),
                         preferred_element_type=F32) for d, h in items]
        c_loc = [jnp.dot(m_kw[t][j], v1s[j], preferred_element_type=F32) for j in range(len(items))]
        ry_intra = [jnp.dot(r_s[t][j], r_operands(spans, d, h)[2], preferred_element_type=F32)
                    for j, (d, h) in enumerate(items)]
        ry_inter = [jnp.dot(r_operands(spans, d, h)[0], r_cur[RET_HEADS * d + h].astype(BF16),
                            preferred_element_type=F32) for d, h in items]
        r_loc = [jnp.dot(r_kw[t][j], r_operands(spans, d, h)[2], preferred_element_type=F32)
                 for j, (d, h) in enumerate(items)]
        for j, (d, h) in enumerate(items):
            st = ML_HEADS * d + h
            w_inter = m_winter[t][j]
            num = intra[j][:, :ML_DIM] + w_inter * inter[j][:, :ML_DIM]
            den = intra[j][:, ML_DIM:] + w_inter * inter[j][:, ML_DIM:]
            m_refs[d][3][spans[d], h * ML_DIM:(h + 1) * ML_DIM] = (
                num / jnp.maximum(jnp.abs(den), m_floor[t][j])).astype(BF16)
            keep, gain = keep_at[t][j], gain_at[t][j]
            c_cur[st] = (jnp.concatenate([keep, keep], axis=1) * c_cur[st]
                         + jnp.concatenate([gain, gain], axis=1) * c_loc[j])
            r_refs[d][3][spans[d], h * RET_V:(h + 1) * RET_V] = (
                ry_intra[j] + ry_inter[j] * inner_ref[st]).astype(BF16)
            r_cur[st] = chunk_decay[st] * r_cur[st] + r_loc[j]
    for st in range(2 * ML_HEADS):
        c_ref[st] = c_cur[st]
        r_ref[st] = r_cur[st]


def _gate_select_matrix():
    sel = np.zeros((2 * 3 * GATE_ROWS, 2 * ML_HEADS * LANE), np.float32)
    for d in range(2):
        for part in range(3):
            for h in range(ML_HEADS):
                st = ML_HEADS * d + h
                sel[(3 * d + part) * GATE_ROWS + 8 * d + 4 + h, st * LANE:(st + 1) * LANE] = 1.0
    return jnp.asarray(sel, BF16)


def _mixers(mq, mkt, rq, rkt, proj, gates_t, bias_t, layer, tri, sel, ret_consts, g_q, g_kv, w_q, w_kv, cos4, sin4,
            *, cps):
    S = proj.shape[0]
    q_rank, kv_rank = w_q.shape[1], w_kv.shape[1]
    rows = cps * CHUNK
    nb = S // rows
    MW = ML_HEADS * ML_DIM
    RQ = RET_HEADS * RET_QK
    RV = RET_HEADS * RET_V
    decay, inner, zeta, chunk_decay = ret_consts
    fwd = lambda n: n
    bwd = lambda n: nb - 1 - n

    def specs(idx):
        return [
            pl.BlockSpec((rows, MW), lambda n: (idx(n), 0)),
            pl.BlockSpec((MW, rows), lambda n: (0, idx(n))),
            pl.BlockSpec((rows, MW), lambda n: (idx(n), COL_ML_V * LANE // MW)),
            pl.BlockSpec((GATE_ROWS, rows), lambda n: (0, idx(n))),
            pl.BlockSpec((rows, RQ), lambda n: (idx(n), 0)),
            pl.BlockSpec((RQ, rows), lambda n: (0, idx(n))),
            pl.BlockSpec((rows, RV), lambda n: (idx(n), COL_RV * LANE // RV)),
        ]

    whole = lambda a: pl.BlockSpec(a.shape, lambda n: (0,) * a.ndim)
    out = lambda idx, w: pl.BlockSpec((rows, w), lambda n: (idx(n), 0))
    operands = (mq, mkt, proj, gates_t, rq, rkt, proj)
    return pl.pallas_call(
        functools.partial(_mixers_kernel, chunk_decay=chunk_decay, cps=cps),
        name="mixers",
        grid=(nb,),
        in_specs=specs(fwd) + specs(bwd) + [
            pl.BlockSpec((None, GATE_ROWS, CHUNK), lambda n: (layer, 0, 0)),
            whole(tri), whole(sel), whole(decay), whole(inner), whole(zeta),
            pl.BlockSpec((rows, q_rank), lambda n: (n, COL_CQ * LANE // q_rank)),
            pl.BlockSpec((rows, kv_rank), lambda n: (n, COL_CKV * LANE // kv_rank)),
            pl.BlockSpec((rows, LANE), lambda n: (n, COL_KROPE)),
            pl.BlockSpec((None, 1, q_rank), lambda n: (layer, 0, 0)),
            pl.BlockSpec((None, 1, kv_rank), lambda n: (layer, 0, 0)),
            pl.BlockSpec((None, q_rank, w_q.shape[2]), lambda n: (layer, 0, 0)),
            pl.BlockSpec((None, kv_rank, w_kv.shape[2]), lambda n: (layer, 0, 0)),
            pl.BlockSpec((rows, 2 * LANE), lambda n: (n, 0)),
            pl.BlockSpec((rows, 2 * LANE), lambda n: (n, 0)),
        ],
        out_specs=[out(fwd, MW), out(bwd, MW), out(fwd, RV), out(bwd, RV),
                   pl.BlockSpec((MLA_HEADS, rows, MLA_QK), lambda n: (0, n, 0)),
                   pl.BlockSpec((MLA_HEADS, rows, MLA_QK), lambda n: (0, n, 0)),
                   pl.BlockSpec((MLA_HEADS, rows, 2 * MLA_V), lambda n: (0, n, 0))],
        out_shape=[jax.ShapeDtypeStruct((S, MW), BF16), jax.ShapeDtypeStruct((S, MW), BF16),
                   jax.ShapeDtypeStruct((S, RV), BF16), jax.ShapeDtypeStruct((S, RV), BF16),
                   jax.ShapeDtypeStruct((MLA_HEADS, S, MLA_QK), BF16),
                   jax.ShapeDtypeStruct((MLA_HEADS, S, MLA_QK), BF16),
                   jax.ShapeDtypeStruct((MLA_HEADS, S, 2 * MLA_V), BF16)],
        scratch_shapes=[pltpu.VMEM((2 * ML_HEADS, ML_DIM, 2 * ML_DIM), F32), pltpu.VMEM((2 * ML_HEADS, LANE), F32),
                        pltpu.VMEM((2 * RET_HEADS, RET_QK, RET_V), F32)],
        compiler_params=_params(("arbitrary",)),
    )(*operands, *operands, bias_t, tri, sel, decay, inner, zeta, proj, proj, proj, g_q, g_kv, w_q, w_kv, cos4, sin4)


def _retention_consts():
    log_gamma = jnp.log1p(-jnp.exp2(-RET_DECAY_BASE - jnp.arange(RET_HEADS, dtype=F32)))
    idx = jnp.arange(CHUNK, dtype=F32)
    diff = idx[:, None] - idx[None, :]
    decay, inner, zeta = [], [], []
    for d in range(2):
        lg_d = log_gamma if d == 0 else log_gamma[::-1]
        dd = diff if d == 0 else -diff
        pos = idx if d == 0 else (CHUNK - 1.0 - idx)
        keep = dd >= 0
        for h in range(RET_HEADS):
            lg = lg_d[h]
            decay.append(jnp.where(keep, jnp.exp(jnp.where(keep, dd, 0.0) * lg), 0.0))
            inner.append(jnp.broadcast_to(jnp.exp((pos + 1.0) * lg)[:, None], (CHUNK, LANE)))
            zeta.append(jnp.exp((CHUNK - 1.0 - pos) * lg))
    lg_np = np.log1p(-np.exp2(-RET_DECAY_BASE - np.arange(RET_HEADS, dtype=np.float32))).astype(np.float32)
    chunk_decay = tuple(float(np.exp(np.float32(CHUNK) * lg)) for lg in list(lg_np) + list(lg_np[::-1]))
    return jnp.stack(decay), jnp.stack(inner), jnp.stack(zeta), chunk_decay


def _mla_project(cq_ref, ckv_ref, kr_ref, gq_ref, gkv_ref, wq_ref, wkv_ref, cos_ref, sin_ref,
                 q_out, k_out, v_out):
    tm = cq_ref.shape[0]
    cos4 = cos_ref[...]
    sin4 = sin_ref[...]
    qn = _rmsnorm(cq_ref[...].astype(F32), gq_ref[...]).astype(BF16)
    kvn = _rmsnorm(ckv_ref[...].astype(F32), gkv_ref[...]).astype(BF16)
    q_all = jnp.dot(qn, wq_ref[...], preferred_element_type=F32)
    kv_all = jnp.dot(kvn, wkv_ref[...], preferred_element_type=F32)
    q_scale = (MLA_QK ** -0.5) * math.log2(math.e)
    nope_w = MLA_HEADS * MLA_NOPE
    q_rope = _rope_grouped(q_all[:, nope_w:], jnp.concatenate([cos4, cos4], axis=1),
                           jnp.concatenate([sin4, sin4], axis=1)) * q_scale
    k_rope = _rope_grouped(kr_ref[...].astype(F32), cos4[:, :LANE], sin4[:, :LANE])[:, :ROPE_DIM].astype(BF16)
    ones = jnp.ones((tm, MLA_V), BF16)
    for h in range(MLA_HEADS):
        q_out[h, :, :MLA_NOPE] = (q_all[:, h * MLA_NOPE:(h + 1) * MLA_NOPE] * q_scale).astype(BF16)
        q_out[h, :, MLA_NOPE:] = q_rope[:, h * ROPE_DIM:(h + 1) * ROPE_DIM].astype(BF16)
        kv0 = h * (MLA_NOPE + MLA_V)
        k_out[h, :, :MLA_NOPE] = kv_all[:, kv0:kv0 + MLA_NOPE].astype(BF16)
        k_out[h, :, MLA_NOPE:] = k_rope
        v_out[h, :, :MLA_V] = kv_all[:, kv0 + MLA_NOPE:kv0 + MLA_NOPE + MLA_V].astype(BF16)
        v_out[h, :, MLA_V:] = ones


def _pack_w_in_block(w_ref, o_ref):
    n_in, tc = w_ref.shape
    ml_end = 4 * ML_HEADS * ML_DIM
    gates = 4 * ML_HEADS
    rest = n_in - ml_end - gates
    gate_row = IN_WIDTH_PACKED - LANE
    assert ml_end + rest + (LANE - ROPE_DIM) == gate_row
    o_ref[:ml_end, :] = w_ref[:ml_end, :].astype(BF16)
    o_ref[ml_end:ml_end + rest, :] = w_ref[ml_end + gates:, :].astype(BF16)
    o_ref[ml_end + rest:gate_row, :] = jnp.zeros((LANE - ROPE_DIM, tc), BF16)
    o_ref[gate_row:gate_row + gates, :] = w_ref[ml_end:ml_end + gates, :].astype(BF16)
    o_ref[gate_row + gates:, :] = jnp.zeros((LANE - gates, tc), BF16)


def _pack_w_in(w_in_t, n_layers, *, tc):
    _, n_in, K = w_in_t.shape
    return pl.pallas_call(
        _pack_w_in_block,
        name="pack_w_in",
        grid=(n_layers, K // tc),
        in_specs=[pl.BlockSpec((None, n_in, tc), lambda l, i: (l, 0, i))],
        out_specs=pl.BlockSpec((None, IN_WIDTH_PACKED, tc), lambda l, i: (l, 0, i)),
        out_shape=jax.ShapeDtypeStruct((n_layers, IN_WIDTH_PACKED, K), BF16),
        compiler_params=_params(("parallel", "parallel")),
    )(w_in_t)


def _attn_kernel(q_ref, k_ref, v_ref, *rest, tk, nk, n_cast, pack):
    n_in = n_cast + pack
    w_refs, o_ref, wb_refs = rest[:n_in], rest[n_in], rest[n_in + 1:]
    for w_ref, wb_ref in zip(w_refs[:n_cast], wb_refs[:n_cast]):
        wb_ref[...] = w_ref[...].astype(wb_ref.dtype)
    if pack:
        _pack_w_in_block(w_refs[n_cast], wb_refs[n_cast])

    q = q_ref[...]
    m = acc = None
    for c in range(nk):
        k = k_ref[c * tk:(c + 1) * tk, :]
        v = v_ref[c * tk:(c + 1) * tk, :]
        s = lax.dot_general(q, k, _NT, preferred_element_type=F32)
        m_c = jnp.max(s, axis=-1, keepdims=True)
        if c == 0:
            m = m_c
            acc = jnp.dot(jnp.exp2(s - m).astype(BF16), v, preferred_element_type=F32)
        else:
            m_new = jnp.maximum(m, m_c)
            acc = jnp.exp2(m - m_new) * acc + jnp.dot(jnp.exp2(s - m_new).astype(BF16), v, preferred_element_type=F32)
            m = m_new
    o_ref[...] = (acc[:, :MLA_V] / acc[:, MLA_V:]).astype(o_ref.dtype)


def _attention_steps(S, tq):
    return MLA_HEADS * (S // tq)


def _attention(q, k, v, cast_weights, layer, pack_src, *, tq, tk, tc):
    S = q.shape[1]
    nq = S // tq
    n_steps = _attention_steps(S, tq)
    step = lambda h, i: h * nq + i
    w_specs, wb_specs, wb_shapes = [], [], []
    for w in cast_weights:
        _, rows, cols = w.shape
        rb = rows // n_steps
        assert rb * n_steps == rows and rb % BF16_ROWS == 0
        w_specs.append(pl.BlockSpec((None, rb, cols), lambda h, i: (layer, step(h, i), 0)))
        wb_specs.append(pl.BlockSpec((rb, cols), lambda h, i: (step(h, i), 0)))
        wb_shapes.append(jax.ShapeDtypeStruct((rows, cols), BF16))
    operands = list(cast_weights)
    if pack_src is not None:
        L, n_in, K = pack_src.shape
        kb = K // tc
        n_blocks = (L - 1) * kb
        assert n_blocks <= n_steps
        blk = lambda h, i: jnp.minimum(step(h, i), n_blocks - 1)
        w_specs.append(pl.BlockSpec((None, n_in, tc), lambda h, i: (1 + blk(h, i) // kb, 0, blk(h, i) % kb)))
        wb_specs.append(pl.BlockSpec((None, IN_WIDTH_PACKED, tc), lambda h, i: (blk(h, i) // kb, 0, blk(h, i) % kb)))
        wb_shapes.append(jax.ShapeDtypeStruct((L - 1, IN_WIDTH_PACKED, K), BF16))
        operands.append(pack_src)
    outs = pl.pallas_call(
        functools.partial(_attn_kernel, tk=tk, nk=S // tk, n_cast=len(cast_weights), pack=pack_src is not None),
        name="attention",
        grid=(MLA_HEADS, nq),
        in_specs=[
            pl.BlockSpec((None, tq, MLA_QK), lambda h, i: (h, i, 0)),
            pl.BlockSpec((None, S, MLA_QK), lambda h, i: (h, 0, 0)),
            pl.BlockSpec((None, S, 2 * MLA_V), lambda h, i: (h, 0, 0)),
        ] + w_specs,
        out_specs=[pl.BlockSpec((tq, MLA_V), lambda h, i: (i, h))] + wb_specs,
        out_shape=[jax.ShapeDtypeStruct((S, MLA_HEADS * MLA_V), BF16)] + wb_shapes,
        compiler_params=_params(("arbitrary", "arbitrary")),
    )(q, k, v, *operands)
    return outs[0], outs[1:]


def _tiles(S, D, F):
    cap = lambda n: min(n, S)
    return dict(
        row=cap(512),
        row_big=cap(1024),
        mixer_cps=4 if S % (4 * CHUNK) == 0 else 1,
        attn_tq=cap(2048), attn_tk=cap(512),
        ffn1_tm=cap(2048), ffn1_tn=min(1024, F),
        ffn2_tn=min(1024, D), ffn2_tk=min(4096, F),
        ffn2_final_tk=min(2048, F),
        pack_tc=min(256, D),
    )


def kernel(x, positions, g_mix, w_in, b_gates, w_conv, g_ml_out, g_ret_out, g_q_norm, w_q_up, g_kv_norm, w_kv_up,
           w_out, g_ffn, w_ff1, w_ff2, g_final):
    B, S, D = x.shape
    assert B == 1
    depth = w_in.shape[0]
    t = _tiles(S, D, w_ff1.shape[-1])

    half = ROPE_DIM // 2
    lane = jnp.arange(RET_HEADS * ROPE_DIM)
    inv = ROPE_THETA ** (-(lane % half).astype(F32) / half)
    ang = positions[0].astype(F32)[:, None] * inv
    cos4 = jnp.cos(ang)
    sin4 = jnp.where((lane % ROPE_DIM) < half, -jnp.sin(ang), jnp.sin(ang))

    w_in_t = jnp.swapaxes(w_in, 1, 2)
    ride_pack = depth > 1 and (depth - 1) * (D // t["pack_tc"]) <= _attention_steps(S, t["attn_tq"])
    w_in_first = _pack_w_in(w_in_t, 1 if ride_pack else depth, tc=t["pack_tc"])
    w_in_rest = None
    q_rank = w_q_up.shape[1]
    w_q4 = w_q_up.astype(BF16).reshape(depth, q_rank, MLA_HEADS, MLA_QK)
    w_q = jnp.concatenate([w_q4[..., :MLA_NOPE].reshape(depth, q_rank, MLA_HEADS * MLA_NOPE),
                           w_q4[..., MLA_NOPE:].reshape(depth, q_rank, MLA_HEADS * ROPE_DIM)], axis=-1)
    w_kv = w_kv_up.astype(BF16)
    bias_t = jnp.broadcast_to(b_gates[:, :, None], (depth, GATE_ROWS, CHUNK))
    idx = jnp.arange(CHUNK)
    tri = jnp.stack([idx[:, None] >= idx[None, :], idx[:, None] <= idx[None, :]]).astype(BF16)
    sel = _gate_select_matrix()
    ret_consts = _retention_consts()
    r3 = lambda g: g[:, None, :]

    xs = x[0]
    for l in range(depth):
        w_in_l, l_in = (w_in_first, l) if (l == 0 or not ride_pack) else (w_in_rest, l - 1)
        proj, gates_t, mq, mkt, rq, rkt = _in_proj(xs, r3(g_mix), w_in_l, w_conv, cos4, sin4, l, l_in,
                                                   tm=t["row_big"])
        h_f, h_b, y_f, y_b, q, k, v = _mixers(mq, mkt, rq, rkt, proj, gates_t, bias_t, l, tri, sel, ret_consts,
                                              r3(g_q_norm), r3(g_kv_norm), w_q, w_kv, cos4, sin4, cps=t["mixer_cps"])
        y_mla, prepared = _attention(q, k, v, (w_out, w_ff1, w_ff2), l, w_in_t if (ride_pack and l == 0) else None,
                                     tq=t["attn_tq"], tk=t["attn_tk"], tc=t["pack_tc"])
        w_out_b, w_ff1_b, w_ff2_b = prepared[:3]
        if ride_pack and l == 0:
            w_in_rest = prepared[3]
        xs, hn = _out_proj(h_f, h_b, y_f, y_b, proj, y_mla, r3(g_ml_out), r3(g_ret_out), r3(g_ffn), l, w_out_b, xs,
                           tm=t["row"])
        act = _ffn1(hn, w_ff1_b, tm=t["ffn1_tm"], tn=t["ffn1_tn"])
        if l < depth - 1:
            xs = _ffn2(act, w_ff2_b, xs, g_final[None, :], tm=t["row_big"], tn=t["ffn2_tn"], tk=t["ffn2_tk"],
                       final=False)
        else:
            xs = _ffn2(act, w_ff2_b, xs, g_final[None, :], tm=t["row_big"], tn=D, tk=t["ffn2_final_tk"], final=True)
    return xs[None]
```
